```python
import jax
import jax.numpy as jnp
from jax import lax
import numpy as np

D_MODEL = 1024
BATCH = 2
SEQ = 8192
DEPTH = 2

PLE_DIM = 256
N_DIR = 2
RWKV_WIDTH = D_MODEL // 2
RWKV_HEAD = 64
RWKV_HEADS = RWKV_WIDTH // RWKV_HEAD
DECAY_RANK = 64
ICLR_RANK = 64
VRES_RANK = 32
ATTN_WIDTH = D_MODEL // 2
ATTN_HEAD = 64
ATTN_Q_HEADS = ATTN_WIDTH // ATTN_HEAD
ATTN_KV_HEADS = 2
ATTN_GROUP = ATTN_Q_HEADS // ATTN_KV_HEADS
KV_WIDTH = ATTN_KV_HEADS * ATTN_HEAD
WINDOW = 128
BLOCK = 128
RMS_EPS = 1e-6
GN_EPS = 64e-5
NEG_INF = -1e30

A_COLS = 4 * RWKV_WIDTH + N_DIR * (DECAY_RANK + ICLR_RANK)
B_COLS = 2 * ATTN_WIDTH + 2 * KV_WIDTH
G_COLS = 2 * D_MODEL
IN_COLS = A_COLS + B_COLS + G_COLS

kernel_name = 'hybrid_rwkv7_swa_gated_encoder'


def rms_norm(x, g):
    xf = x.astype(jnp.float32)
    y = xf * lax.rsqrt(jnp.mean(xf * xf, axis=-1, keepdims=True) + RMS_EPS)
    return (y * g.astype(jnp.float32)).astype(x.dtype)


def centred_shift(u, mu):
    zero = jnp.zeros_like(u[:, :1])
    prev = jnp.concatenate([zero, u[:, :-1]], axis=1)
    nxt = jnp.concatenate([u[:, 1:], zero], axis=1)
    return u + mu * (0.5 * (prev + nxt) - u)


def stack_dirs(t):
    return jnp.stack([t, jnp.flip(t, axis=1)])


def flip_backward(t):
    return jnp.stack([t[0], jnp.flip(t[1], axis=1)])


def wkv7_scan(r, w, k, v, a, b):
    def step(S, inp):
        r_t, w_t, k_t, v_t, a_t, b_t = inp
        sa = jnp.einsum('dbhvk,dbhk->dbhv', S, a_t)
        S = S * w_t[..., None, :] + sa[..., None] * b_t[..., None, :] + v_t[..., None] * k_t[..., None, :]
        return S, jnp.einsum('dbhvk,dbhk->dbhv', S, r_t)
    xs = tuple(jnp.moveaxis(t, 2, 0) for t in (r, w, k, v, a, b))
    n_dir, bsz, _, h, n = r.shape
    S0 = jnp.zeros((n_dir, bsz, h, n, n), jnp.float32)
    _, ys = lax.scan(step, S0, xs)
    return jnp.moveaxis(ys, 0, 2)


def rwkv7_mixer(ua, h, v_first, shift_mu, decay_w0, decay_up, iclr_a0, iclr_up,
                vres, k_k, k_a, r_k, ln_w, ln_b):
    f32 = jnp.float32
    bsz, T, _ = ua.shape
    W, H, N = RWKV_WIDTH, RWKV_HEADS, RWKV_HEAD
    ua = centred_shift(ua, shift_mu).astype(f32)
    r, k, v, z = (ua[..., j * W:(j + 1) * W] for j in range(4))
    low = ua[..., 4 * W:]
    dec_down = low[..., :N_DIR * DECAY_RANK].reshape(bsz, T, N_DIR, DECAY_RANK)
    icl_down = low[..., N_DIR * DECAY_RANK:].reshape(bsz, T, N_DIR, ICLR_RANK)
    w_raw = decay_w0.astype(f32)[:, None, None, :] + jnp.einsum('btdr,drc->dbtc', jnp.tanh(dec_down), decay_up.astype(f32))
    decay = jnp.exp(-jnp.exp(-jax.nn.softplus(-w_raw) - 0.5))
    a = jax.nn.sigmoid(iclr_a0.astype(f32)[:, None, None, :] + jnp.einsum('btdr,drc->dbtc', icl_down, iclr_up.astype(f32)))
    if vres is not None:
        vd, vu, v0 = vres
        mix = jax.nn.sigmoid(v0.astype(f32) + (h.astype(f32) @ vd.astype(f32)) @ vu.astype(f32))
        v = v + (v_first - v) * mix
    kk = (k * k_k.astype(f32)).reshape(bsz, T, H, N)
    kk = (kk / jnp.maximum(jnp.linalg.norm(kk, axis=-1, keepdims=True), 1e-12)).reshape(bsz, T, W)
    k_dir = k[None] * (1.0 + (a - 1.0) * k_a.astype(f32))
    heads = lambda t: t.reshape(t.shape[:3] + (H, N))
    y = wkv7_scan(heads(stack_dirs(r)), heads(flip_backward(decay)), heads(flip_backward(k_dir)),
                  heads(stack_dirs(v)), heads(stack_dirs(-kk)), heads(flip_backward(kk[None] * a)))
    y = flip_backward(y).sum(0)
    mean = jnp.mean(y, axis=-1, keepdims=True)
    var = jnp.mean(jnp.square(y - mean), axis=-1, keepdims=True)
    y = ((y - mean) * lax.rsqrt(var + GN_EPS)).reshape(bsz, T, W) * ln_w.astype(f32) + ln_b.astype(f32)
    rh = r.reshape(bsz, T, H, N)
    ksum = k_dir.sum(0).reshape(bsz, T, H, N)
    bonus = (jnp.sum(rh * ksum * r_k.astype(f32), axis=-1, keepdims=True) * v.reshape(bsz, T, H, N)).reshape(bsz, T, W)
    o = (y + bonus) * jax.nn.silu(z)
    return o.astype(h.dtype), v


def alibi_slopes(n_heads):
    return jnp.asarray(2.0 ** (-8.0 * np.arange(1, n_heads + 1) / n_heads), dtype=jnp.float32)


def window_attention(ub, q_g, k_g, sink):
    f32 = jnp.float32
    bsz, T, _ = ub.shape
    nb = T // BLOCK
    q = ub[..., :ATTN_WIDTH].reshape(bsz, T, ATTN_Q_HEADS, ATTN_HEAD)
    k = ub[..., ATTN_WIDTH:ATTN_WIDTH + KV_WIDTH].reshape(bsz, T, ATTN_KV_HEADS, ATTN_HEAD)
    v = ub[..., ATTN_WIDTH + KV_WIDTH:ATTN_WIDTH + 2 * KV_WIDTH].reshape(bsz, T, ATTN_KV_HEADS, ATTN_HEAD)
    z = ub[..., ATTN_WIDTH + 2 * KV_WIDTH:]
    q = rms_norm(q, q_g).astype(f32) * (ATTN_HEAD ** -0.5)
    k = rms_norm(k, k_g).astype(f32)
    v = v.astype(f32)
    qb = q.reshape(bsz, nb, BLOCK, ATTN_KV_HEADS, ATTN_GROUP, ATTN_HEAD)

    def neighbours(t):
        tp = jnp.pad(t.reshape(bsz, nb, BLOCK, ATTN_KV_HEADS, ATTN_HEAD), ((0, 0), (1, 1), (0, 0), (0, 0), (0, 0)))
        return jnp.concatenate([tp[:, :-2], tp[:, 1:-1], tp[:, 2:]], axis=2)

    kw, vw = neighbours(k), neighbours(v)
    s = jnp.einsum('bnqgrd,bnkgd->bngrqk', qb, kw)
    blk = jnp.arange(nb)[:, None]
    q_pos = blk * BLOCK + jnp.arange(BLOCK)[None, :]
    k_pos = (blk - 1) * BLOCK + jnp.arange(3 * BLOCK)[None, :]
    dist = jnp.abs(q_pos[:, :, None] - k_pos[:, None, :])
    valid = (dist <= WINDOW) & (k_pos[:, None, :] >= 0) & (k_pos[:, None, :] < T)
    slopes = alibi_slopes(ATTN_Q_HEADS).reshape(ATTN_KV_HEADS, ATTN_GROUP)
    s = s - slopes[None, None, :, :, None, None] * dist.astype(f32)[None, :, None, None]
    s = jnp.where(valid[None, :, None, None], s, NEG_INF)
    sk = sink.astype(f32).reshape(ATTN_KV_HEADS, ATTN_GROUP)[None, None, :, :, None, None]
    m = jnp.maximum(jnp.max(s, axis=-1, keepdims=True), sk)
    e = jnp.exp(s - m)
    pr = e / (jnp.sum(e, axis=-1, keepdims=True) + jnp.exp(sk - m))
    o = jnp.einsum('bngrqk,bnkgd->bnqgrd', pr, vw).reshape(bsz, T, ATTN_WIDTH)
    return o.astype(ub.dtype) * jax.nn.silu(z)


def setup_inputs(seed: int = 0) -> dict:
    key = jax.random.key(seed)
    ks = jax.random.split(key, 26)
    f32 = jnp.float32
    nrm = lambda k, shape, scale: scale * jax.random.normal(k, shape, f32)
    lin = jnp.arange(RWKV_WIDTH, dtype=f32) / (RWKV_WIDTH - 1)
    w0_base = -6.5 + 5.0 * lin ** 1.5
    nv = DEPTH - 1
    return {
        'x': nrm(ks[0], (BATCH, SEQ, D_MODEL), 1.0),
        'p': nrm(ks[1], (DEPTH, BATCH, SEQ, PLE_DIM), 1.0),
        'norm_g': 1.0 + nrm(ks[2], (DEPTH, D_MODEL), 0.05),
        'w_in': nrm(ks[3], (DEPTH, D_MODEL, IN_COLS), D_MODEL ** -0.5),
        'shift_mu': jax.random.uniform(ks[4], (DEPTH, A_COLS), f32, 0.1, 0.9),
        'decay_w0': w0_base + nrm(ks[5], (DEPTH, N_DIR, RWKV_WIDTH), 0.1),
        'decay_up': nrm(ks[6], (DEPTH, N_DIR, DECAY_RANK, RWKV_WIDTH), 0.5 * DECAY_RANK ** -0.5),
        'iclr_a0': nrm(ks[7], (DEPTH, N_DIR, RWKV_WIDTH), 0.1),
        'iclr_up': nrm(ks[8], (DEPTH, N_DIR, ICLR_RANK, RWKV_WIDTH), 0.5 * ICLR_RANK ** -0.5),
        'vres_down': nrm(ks[9], (nv, D_MODEL, VRES_RANK), D_MODEL ** -0.5),
        'vres_up': nrm(ks[10], (nv, VRES_RANK, RWKV_WIDTH), 0.5 * VRES_RANK ** -0.5),
        'vres_v0': 1.0 + nrm(ks[11], (nv, RWKV_WIDTH), 0.1),
        'k_k': 0.85 + nrm(ks[12], (DEPTH, RWKV_WIDTH), 0.02),
        'k_a': 1.0 + nrm(ks[13], (DEPTH, RWKV_WIDTH), 0.02),
        'r_k': nrm(ks[14], (DEPTH, RWKV_HEADS, RWKV_HEAD), 0.05),
        'ln_x_w': 1.0 + nrm(ks[15], (DEPTH, RWKV_WIDTH), 0.05),
        'ln_x_b': nrm(ks[16], (DEPTH, RWKV_WIDTH), 0.02),
        'q_norm_g': 1.0 + nrm(ks[17], (DEPTH, ATTN_HEAD), 0.05),
        'k_norm_g': 1.0 + nrm(ks[18], (DEPTH, ATTN_HEAD), 0.05),
        'sink': nrm(ks[19], (DEPTH, ATTN_Q_HEADS), 0.5),
        'proj_a': nrm(ks[20], (DEPTH, RWKV_WIDTH, D_MODEL), RWKV_WIDTH ** -0.5),
        'proj_b': nrm(ks[21], (DEPTH, ATTN_WIDTH, D_MODEL), ATTN_WIDTH ** -0.5),
        'w_out': nrm(ks[22], (DEPTH, D_MODEL, D_MODEL), 0.5 * D_MODEL ** -0.5),
        'ple_norm_g': 1.0 + nrm(ks[23], (DEPTH, D_MODEL), 0.05),
        'ple_gate_w': nrm(ks[24], (DEPTH, D_MODEL, D_MODEL), D_MODEL ** -0.5),
        'ple_proj': nrm(ks[25], (DEPTH, PLE_DIM, D_MODEL), PLE_DIM ** -0.5),
    }


def reference(x, p, norm_g, w_in, shift_mu, decay_w0, decay_up, iclr_a0, iclr_up,
              vres_down, vres_up, vres_v0, k_k, k_a, r_k, ln_x_w, ln_x_b,
              q_norm_g, k_norm_g, sink, proj_a, proj_b, w_out,
              ple_norm_g, ple_gate_w, ple_proj):
    v_first = None
    for i in range(DEPTH):
        h = rms_norm(x, norm_g[i])
        u = h @ w_in[i]
        ua = u[..., :A_COLS]
        ub = u[..., A_COLS:A_COLS + B_COLS]
        ug = u[..., A_COLS + B_COLS:]
        vres = None if i == 0 else (vres_down[i - 1], vres_up[i - 1], vres_v0[i - 1])
        o_a, v_a = rwkv7_mixer(ua, h, v_first, shift_mu[i], decay_w0[i], decay_up[i], iclr_a0[i],
                               iclr_up[i], vres, k_k[i], k_a[i], r_k[i], ln_x_w[i], ln_x_b[i])
        if i == 0:
            v_first = v_a
        o_b = window_attention(ub, q_norm_g[i], k_norm_g[i], sink[i])
        y_a = o_a @ proj_a[i]
        y_b = o_b @ proj_b[i]
        merged = jax.nn.sigmoid(ug[..., :D_MODEL]) * y_a + jax.nn.sigmoid(ug[..., D_MODEL:]) * y_b
        x = x + merged @ w_out[i]
        ple = p[i] @ ple_proj[i]
        x = x + jax.nn.sigmoid(rms_norm(x, ple_norm_g[i]) @ ple_gate_w[i]) * ple
    return x
```

```python
import functools

import jax
import jax.numpy as jnp
from jax import lax
from jax.experimental import pallas as pl
from jax.experimental.pallas import tpu as pltpu

F32 = jnp.float32
BF16 = jnp.bfloat16

D_MODEL = 1024
PLE_DIM = 256
RWKV_WIDTH = 512
HEAD = 64
RWKV_HEADS = RWKV_WIDTH // HEAD
LOW_RANK = 64
VRES_RANK = 32
ATTN_WIDTH = 512
ATTN_Q_HEADS = ATTN_WIDTH // HEAD
ATTN_KV_HEADS = 2
ATTN_GROUP = ATTN_Q_HEADS // ATTN_KV_HEADS
KV_WIDTH = ATTN_KV_HEADS * HEAD
WINDOW = 128
BLOCK = 128
RMS_EPS = 1e-6
GN_EPS = 64e-5
NEG_INF = -1e30
A_COLS = 4 * RWKV_WIDTH + 4 * LOW_RANK
B_COLS = 2 * ATTN_WIDTH + 2 * KV_WIDTH
G_COLS = 2 * D_MODEL
LANES = 128
SUBLANES = 8
CHUNK = 64
INV_BASE = 8
VMEM_LIMIT = 48 * 1024 * 1024

NN = ((1,), (0,))
NT = ((1,), (1,))
TN = ((0,), (0,))


def _dot(a, b, dims=NN):
    return lax.dot_general(a, b, (dims, ((), ())), preferred_element_type=F32)


def _split2(x):
    hi = x.astype(BF16)
    lo = (x - hi.astype(F32)).astype(BF16)
    return hi, lo


def _dot3(a, b, dims=NN):
    ah, al = _split2(a)
    bh, bl = _split2(b)
    return _dot(ah, bh, dims) + (_dot(ah, bl, dims) + _dot(al, bh, dims))


def _dot_exact_lhs(a_bf16, b, dims=NN):
    b1 = b.astype(BF16)
    r1 = b - b1.astype(F32)
    b2 = r1.astype(BF16)
    b3 = (r1 - b2.astype(F32)).astype(BF16)
    return _dot(a_bf16, b1, dims) + (_dot(a_bf16, b2, dims) + _dot(a_bf16, b3, dims))


def _dot_exact_rhs(a, b_bf16):
    a1 = a.astype(BF16)
    r1 = a - a1.astype(F32)
    a2 = r1.astype(BF16)
    a3 = (r1 - a2.astype(F32)).astype(BF16)
    return _dot(a1, b_bf16) + (_dot(a2, b_bf16) + _dot(a3, b_bf16))


def _sigmoid(x):
    return 1.0 / (1.0 + jnp.exp(-x))


def _head_sum_matrix(width):
    r = lax.broadcasted_iota(jnp.int32, (width, width), 0) // HEAD
    c = lax.broadcasted_iota(jnp.int32, (width, width), 1) // HEAD
    return (r == c).astype(BF16)


def _cparams(sem):
    return pltpu.CompilerParams(dimension_semantics=sem, vmem_limit_bytes=VMEM_LIMIT)


def _proj_in_body(x_ref, g_ref, w_ref, *out_refs):
    x = x_ref[...]
    h = x * lax.rsqrt(jnp.mean(x * x, axis=-1, keepdims=True) + RMS_EPS) * g_ref[...]
    hb = h.astype(BF16)
    off = 0
    for o_ref in out_refs:
        n = o_ref.shape[-1]
        o_ref[...] = _dot(hb, w_ref[:, off:off + n])
        off += n


def _proj_in(x2, g, w_bf16, splits, tm=256):
    rows = x2.shape[0]
    ncol = w_bf16.shape[1]
    assert sum(splits) == ncol and rows % tm == 0
    return pl.pallas_call(
        _proj_in_body,
        grid=(rows // tm,),
        in_specs=[
            pl.BlockSpec((tm, D_MODEL), lambda i: (i, 0)),
            pl.BlockSpec((1, D_MODEL), lambda i: (0, 0)),
            pl.BlockSpec((D_MODEL, ncol), lambda i: (0, 0)),
        ],
        out_specs=[pl.BlockSpec((tm, n), lambda i: (i, 0)) for n in splits],
        out_shape=[jax.ShapeDtypeStruct((rows, n), F32) for n in splits],
        compiler_params=_cparams(("parallel",)),
        name="proj_in",
    )(x2, g, w_bf16)


def _rwkv_prep_body(*refs, has_vres):
    if has_vres:
        (ua_ref, prev_ref, next_ref, mu_ref, w0_ref, dup_ref, a0_ref, iup_ref, kk_ref, ka_ref, rk_ref,
         hd_ref, vu_ref, v0_ref, vf_ref,
         r_o, v_o, an_o, lw_o, kd_o, bb_o, bonus_o, gz_o) = refs
    else:
        (ua_ref, prev_ref, next_ref, mu_ref, w0_ref, dup_ref, a0_ref, iup_ref, kk_ref, ka_ref, rk_ref,
         r_o, v_o, an_o, lw_o, kd_o, bb_o, bonus_o, gz_o) = refs
    tt = ua_ref.shape[0]
    i = pl.program_id(1)
    last = pl.num_programs(1) - 1
    row = lax.broadcasted_iota(jnp.int32, (tt, 1), 0)
    hsum = _head_sum_matrix(RWKV_WIDTH)

    def shifted(lo, hi):
        u = ua_ref[:, lo:hi]
        p_row = jnp.where(i == 0, 0.0, prev_ref[SUBLANES - 1:SUBLANES, lo:hi])
        n_row = jnp.where(i == last, 0.0, next_ref[0:1, lo:hi])
        prev = jnp.where(row == 0, p_row, pltpu.roll(u, 1, axis=0))
        nxt = jnp.where(row == tt - 1, n_row, pltpu.roll(u, tt - 1, axis=0))
        return u + mu_ref[:, lo:hi] * (0.5 * (prev + nxt) - u)

    W = RWKV_WIDTH
    r = shifted(0, W)
    k = shifted(W, 2 * W)
    v = shifted(2 * W, 3 * W)
    z = shifted(3 * W, 4 * W)
    low = shifted(4 * W, 4 * W + 4 * LOW_RANK)
    w_raw = w0_ref[...] + _dot3(jnp.tanh(low[:, :2 * LOW_RANK]), dup_ref[...])
    nw = -w_raw
    softplus = jnp.maximum(nw, 0.0) + jnp.log(1.0 + jnp.exp(-jnp.abs(nw)))
    lw = -jnp.exp(-softplus - 0.5)
    a = _sigmoid(a0_ref[...] + _dot3(low[:, 2 * LOW_RANK:], iup_ref[...]))
    if has_vres:
        mix = _sigmoid(v0_ref[...] + _dot3(hd_ref[...], vu_ref[...]))
        v = v + (vf_ref[...] - v) * mix
    kk = k * kk_ref[...]
    ss = _dot_exact_rhs(kk * kk, hsum)
    kk = kk / jnp.maximum(jnp.sqrt(ss), 1e-12)
    ka = ka_ref[...]
    ksum = jnp.zeros_like(k)
    for d in range(2):
        a_d = a[:, d * W:(d + 1) * W]
        kd = k * (1.0 + (a_d - 1.0) * ka)
        ksum = ksum + kd
        lw_o[d] = lw[:, d * W:(d + 1) * W]
        kd_o[d] = kd
        bb_o[d] = kk * a_d
    r_o[...] = r
    v_o[...] = v
    an_o[...] = -kk
    bonus_o[...] = _dot_exact_rhs(r * ksum * rk_ref[...], hsum) * v
    gz_o[...] = z * _sigmoid(z)


def _rwkv_prep(ua, params, vres, tt=256):
    bsz, T, _ = ua.shape
    W = RWKV_WIDTH
    nt = T // tt
    hb = tt // SUBLANES
    n8 = T // SUBLANES
    full = lambda shape: pl.BlockSpec(shape, lambda b, i: (0,) * len(shape))
    tile = lambda n: pl.BlockSpec((None, tt, n), lambda b, i: (b, i, 0))
    in_specs = [
        tile(A_COLS),
        pl.BlockSpec((None, SUBLANES, A_COLS), lambda b, i: (b, jnp.maximum(i * hb - 1, 0), 0)),
        pl.BlockSpec((None, SUBLANES, A_COLS), lambda b, i: (b, jnp.minimum((i + 1) * hb, n8 - 1), 0)),
        full((1, A_COLS)), full((1, 2 * W)), full((2 * LOW_RANK, 2 * W)), full((1, 2 * W)),
        full((2 * LOW_RANK, 2 * W)), full((1, W)), full((1, W)), full((1, W)),
    ]
    args = [ua, ua, ua] + list(params)
    if vres is not None:
        hd, vu_pad, v0, v_first = vres
        in_specs += [tile(LANES), full((LANES, W)), full((1, W)), tile(W)]
        args += [hd, vu_pad, v0, v_first]
    dir_tile = pl.BlockSpec((2, None, tt, W), lambda b, i: (0, b, i, 0))
    out_specs = [tile(W), tile(W), tile(W), dir_tile, dir_tile, dir_tile, tile(W), tile(W)]
    one = jax.ShapeDtypeStruct((bsz, T, W), F32)
    two = jax.ShapeDtypeStruct((2, bsz, T, W), F32)
    return pl.pallas_call(
        functools.partial(_rwkv_prep_body, has_vres=vres is not None),
        grid=(bsz, nt),
        in_specs=in_specs,
        out_specs=out_specs,
        out_shape=[one, one, one, two, two, two, one, one],
        compiler_params=_cparams(("parallel", "parallel")),
        name="rwkv_prep",
    )(*args)


def _unit_lower_inverse(m, same_blk, eye):
    md = m * same_blk[0]
    m2 = _dot3(md, md)
    x = eye + md
    x = x + _dot3(x, m2)
    m4 = _dot3(m2, m2)
    x = x + _dot3(x, m4)
    for lvl in range(len(same_blk)):
        inner = same_blk[lvl]
        outer = same_blk[lvl + 1] if lvl + 1 < len(same_blk) else 1.0
        off = m * (outer - inner)
        x = x + _dot3(_dot3(x, off), x)
    return x


def _wkv_scan_body(r_ref, v_ref, an_ref, lw_ref, kd_ref, bb_ref, y_ref, s_ref):
    d = pl.program_id(0)
    c = pl.program_id(2)
    C = r_ref.shape[0]

    @pl.when(c == 0)
    def _():
        s_ref[...] = jnp.zeros_like(s_ref)

    row = lax.broadcasted_iota(jnp.int32, (C, C), 0)
    col = lax.broadcasted_iota(jnp.int32, (C, C), 1)
    ahead = (row - col) * (1 - 2 * d)
    incl_b = ahead >= 0
    incl = incl_b.astype(F32)
    strict = (ahead > 0).astype(F32)
    eye = (row == col).astype(F32)
    same_blk = []
    size = INV_BASE
    while size < C:
        same_blk.append(((row // size) == (col // size)).astype(F32))
        size *= 2

    lw = lw_ref[...]
    cs = _dot_exact_lhs(incl_b.astype(BF16), lw)
    tot = jnp.sum(lw, axis=0, keepdims=True)
    e_incl = jnp.exp(cs)
    e_excl = jnp.exp(cs - lw)
    e_inv = jnp.exp(-cs)
    p_tot = jnp.exp(tot)
    rt = r_ref[...] * e_incl
    at = an_ref[...] * e_excl
    bt = bb_ref[...] * e_inv
    kt = kd_ref[...] * e_inv
    bh = bt * p_tot
    kh = kt * p_tot
    vv = v_ref[...]

    for h in range(RWKV_HEADS):
        sl = slice(h * HEAD, (h + 1) * HEAD)
        at_h, rt_h, bt_h, kt_h, bh_h, kh_h, v_h = at[:, sl], rt[:, sl], bt[:, sl], kt[:, sl], bh[:, sl], kh[:, sl], vv[:, sl]
        m_ab = _dot3(at_h, bt_h, NT) * strict
        m_ak = _dot3(at_h, kt_h, NT) * strict
        n_rb = _dot3(rt_h, bt_h, NT) * incl
        n_rk = _dot3(rt_h, kt_h, NT) * incl
        t_inv = _unit_lower_inverse(m_ab, same_blk, eye)
        w_a = _dot3(t_inv, at_h)
        w_b = _dot3(t_inv, _dot3(m_ak, v_h))
        y_q = rt_h + _dot3(n_rb, w_a)
        y_c = _dot3(n_rb, w_b) + _dot3(n_rk, v_h)
        phi = _dot3(w_a, bh_h, TN)
        psi = _dot3(w_b, bh_h, TN) + _dot3(v_h, kh_h, TN)
        s0 = s_ref[h]
        y_ref[:, sl] = _dot3(y_q, s0, NT) + y_c
        s_ref[h] = s0 * p_tot[:, sl] + _dot3(s0, phi) + psi


def _wkv_scan(r, v, an, lw, kd, bb):
    bsz, T, W = r.shape
    C = CHUNK
    nc = T // C
    shared = pl.BlockSpec((None, C, W), lambda d, b, c: (b, c + d * (nc - 1 - 2 * c), 0))
    per_dir = pl.BlockSpec((None, None, C, W), lambda d, b, c: (d, b, c + d * (nc - 1 - 2 * c), 0))
    return pl.pallas_call(
        _wkv_scan_body,
        grid=(2, bsz, nc),
        in_specs=[shared, shared, shared, per_dir, per_dir, per_dir],
        out_specs=per_dir,
        out_shape=jax.ShapeDtypeStruct((2, bsz, T, W), F32),
        scratch_shapes=[pltpu.VMEM((RWKV_HEADS, HEAD, HEAD), F32)],
        compiler_params=_cparams(("parallel", "parallel", "arbitrary")),
        name="wkv_scan",
    )(r, v, an, lw, kd, bb)


def _win_attn_body(q_ref, kp_ref, kc_ref, kn_ref, qg_ref, kg_ref, slope_ref, sink_ref, o_ref):
    n = pl.program_id(1)
    nb = pl.num_programs(1)
    hq = _head_sum_matrix(ATTN_WIDTH)
    hk = _head_sum_matrix(KV_WIDTH)

    q = q_ref[:, :ATTN_WIDTH]
    z = q_ref[:, ATTN_WIDTH + 2 * KV_WIDTH:]
    q = q * lax.rsqrt(_dot_exact_rhs(q * q, hq) * (1.0 / HEAD) + RMS_EPS) * qg_ref[...] * (HEAD ** -0.5)

    def kv_of(ref):
        kx = ref[:, ATTN_WIDTH:ATTN_WIDTH + KV_WIDTH]
        kx = kx * lax.rsqrt(_dot_exact_rhs(kx * kx, hk) * (1.0 / HEAD) + RMS_EPS) * kg_ref[...]
        return kx, ref[:, ATTN_WIDTH + KV_WIDTH:ATTN_WIDTH + 2 * KV_WIDTH]

    ks, vs = zip(*(kv_of(ref) for ref in (kp_ref, kc_ref, kn_ref)))
    qi = lax.broadcasted_iota(jnp.int32, (BLOCK, BLOCK), 0)
    ki = lax.broadcasted_iota(jnp.int32, (BLOCK, BLOCK), 1)
    dist, valid = [], []
    for j in range(3):
        dj = jnp.abs(qi - (ki + (j - 1) * BLOCK))
        ok = dj <= WINDOW
        if j == 0:
            ok = ok & (n > 0)
        if j == 2:
            ok = ok & (n < nb - 1)
        dist.append(dj.astype(F32))
        valid.append(ok)

    for h in range(ATTN_Q_HEADS):
        g = h // ATTN_GROUP
        qh = q[:, h * HEAD:(h + 1) * HEAD]
        slope = slope_ref[h]
        sink = sink_ref[h]
        s = []
        for j in range(3):
            sj = _dot(qh.astype(BF16), ks[j][:, g * HEAD:(g + 1) * HEAD].astype(BF16), NT)
            s.append(jnp.where(valid[j], sj - slope * dist[j], NEG_INF))
        m = jnp.maximum(jnp.maximum(jnp.max(s[0], axis=-1, keepdims=True), jnp.max(s[1], axis=-1, keepdims=True)),
                        jnp.maximum(jnp.max(s[2], axis=-1, keepdims=True), sink))
        e = [jnp.exp(sj - m) for sj in s]
        den = (jnp.sum(e[0], axis=-1, keepdims=True) + jnp.sum(e[1], axis=-1, keepdims=True)
               + jnp.sum(e[2], axis=-1, keepdims=True) + jnp.exp(sink - m))
        inv = 1.0 / den
        o = jnp.zeros((BLOCK, HEAD), F32)
        for j in range(3):
            o = o + _dot((e[j] * inv).astype(BF16), vs[j][:, g * HEAD:(g + 1) * HEAD].astype(BF16))
        zh = z[:, h * HEAD:(h + 1) * HEAD]
        o_ref[:, h * HEAD:(h + 1) * HEAD] = o * (zh * _sigmoid(zh))


def _win_attn(ub, q_g, k_g, slopes, sink):
    bsz, T, _ = ub.shape
    nb = T // BLOCK
    blk = lambda f: pl.BlockSpec((None, BLOCK, B_COLS), f)
    smem = pl.BlockSpec(memory_space=pltpu.SMEM)
    return pl.pallas_call(
        _win_attn_body,
        grid=(bsz, nb),
        in_specs=[
            blk(lambda b, n: (b, n, 0)),
            blk(lambda b, n: (b, jnp.maximum(n - 1, 0), 0)),
            blk(lambda b, n: (b, n, 0)),
            blk(lambda b, n: (b, jnp.minimum(n + 1, nb - 1), 0)),
            pl.BlockSpec((1, ATTN_WIDTH), lambda b, n: (0, 0)),
            pl.BlockSpec((1, KV_WIDTH), lambda b, n: (0, 0)),
            smem, smem,
        ],
        out_specs=pl.BlockSpec((None, BLOCK, ATTN_WIDTH), lambda b, n: (b, n, 0)),
        out_shape=jax.ShapeDtypeStruct((bsz, T, ATTN_WIDTH), F32),
        compiler_params=_cparams(("parallel", "parallel")),
        name="win_attn",
    )(ub, ub, ub, ub, q_g, k_g, slopes, sink)


def _merge_out_body(x_ref, y_ref, bonus_ref, gz_ref, ob_ref, ug_ref, p_ref, lnw_ref, lnb_ref,
                    pa_ref, pb_ref, wo_ref, pg_ref, gw_ref, pp_ref, o_ref):
    hsum = _head_sum_matrix(RWKV_WIDTH)
    y = y_ref[0] + y_ref[1]
    mean = _dot_exact_rhs(y, hsum) * (1.0 / HEAD)
    yc = y - mean
    var = _dot_exact_rhs(yc * yc, hsum) * (1.0 / HEAD)
    yn = yc * lax.rsqrt(var + GN_EPS) * lnw_ref[...] + lnb_ref[...]
    o_a = (yn + bonus_ref[...]) * gz_ref[...]
    y_a = _dot(o_a.astype(BF16), pa_ref[...])
    y_b = _dot(ob_ref[...].astype(BF16), pb_ref[...])
    merged = _sigmoid(ug_ref[:, :D_MODEL]) * y_a + _sigmoid(ug_ref[:, D_MODEL:]) * y_b
    x1 = x_ref[...] + _dot(merged.astype(BF16), wo_ref[...])
    ple = _dot(p_ref[...].astype(BF16), pp_ref[...])
    hn = x1 * lax.rsqrt(jnp.mean(x1 * x1, axis=-1, keepdims=True) + RMS_EPS) * pg_ref[...]
    o_ref[...] = x1 + _sigmoid(_dot(hn.astype(BF16), gw_ref[...])) * ple


def _merge_out(x2, y, bonus, gz, ob, ug, p2, lnw, lnb, pa, pb, wo, pg, gw, pp, tm=256):
    rows = x2.shape[0]
    W = RWKV_WIDTH
    tile = lambda n: pl.BlockSpec((tm, n), lambda i: (i, 0))
    full = lambda a: pl.BlockSpec(a.shape, lambda i: (0,) * a.ndim)
    return pl.pallas_call(
        _merge_out_body,
        grid=(rows // tm,),
        in_specs=[tile(D_MODEL), pl.BlockSpec((2, tm, W), lambda i: (0, i, 0)), tile(W), tile(W), tile(ATTN_WIDTH),
                  tile(G_COLS), tile(PLE_DIM), full(lnw), full(lnb), full(pa), full(pb), full(wo), full(pg),
                  full(gw), full(pp)],
        out_specs=tile(D_MODEL),
        out_shape=jax.ShapeDtypeStruct((rows, D_MODEL), F32),
        compiler_params=_cparams(("parallel",)),
        name="merge_out",
    )(x2, y, bonus, gz, ob, ug, p2, lnw, lnb, pa, pb, wo, pg, gw, pp)


def _block_diag2(m):
    z = jnp.zeros_like(m[0])
    return jnp.concatenate([jnp.concatenate([m[0], z], axis=1), jnp.concatenate([z, m[1]], axis=1)], axis=0)


def kernel(x, p, norm_g, w_in, shift_mu, decay_w0, decay_up, iclr_a0, iclr_up, vres_down, vres_up, vres_v0, k_k, k_a, r_k, ln_x_w, ln_x_b, q_norm_g, k_norm_g, sink, proj_a, proj_b, w_out, ple_norm_g, ple_gate_w, ple_proj):
    bsz, T, _ = x.shape
    depth = w_in.shape[0]
    rows = bsz * T
    W = RWKV_WIDTH
    slopes = jnp.asarray(2.0 ** (-8.0 * jnp.arange(1, ATTN_Q_HEADS + 1, dtype=F32) / ATTN_Q_HEADS), F32)
    x2 = x.reshape(rows, D_MODEL)
    v_first = None
    for i in range(depth):
        splits = [A_COLS, B_COLS, G_COLS]
        w_i = w_in[i]
        if i > 0:
            pad = jnp.zeros((D_MODEL, LANES - VRES_RANK), F32)
            w_i = jnp.concatenate([w_i, vres_down[i - 1], pad], axis=1)
            splits.append(LANES)
        outs = _proj_in(x2, norm_g[i].reshape(1, D_MODEL), w_i.astype(BF16), splits)
        ua = outs[0].reshape(bsz, T, A_COLS)
        ub = outs[1].reshape(bsz, T, B_COLS)
        ug = outs[2]
        params = [shift_mu[i].reshape(1, A_COLS), decay_w0[i].reshape(1, 2 * W), _block_diag2(decay_up[i]),
                  iclr_a0[i].reshape(1, 2 * W), _block_diag2(iclr_up[i]), k_k[i].reshape(1, W),
                  k_a[i].reshape(1, W), r_k[i].reshape(1, W)]
        vres = None
        if i > 0:
            vu_pad = jnp.concatenate([vres_up[i - 1], jnp.zeros((LANES - VRES_RANK, W), F32)], axis=0)
            vres = (outs[3].reshape(bsz, T, LANES), vu_pad, vres_v0[i - 1].reshape(1, W), v_first)
        r, v, an, lw, kd, bb, bonus, gz = _rwkv_prep(ua, params, vres)
        if i == 0:
            v_first = v
        y = _wkv_scan(r, v, an, lw, kd, bb)
        ob = _win_attn(ub, jnp.tile(q_norm_g[i], ATTN_Q_HEADS).reshape(1, ATTN_WIDTH),
                       jnp.tile(k_norm_g[i], ATTN_KV_HEADS).reshape(1, KV_WIDTH), slopes, sink[i])
        x2 = _merge_out(x2, y.reshape(2, rows, W), bonus.reshape(rows, W), gz.reshape(rows, W),
                        ob.reshape(rows, ATTN_WIDTH), ug, p[i].reshape(rows, PLE_DIM),
                        ln_x_w[i].reshape(1, W), ln_x_b[i].reshape(1, W), proj_a[i].astype(BF16),
                        proj_b[i].astype(BF16), w_out[i].astype(BF16), ple_norm_g[i].reshape(1, D_MODEL),
                        ple_gate_w[i].astype(BF16), ple_proj[i].astype(BF16))
    return x2.reshape(bsz, T, D_MODEL)
```

```python
import functools

import jax
import jax.numpy as jnp
from jax import lax
from jax.experimental import pallas as pl
from jax.experimental.pallas import tpu as pltpu

F32 = jnp.float32
BF16 = jnp.bfloat16

D_MODEL = 1024
PLE_DIM = 256
RWKV_WIDTH = 512
HEAD = 64
RWKV_HEADS = RWKV_WIDTH // HEAD
LOW_RANK = 64
VRES_RANK = 32
ATTN_WIDTH = 512
ATTN_Q_HEADS = ATTN_WIDTH // HEAD
ATTN_KV_HEADS = 2
ATTN_GROUP = ATTN_Q_HEADS // ATTN_KV_HEADS
KV_WIDTH = ATTN_KV_HEADS * HEAD
WINDOW = 128
BLOCK = 128
RMS_EPS = 1e-6
GN_EPS = 64e-5
NEG_INF = -1e30
A_COLS = 4 * RWKV_WIDTH + 4 * LOW_RANK
B_COLS = 2 * ATTN_WIDTH + 2 * KV_WIDTH
G_COLS = 2 * D_MODEL
LANES = 128
SUBLANES = 8
CHUNK = 64
INV_BASE = 8
WKV_GROUP = 2
VMEM_LIMIT = 48 * 1024 * 1024

NN = ((1,), (0,))
NT = ((1,), (1,))
TN = ((0,), (0,))


def _dot(a, b, dims=NN):
    return lax.dot_general(a, b, (dims, ((), ())), preferred_element_type=F32)


def _split2(x):
    hi = x.astype(BF16)
    lo = (x - hi.astype(F32)).astype(BF16)
    return hi, lo


def _dot3(a, b, dims=NN):
    ah, al = _split2(a)
    bh, bl = _split2(b)
    return _dot(ah, bh, dims) + (_dot(ah, bl, dims) + _dot(al, bh, dims))


def _dot_exact_lhs(a_bf16, b, dims=NN):
    b1 = b.astype(BF16)
    r1 = b - b1.astype(F32)
    b2 = r1.astype(BF16)
    b3 = (r1 - b2.astype(F32)).astype(BF16)
    return _dot(a_bf16, b1, dims) + (_dot(a_bf16, b2, dims) + _dot(a_bf16, b3, dims))


def _dot_exact_rhs(a, b_bf16):
    a1 = a.astype(BF16)
    r1 = a - a1.astype(F32)
    a2 = r1.astype(BF16)
    a3 = (r1 - a2.astype(F32)).astype(BF16)
    return _dot(a1, b_bf16) + (_dot(a2, b_bf16) + _dot(a3, b_bf16))


def _sigmoid(x):
    return 1.0 / (1.0 + jnp.exp(-x))


def _head_sum_matrix(width):
    r = lax.broadcasted_iota(jnp.int32, (width, width), 0) // HEAD
    c = lax.broadcasted_iota(jnp.int32, (width, width), 1) // HEAD
    return (r == c).astype(BF16)


def _cparams(sem):
    return pltpu.CompilerParams(dimension_semantics=sem, vmem_limit_bytes=VMEM_LIMIT)


def _proj_in_body(x_ref, g_ref, w_ref, *out_refs):
    x = x_ref[...]
    h = x * lax.rsqrt(jnp.mean(x * x, axis=-1, keepdims=True) + RMS_EPS) * g_ref[...]
    hb = h.astype(BF16)
    off = 0
    for o_ref in out_refs:
        n = o_ref.shape[-1]
        o_ref[...] = _dot(hb, w_ref[:, off:off + n])
        off += n


def _proj_in(x2, g, w_bf16, splits, tm=256):
    rows = x2.shape[0]
    ncol = w_bf16.shape[1]
    assert sum(splits) == ncol and rows % tm == 0
    return pl.pallas_call(
        _proj_in_body,
        grid=(rows // tm,),
        in_specs=[
            pl.BlockSpec((tm, D_MODEL), lambda i: (i, 0)),
            pl.BlockSpec((1, D_MODEL), lambda i: (0, 0)),
            pl.BlockSpec((D_MODEL, ncol), lambda i: (0, 0)),
        ],
        out_specs=[pl.BlockSpec((tm, n), lambda i: (i, 0)) for n in splits],
        out_shape=[jax.ShapeDtypeStruct((rows, n), F32) for n in splits],
        compiler_params=_cparams(("parallel",)),
        name="proj_in",
    )(x2, g, w_bf16)


def _rwkv_prep_body(*refs, has_vres):
    if has_vres:
        (ua_ref, prev_ref, next_ref, mu_ref, w0_ref, dup_ref, a0_ref, iup_ref, kk_ref, ka_ref, rk_ref,
         hd_ref, vu_ref, v0_ref, vf_ref,
         r_o, v_o, an_o, lw_o, kd_o, bb_o, bonus_o, gz_o) = refs
    else:
        (ua_ref, prev_ref, next_ref, mu_ref, w0_ref, dup_ref, a0_ref, iup_ref, kk_ref, ka_ref, rk_ref,
         r_o, v_o, an_o, lw_o, kd_o, bb_o, bonus_o, gz_o) = refs
    tt = ua_ref.shape[0]
    i = pl.program_id(1)
    last = pl.num_programs(1) - 1
    row = lax.broadcasted_iota(jnp.int32, (tt, 1), 0)
    hsum = _head_sum_matrix(RWKV_WIDTH)

    def shifted(lo, hi):
        u = ua_ref[:, lo:hi]
        p_row = jnp.where(i == 0, 0.0, prev_ref[SUBLANES - 1:SUBLANES, lo:hi])
        n_row = jnp.where(i == last, 0.0, next_ref[0:1, lo:hi])
        prev = jnp.where(row == 0, p_row, pltpu.roll(u, 1, axis=0))
        nxt = jnp.where(row == tt - 1, n_row, pltpu.roll(u, tt - 1, axis=0))
        return u + mu_ref[:, lo:hi] * (0.5 * (prev + nxt) - u)

    W = RWKV_WIDTH
    r = shifted(0, W)
    k = shifted(W, 2 * W)
    v = shifted(2 * W, 3 * W)
    z = shifted(3 * W, 4 * W)
    low = shifted(4 * W, 4 * W + 4 * LOW_RANK)
    w_raw = w0_ref[...] + _dot3(jnp.tanh(low[:, :2 * LOW_RANK]), dup_ref[...])
    nw = -w_raw
    softplus = jnp.maximum(nw, 0.0) + jnp.log(1.0 + jnp.exp(-jnp.abs(nw)))
    lw = -jnp.exp(-softplus - 0.5)
    a = _sigmoid(a0_ref[...] + _dot3(low[:, 2 * LOW_RANK:], iup_ref[...]))
    if has_vres:
        mix = _sigmoid(v0_ref[...] + _dot3(hd_ref[...], vu_ref[...]))
        v = v + (vf_ref[...] - v) * mix
    kk = k * kk_ref[...]
    ss = _dot_exact_rhs(kk * kk, hsum)
    kk = kk / jnp.maximum(jnp.sqrt(ss), 1e-12)
    ka = ka_ref[...]
    ksum = jnp.zeros_like(k)
    for d in range(2):
        a_d = a[:, d * W:(d + 1) * W]
        kd = k * (1.0 + (a_d - 1.0) * ka)
        ksum = ksum + kd
        lw_o[d] = lw[:, d * W:(d + 1) * W]
        kd_o[d] = kd
        bb_o[d] = kk * a_d
    r_o[...] = r
    v_o[...] = v
    an_o[...] = -kk
    bonus_o[...] = _dot_exact_rhs(r * ksum * rk_ref[...], hsum) * v
    gz_o[...] = z * _sigmoid(z)


def _rwkv_prep(ua, params, vres, tt=256):
    bsz, T, _ = ua.shape
    W = RWKV_WIDTH
    nt = T // tt
    hb = tt // SUBLANES
    n8 = T // SUBLANES
    full = lambda shape: pl.BlockSpec(shape, lambda b, i: (0,) * len(shape))
    tile = lambda n: pl.BlockSpec((None, tt, n), lambda b, i: (b, i, 0))
    in_specs = [
        tile(A_COLS),
        pl.BlockSpec((None, SUBLANES, A_COLS), lambda b, i: (b, jnp.maximum(i * hb - 1, 0), 0)),
        pl.BlockSpec((None, SUBLANES, A_COLS), lambda b, i: (b, jnp.minimum((i + 1) * hb, n8 - 1), 0)),
        full((1, A_COLS)), full((1, 2 * W)), full((2 * LOW_RANK, 2 * W)), full((1, 2 * W)),
        full((2 * LOW_RANK, 2 * W)), full((1, W)), full((1, W)), full((1, W)),
    ]
    args = [ua, ua, ua] + list(params)
    if vres is not None:
        hd, vu_pad, v0, v_first = vres
        in_specs += [tile(LANES), full((LANES, W)), full((1, W)), tile(W)]
        args += [hd, vu_pad, v0, v_first]
    dir_tile = pl.BlockSpec((2, None, tt, W), lambda b, i: (0, b, i, 0))
    out_specs = [tile(W), tile(W), tile(W), dir_tile, dir_tile, dir_tile, tile(W), tile(W)]
    one = jax.ShapeDtypeStruct((bsz, T, W), F32)
    two = jax.ShapeDtypeStruct((2, bsz, T, W), F32)
    return pl.pallas_call(
        functools.partial(_rwkv_prep_body, has_vres=vres is not None),
        grid=(bsz, nt),
        in_specs=in_specs,
        out_specs=out_specs,
        out_shape=[one, one, one, two, two, two, one, one],
        compiler_params=_cparams(("parallel", "parallel")),
        name="rwkv_prep",
    )(*args)


def _mm3(a, b, dims=NN):
    return _dot(a[0], b[0], dims) + (_dot(a[0], b[1], dims) + _dot(a[1], b[0], dims))


def _wkv_scan_body(r_ref, v_ref, an_ref, lw_ref, kd_ref, bb_ref, y_ref, s_ref):
    d = pl.program_id(0)
    c = pl.program_id(2)
    C = r_ref.shape[0]
    G = WKV_GROUP
    GW = G * HEAD
    NG = RWKV_HEADS // G
    assert C == HEAD

    @pl.when(c == 0)
    def _():
        s_ref[...] = jnp.zeros_like(s_ref)

    row = lax.broadcasted_iota(jnp.int32, (C, GW), 0)
    lane = lax.broadcasted_iota(jnp.int32, (C, GW), 1)
    col = lane % HEAD
    ahead = (row - col) * (1 - 2 * d)
    incl = (ahead >= 0).astype(F32)
    strict = (ahead > 0).astype(F32)
    eye = (row == col).astype(F32)
    same_blk = []
    size = INV_BASE
    while size < C:
        same_blk.append(((row // size) == (col // size)).astype(F32))
        size *= 2
    head_mask = [((lane // HEAD) == g).astype(BF16) for g in range(G)]
    srow = lax.broadcasted_iota(jnp.int32, (GW, GW), 0) // HEAD
    slane = lax.broadcasted_iota(jnp.int32, (GW, GW), 1) // HEAD
    state_mask = (srow == slane).astype(F32)

    trow = lax.broadcasted_iota(jnp.int32, (C, C), 0)
    tcol = lax.broadcasted_iota(jnp.int32, (C, C), 1)
    tri = ((trow - tcol) * (1 - 2 * d) >= 0).astype(BF16)

    lw = lw_ref[...]
    cs = _dot_exact_lhs(tri, lw)
    tot = jnp.sum(lw, axis=0, keepdims=True)
    e_incl = jnp.exp(cs)
    e_excl = jnp.exp(cs - lw)
    e_inv = jnp.exp(-cs)
    p_tot = jnp.exp(tot)
    rt = r_ref[...] * e_incl
    at = an_ref[...] * e_excl
    bt = bb_ref[...] * e_inv
    kt = kd_ref[...] * e_inv
    bh = bt * p_tot
    kh = kt * p_tot
    vv = v_ref[...]

    groups = range(NG)
    grp = lambda x: [x[:, g * GW:(g + 1) * GW] for g in groups]
    sp = _split2
    bd = lambda s: tuple(jnp.concatenate([p * m for m in head_mask], axis=0) for p in s)
    rows = lambda s, lo, hi: tuple(p[lo:hi] for p in s)
    cat = lambda a, b, axis: tuple(jnp.concatenate([x, y], axis=axis) for x, y in zip(a, b))

    at_g, rt_g, v_g = grp(at), grp(rt), grp(vv)
    ar_s = [sp(jnp.concatenate([a, r], axis=0)) for a, r in zip(at_g, rt_g)]
    sb = [_mm3(ar_s[g], bd(sp(x)), NT) for g, x in zip(groups, grp(bt))]
    sk = [_mm3(ar_s[g], bd(sp(x)), NT) for g, x in zip(groups, grp(kt))]
    m_ab = [x[:C] * strict for x in sb]
    n_rb = [x[C:] * incl for x in sb]
    m_ak = [x[:C] * strict for x in sk]
    n_rk = [x[C:] * incl for x in sk]
    v_bd = [bd(sp(x)) for x in v_g]
    mv = [_mm3(sp(m), vb) for m, vb in zip(m_ak, v_bd)]

    md = [m * same_blk[0] for m in m_ab]
    md_s = [sp(m) for m in md]
    m2 = [_mm3(s, bd(s)) for s in md_s]
    m2_s = [sp(m) for m in m2]
    m2_bd = [bd(s) for s in m2_s]
    x = [eye + m for m in md]
    x = [xi + _mm3(sp(xi), mb) for xi, mb in zip(x, m2_bd)]
    m4 = [_mm3(s, mb) for s, mb in zip(m2_s, m2_bd)]
    x = [xi + _mm3(sp(xi), bd(sp(m))) for xi, m in zip(x, m4)]
    for lvl in range(len(same_blk)):
        inner = same_blk[lvl]
        outer = same_blk[lvl + 1] if lvl + 1 < len(same_blk) else 1.0
        sel = outer - inner
        x_s = [sp(xi) for xi in x]
        t = [_mm3(s, bd(sp(m * sel))) for s, m in zip(x_s, m_ab)]
        x = [xi + _mm3(sp(ti), bd(s)) for xi, ti, s in zip(x, t, x_s)]

    at_s = [rows(s, 0, C) for s in ar_s]
    w_ab = [_mm3(sp(xi), cat(bd(a), bd(sp(m)), 1)) for xi, a, m in zip(x, at_s, mv)]
    w_a = [w[:, :GW] for w in w_ab]
    w_b = [w[:, GW:] for w in w_ab]
    w_a_s = [sp(w) for w in w_a]
    w_b_s = [sp(w) for w in w_b]
    n_rb_s = [sp(n) for n in n_rb]
    y_q = [r + _mm3(n, bd(w)) for r, n, w in zip(rt_g, n_rb_s, w_a_s)]
    y_c = [_mm3(cat(n, sp(nk), 1), cat(bd(w), vb, 0)) for n, nk, w, vb in zip(n_rb_s, n_rk, w_b_s, v_bd)]
    bh_s = [sp(b) for b in grp(bh)]
    phi = [_mm3(w, b, TN) * state_mask for w, b in zip(w_a_s, bh_s)]
    psi = [_mm3(cat(w, sp(v), 0), cat(b, sp(k), 0), TN) * state_mask
           for w, v, b, k in zip(w_b_s, v_g, bh_s, grp(kh))]
    for g in groups:
        s0 = s_ref[g]
        s0_s = sp(s0)
        y_ref[:, g * GW:(g + 1) * GW] = _mm3(sp(y_q[g]), s0_s, NT) + y_c[g]
        s_ref[g] = s0 * p_tot[:, g * GW:(g + 1) * GW] + _mm3(s0_s, sp(phi[g])) + psi[g]


def _wkv_scan(r, v, an, lw, kd, bb):
    bsz, T, W = r.shape
    C = CHUNK
    nc = T // C
    gw = WKV_GROUP * HEAD
    shared = pl.BlockSpec((None, C, W), lambda d, b, c: (b, c + d * (nc - 1 - 2 * c), 0))
    per_dir = pl.BlockSpec((None, None, C, W), lambda d, b, c: (d, b, c + d * (nc - 1 - 2 * c), 0))
    return pl.pallas_call(
        _wkv_scan_body,
        grid=(2, bsz, nc),
        in_specs=[shared, shared, shared, per_dir, per_dir, per_dir],
        out_specs=per_dir,
        out_shape=jax.ShapeDtypeStruct((2, bsz, T, W), F32),
        scratch_shapes=[pltpu.VMEM((RWKV_HEADS // WKV_GROUP, gw, gw), F32)],
        compiler_params=_cparams(("parallel", "parallel", "arbitrary")),
        name="wkv_scan",
    )(r, v, an, lw, kd, bb)


def _win_attn_body(q_ref, kp_ref, kc_ref, kn_ref, qg_ref, kg_ref, slope_ref, sink_ref, o_ref):
    n = pl.program_id(1)
    nb = pl.num_programs(1)
    hq = _head_sum_matrix(ATTN_WIDTH)
    hk = _head_sum_matrix(KV_WIDTH)

    q = q_ref[:, :ATTN_WIDTH]
    z = q_ref[:, ATTN_WIDTH + 2 * KV_WIDTH:]
    q = q * lax.rsqrt(_dot_exact_rhs(q * q, hq) * (1.0 / HEAD) + RMS_EPS) * qg_ref[...] * (HEAD ** -0.5)

    def kv_of(ref):
        kx = ref[:, ATTN_WIDTH:ATTN_WIDTH + KV_WIDTH]
        kx = kx * lax.rsqrt(_dot_exact_rhs(kx * kx, hk) * (1.0 / HEAD) + RMS_EPS) * kg_ref[...]
        return kx, ref[:, ATTN_WIDTH + KV_WIDTH:ATTN_WIDTH + 2 * KV_WIDTH]

    ks, vs = zip(*(kv_of(ref) for ref in (kp_ref, kc_ref, kn_ref)))
    qi = lax.broadcasted_iota(jnp.int32, (BLOCK, BLOCK), 0)
    ki = lax.broadcasted_iota(jnp.int32, (BLOCK, BLOCK), 1)
    dist, valid = [], []
    for j in range(3):
        dj = jnp.abs(qi - (ki + (j - 1) * BLOCK))
        ok = dj <= WINDOW
        if j == 0:
            ok = ok & (n > 0)
        if j == 2:
            ok = ok & (n < nb - 1)
        dist.append(dj.astype(F32))
        valid.append(ok)

    for h in range(ATTN_Q_HEADS):
        g = h // ATTN_GROUP
        qh = q[:, h * HEAD:(h + 1) * HEAD]
        slope = slope_ref[h]
        sink = sink_ref[h]
        s = []
        for j in range(3):
            sj = _dot(qh.astype(BF16), ks[j][:, g * HEAD:(g + 1) * HEAD].astype(BF16), NT)
            s.append(jnp.where(valid[j], sj - slope * dist[j], NEG_INF))
        m = jnp.maximum(jnp.maximum(jnp.max(s[0], axis=-1, keepdims=True), jnp.max(s[1], axis=-1, keepdims=True)),
                        jnp.maximum(jnp.max(s[2], axis=-1, keepdims=True), sink))
        e = [jnp.exp(sj - m) for sj in s]
        den = (jnp.sum(e[0], axis=-1, keepdims=True) + jnp.sum(e[1], axis=-1, keepdims=True)
               + jnp.sum(e[2], axis=-1, keepdims=True) + jnp.exp(sink - m))
        inv = 1.0 / den
        o = jnp.zeros((BLOCK, HEAD), F32)
        for j in range(3):
            o = o + _dot((e[j] * inv).astype(BF16), vs[j][:, g * HEAD:(g + 1) * HEAD].astype(BF16))
        zh = z[:, h * HEAD:(h + 1) * HEAD]
        o_ref[:, h * HEAD:(h + 1) * HEAD] = o * (zh * _sigmoid(zh))


def _win_attn(ub, q_g, k_g, slopes, sink):
    bsz, T, _ = ub.shape
    nb = T // BLOCK
    blk = lambda f: pl.BlockSpec((None, BLOCK, B_COLS), f)
    smem = pl.BlockSpec(memory_space=pltpu.SMEM)
    return pl.pallas_call(
        _win_attn_body,
        grid=(bsz, nb),
        in_specs=[
            blk(lambda b, n: (b, n, 0)),
            blk(lambda b, n: (b, jnp.maximum(n - 1, 0), 0)),
            blk(lambda b, n: (b, n, 0)),
            blk(lambda b, n: (b, jnp.minimum(n + 1, nb - 1), 0)),
            pl.BlockSpec((1, ATTN_WIDTH), lambda b, n: (0, 0)),
            pl.BlockSpec((1, KV_WIDTH), lambda b, n: (0, 0)),
            smem, smem,
        ],
        out_specs=pl.BlockSpec((None, BLOCK, ATTN_WIDTH), lambda b, n: (b, n, 0)),
        out_shape=jax.ShapeDtypeStruct((bsz, T, ATTN_WIDTH), F32),
        compiler_params=_cparams(("parallel", "parallel")),
        name="win_attn",
    )(ub, ub, ub, ub, q_g, k_g, slopes, sink)


def _merge_out_body(x_ref, y_ref, bonus_ref, gz_ref, ob_ref, ug_ref, p_ref, lnw_ref, lnb_ref,
                    pa_ref, pb_ref, wo_ref, pg_ref, gw_ref, pp_ref, o_ref):
    hsum = _head_sum_matrix(RWKV_WIDTH)
    y = y_ref[0] + y_ref[1]
    mean = _dot_exact_rhs(y, hsum) * (1.0 / HEAD)
    yc = y - mean
    var = _dot_exact_rhs(yc * yc, hsum) * (1.0 / HEAD)
    yn = yc * lax.rsqrt(var + GN_EPS) * lnw_ref[...] + lnb_ref[...]
    o_a = (yn + bonus_ref[...]) * gz_ref[...]
    y_a = _dot(o_a.astype(BF16), pa_ref[...])
    y_b = _dot(ob_ref[...].astype(BF16), pb_ref[...])
    merged = _sigmoid(ug_ref[:, :D_MODEL]) * y_a + _sigmoid(ug_ref[:, D_MODEL:]) * y_b
    x1 = x_ref[...] + _dot(merged.astype(BF16), wo_ref[...])
    ple = _dot(p_ref[...].astype(BF16), pp_ref[...])
    hn = x1 * lax.rsqrt(jnp.mean(x1 * x1, axis=-1, keepdims=True) + RMS_EPS) * pg_ref[...]
    o_ref[...] = x1 + _sigmoid(_dot(hn.astype(BF16), gw_ref[...])) * ple


def _merge_out(x2, y, bonus, gz, ob, ug, p2, lnw, lnb, pa, pb, wo, pg, gw, pp, tm=256):
    rows = x2.shape[0]
    W = RWKV_WIDTH
    tile = lambda n: pl.BlockSpec((tm, n), lambda i: (i, 0))
    full = lambda a: pl.BlockSpec(a.shape, lambda i: (0,) * a.ndim)
    return pl.pallas_call(
        _merge_out_body,
        grid=(rows // tm,),
        in_specs=[tile(D_MODEL), pl.BlockSpec((2, tm, W), lambda i: (0, i, 0)), tile(W), tile(W), tile(ATTN_WIDTH),
                  tile(G_COLS), tile(PLE_DIM), full(lnw), full(lnb), full(pa), full(pb), full(wo), full(pg),
                  full(gw), full(pp)],
        out_specs=tile(D_MODEL),
        out_shape=jax.ShapeDtypeStruct((rows, D_MODEL), F32),
        compiler_params=_cparams(("parallel",)),
        name="merge_out",
    )(x2, y, bonus, gz, ob, ug, p2, lnw, lnb, pa, pb, wo, pg, gw, pp)


def _block_diag2(m):
    z = jnp.zeros_like(m[0])
    return jnp.concatenate([jnp.concatenate([m[0], z], axis=1), jnp.concatenate([z, m[1]], axis=1)], axis=0)


def kernel(x, p, norm_g, w_in, shift_mu, decay_w0, decay_up, iclr_a0, iclr_up, vres_down, vres_up, vres_v0, k_k, k_a, r_k, ln_x_w, ln_x_b, q_norm_g, k_norm_g, sink, proj_a, proj_b, w_out, ple_norm_g, ple_gate_w, ple_proj):
    bsz, T, _ = x.shape
    depth = w_in.shape[0]
    rows = bsz * T
    W = RWKV_WIDTH
    slopes = jnp.asarray(2.0 ** (-8.0 * jnp.arange(1, ATTN_Q_HEADS + 1, dtype=F32) / ATTN_Q_HEADS), F32)
    x2 = x.reshape(rows, D_MODEL)
    v_first = None
    for i in range(depth):
        splits = [A_COLS, B_COLS, G_COLS]
        w_i = w_in[i]
        if i > 0:
            pad = jnp.zeros((D_MODEL, LANES - VRES_RANK), F32)
            w_i = jnp.concatenate([w_i, vres_down[i - 1], pad], axis=1)
            splits.append(LANES)
        outs = _proj_in(x2, norm_g[i].reshape(1, D_MODEL), w_i.astype(BF16), splits)
        ua = outs[0].reshape(bsz, T, A_COLS)
        ub = outs[1].reshape(bsz, T, B_COLS)
        ug = outs[2]
        params = [shift_mu[i].reshape(1, A_COLS), decay_w0[i].reshape(1, 2 * W), _block_diag2(decay_up[i]),
                  iclr_a0[i].reshape(1, 2 * W), _block_diag2(iclr_up[i]), k_k[i].reshape(1, W),
                  k_a[i].reshape(1, W), r_k[i].reshape(1, W)]
        vres = None
        if i > 0:
            vu_pad = jnp.concatenate([vres_up[i - 1], jnp.zeros((LANES - VRES_RANK, W), F32)], axis=0)
            vres = (outs[3].reshape(bsz, T, LANES), vu_pad, vres_v0[i - 1].reshape(1, W), v_first)
        r, v, an, lw, kd, bb, bonus, gz = _rwkv_prep(ua, params, vres)
        if i == 0:
            v_first = v
        y = _wkv_scan(r, v, an, lw, kd, bb)
        ob = _win_attn(ub, jnp.tile(q_norm_g[i], ATTN_Q_HEADS).reshape(1, ATTN_WIDTH),
                       jnp.tile(k_norm_g[i], ATTN_KV_HEADS).reshape(1, KV_WIDTH), slopes, sink[i])
        x2 = _merge_out(x2, y.reshape(2, rows, W), bonus.reshape(rows, W), gz.reshape(rows, W),
                        ob.reshape(rows, ATTN_WIDTH), ug, p[i].reshape(rows, PLE_DIM),
                        ln_x_w[i].reshape(1, W), ln_x_b[i].reshape(1, W), proj_a[i].astype(BF16),
                        proj_b[i].astype(BF16), w_out[i].astype(BF16), ple_norm_g[i].reshape(1, D_MODEL),
                        ple_gate_w[i].astype(BF16), ple_proj[i].astype(BF16))
    return x2.reshape(bsz, T, D_MODEL)
```

```python
import functools

import jax
import jax.numpy as jnp
from jax import lax
from jax.experimental import pallas as pl
from jax.experimental.pallas import tpu as pltpu

F32 = jnp.float32
BF16 = jnp.bfloat16

D_MODEL = 1024
PLE_DIM = 256
RWKV_WIDTH = 512
HEAD = 64
RWKV_HEADS = RWKV_WIDTH // HEAD
LOW_RANK = 64
VRES_RANK = 32
ATTN_WIDTH = 512
ATTN_Q_HEADS = ATTN_WIDTH // HEAD
ATTN_KV_HEADS = 2
ATTN_GROUP = ATTN_Q_HEADS // ATTN_KV_HEADS
KV_WIDTH = ATTN_KV_HEADS * HEAD
WINDOW = 128
BLOCK = 128
RMS_EPS = 1e-6
GN_EPS = 64e-5
NEG_INF = -1e30
A_COLS = 4 * RWKV_WIDTH + 4 * LOW_RANK
B_COLS = 2 * ATTN_WIDTH + 2 * KV_WIDTH
G_COLS = 2 * D_MODEL
LANES = 128
SUBLANES = 8
CHUNK = 64
INV_BASE = 2
WKV_GROUP = 2
VMEM_LIMIT = 48 * 1024 * 1024

NN = ((1,), (0,))
NT = ((1,), (1,))
TN = ((0,), (0,))


def _dot(a, b, dims=NN):
    return lax.dot_general(a, b, (dims, ((), ())), preferred_element_type=F32)


def _split2(x):
    hi = x.astype(BF16)
    lo = (x - hi.astype(F32)).astype(BF16)
    return hi, lo


def _dot3(a, b, dims=NN):
    ah, al = _split2(a)
    bh, bl = _split2(b)
    return _dot(ah, bh, dims) + (_dot(ah, bl, dims) + _dot(al, bh, dims))


def _dot_exact_lhs(a_bf16, b, dims=NN):
    b1 = b.astype(BF16)
    r1 = b - b1.astype(F32)
    b2 = r1.astype(BF16)
    b3 = (r1 - b2.astype(F32)).astype(BF16)
    return _dot(a_bf16, b1, dims) + (_dot(a_bf16, b2, dims) + _dot(a_bf16, b3, dims))


def _dot_exact_rhs(a, b_bf16):
    a1 = a.astype(BF16)
    r1 = a - a1.astype(F32)
    a2 = r1.astype(BF16)
    a3 = (r1 - a2.astype(F32)).astype(BF16)
    return _dot(a1, b_bf16) + (_dot(a2, b_bf16) + _dot(a3, b_bf16))


def _sigmoid(x):
    return 1.0 / (1.0 + jnp.exp(-x))


def _head_sum_matrix(width):
    r = lax.broadcasted_iota(jnp.int32, (width, width), 0) // HEAD
    c = lax.broadcasted_iota(jnp.int32, (width, width), 1) // HEAD
    return (r == c).astype(BF16)


def _cparams(sem):
    return pltpu.CompilerParams(dimension_semantics=sem, vmem_limit_bytes=VMEM_LIMIT)


def _proj_in_body(x_ref, g_ref, w_ref, *out_refs):
    x = x_ref[...]
    h = x * lax.rsqrt(jnp.mean(x * x, axis=-1, keepdims=True) + RMS_EPS) * g_ref[...]
    hb = h.astype(BF16)
    off = 0
    for o_ref in out_refs:
        n = o_ref.shape[-1]
        o_ref[...] = _dot(hb, w_ref[:, off:off + n])
        off += n


def _proj_in(x2, g, w_bf16, splits, tm=256):
    rows = x2.shape[0]
    ncol = w_bf16.shape[1]
    assert sum(splits) == ncol and rows % tm == 0
    return pl.pallas_call(
        _proj_in_body,
        grid=(rows // tm,),
        in_specs=[
            pl.BlockSpec((tm, D_MODEL), lambda i: (i, 0)),
            pl.BlockSpec((1, D_MODEL), lambda i: (0, 0)),
            pl.BlockSpec((D_MODEL, ncol), lambda i: (0, 0)),
        ],
        out_specs=[pl.BlockSpec((tm, n), lambda i: (i, 0)) for n in splits],
        out_shape=[jax.ShapeDtypeStruct((rows, n), F32) for n in splits],
        compiler_params=_cparams(("parallel",)),
        name="proj_in",
    )(x2, g, w_bf16)


def _rwkv_prep_body(*refs, has_vres):
    if has_vres:
        (ua_ref, prev_ref, next_ref, mu_ref, w0_ref, dup_ref, a0_ref, iup_ref, kk_ref, ka_ref, rk_ref,
         hd_ref, vu_ref, v0_ref, vf_ref,
         r_o, v_o, an_o, lw_o, kd_o, bb_o, bonus_o, gz_o) = refs
    else:
        (ua_ref, prev_ref, next_ref, mu_ref, w0_ref, dup_ref, a0_ref, iup_ref, kk_ref, ka_ref, rk_ref,
         r_o, v_o, an_o, lw_o, kd_o, bb_o, bonus_o, gz_o) = refs
    tt = ua_ref.shape[0]
    i = pl.program_id(1)
    last = pl.num_programs(1) - 1
    row = lax.broadcasted_iota(jnp.int32, (tt, 1), 0)
    hsum = _head_sum_matrix(RWKV_WIDTH)

    def shifted(lo, hi):
        u = ua_ref[:, lo:hi]
        p_row = jnp.where(i == 0, 0.0, prev_ref[SUBLANES - 1:SUBLANES, lo:hi])
        n_row = jnp.where(i == last, 0.0, next_ref[0:1, lo:hi])
        prev = jnp.where(row == 0, p_row, pltpu.roll(u, 1, axis=0))
        nxt = jnp.where(row == tt - 1, n_row, pltpu.roll(u, tt - 1, axis=0))
        return u + mu_ref[:, lo:hi] * (0.5 * (prev + nxt) - u)

    W = RWKV_WIDTH
    r = shifted(0, W)
    k = shifted(W, 2 * W)
    v = shifted(2 * W, 3 * W)
    z = shifted(3 * W, 4 * W)
    low = shifted(4 * W, 4 * W + 4 * LOW_RANK)
    w_raw = w0_ref[...] + _dot3(jnp.tanh(low[:, :2 * LOW_RANK]), dup_ref[...])
    nw = -w_raw
    softplus = jnp.maximum(nw, 0.0) + jnp.log(1.0 + jnp.exp(-jnp.abs(nw)))
    lw = -jnp.exp(-softplus - 0.5)
    a = _sigmoid(a0_ref[...] + _dot3(low[:, 2 * LOW_RANK:], iup_ref[...]))
    if has_vres:
        mix = _sigmoid(v0_ref[...] + _dot3(hd_ref[...], vu_ref[...]))
        v = v + (vf_ref[...] - v) * mix
    kk = k * kk_ref[...]
    ss = _dot_exact_rhs(kk * kk, hsum)
    kk = kk / jnp.maximum(jnp.sqrt(ss), 1e-12)
    ka = ka_ref[...]
    ksum = jnp.zeros_like(k)
    for d in range(2):
        a_d = a[:, d * W:(d + 1) * W]
        kd = k * (1.0 + (a_d - 1.0) * ka)
        ksum = ksum + kd
        lw_o[d] = lw[:, d * W:(d + 1) * W]
        kd_o[d] = kd
        bb_o[d] = kk * a_d
    r_o[...] = r
    v_o[...] = v
    an_o[...] = -kk
    bonus_o[...] = _dot_exact_rhs(r * ksum * rk_ref[...], hsum) * v
    gz_o[...] = z * _sigmoid(z)


def _rwkv_prep(ua, params, vres, tt=256):
    bsz, T, _ = ua.shape
    W = RWKV_WIDTH
    nt = T // tt
    hb = tt // SUBLANES
    n8 = T // SUBLANES
    full = lambda shape: pl.BlockSpec(shape, lambda b, i: (0,) * len(shape))
    tile = lambda n: pl.BlockSpec((None, tt, n), lambda b, i: (b, i, 0))
    in_specs = [
        tile(A_COLS),
        pl.BlockSpec((None, SUBLANES, A_COLS), lambda b, i: (b, jnp.maximum(i * hb - 1, 0), 0)),
        pl.BlockSpec((None, SUBLANES, A_COLS), lambda b, i: (b, jnp.minimum((i + 1) * hb, n8 - 1), 0)),
        full((1, A_COLS)), full((1, 2 * W)), full((2 * LOW_RANK, 2 * W)), full((1, 2 * W)),
        full((2 * LOW_RANK, 2 * W)), full((1, W)), full((1, W)), full((1, W)),
    ]
    args = [ua, ua, ua] + list(params)
    if vres is not None:
        hd, vu_pad, v0, v_first = vres
        in_specs += [tile(LANES), full((LANES, W)), full((1, W)), tile(W)]
        args += [hd, vu_pad, v0, v_first]
    dir_tile = pl.BlockSpec((2, None, tt, W), lambda b, i: (0, b, i, 0))
    out_specs = [tile(W), tile(W), tile(W), dir_tile, dir_tile, dir_tile, tile(W), tile(W)]
    one = jax.ShapeDtypeStruct((bsz, T, W), F32)
    two = jax.ShapeDtypeStruct((2, bsz, T, W), F32)
    return pl.pallas_call(
        functools.partial(_rwkv_prep_body, has_vres=vres is not None),
        grid=(bsz, nt),
        in_specs=in_specs,
        out_specs=out_specs,
        out_shape=[one, one, one, two, two, two, one, one],
        compiler_params=_cparams(("parallel", "parallel")),
        name="rwkv_prep",
    )(*args)


def _wkv_scan_body(rf_ref, vf_ref, af_ref, rb_ref, vb_ref, ab_ref, lwf_ref, kdf_ref, bbf_ref,
                   lwb_ref, kdb_ref, bbb_ref, yf_ref, yb_ref, s_ref):
    c = pl.program_id(0)
    bsz, C, _ = rf_ref.shape
    G = WKV_GROUP
    GW = G * HEAD
    NG = RWKV_HEADS // G
    assert C == HEAD

    @pl.when(c == 0)
    def _():
        s_ref[...] = jnp.zeros_like(s_ref)

    row = lax.broadcasted_iota(jnp.int32, (C, GW), 0)
    lane = lax.broadcasted_iota(jnp.int32, (C, GW), 1)
    col = lane % HEAD
    eye = (row == col).astype(F32)
    same_blk = []
    size = INV_BASE
    while size < C:
        same_blk.append(((row // size) == (col // size)).astype(F32))
        size *= 2
    head_mask = [((lane // HEAD) == g).astype(BF16) for g in range(G)]
    srow = lax.broadcasted_iota(jnp.int32, (GW, GW), 0) // HEAD
    slane = lax.broadcasted_iota(jnp.int32, (GW, GW), 1) // HEAD
    state_mask = (srow == slane).astype(F32)
    trow = lax.broadcasted_iota(jnp.int32, (C, C), 0)
    tcol = lax.broadcasted_iota(jnp.int32, (C, C), 1)
    incl_d = [(col <= row).astype(F32), (col >= row).astype(F32)]
    strict_d = [(col < row).astype(F32), (col > row).astype(F32)]
    tri_d = [(tcol <= trow).astype(BF16), (tcol >= trow).astype(BF16)]

    b16 = lambda x: x.astype(BF16)
    bd = lambda xb: jnp.concatenate([xb * m for m in head_mask], axis=0)
    cat = lambda a, b, axis: jnp.concatenate([a, b], axis=axis)

    dir_refs = [(rf_ref, vf_ref, af_ref, lwf_ref, kdf_ref, bbf_ref, yf_ref),
                (rb_ref, vb_ref, ab_ref, lwb_ref, kdb_ref, bbb_ref, yb_ref)]
    chains = []
    at_g, rt_g, v_g, bt_g, kt_g, bh_g, kh_g, pt_g = [], [], [], [], [], [], [], []
    for d, (r_ref, v_ref, an_ref, lw_ref, kd_ref, bb_ref, _) in enumerate(dir_refs):
        for b in range(bsz):
            lw = lw_ref[b]
            cs = _dot_exact_lhs(tri_d[d], lw)
            tot = jnp.sum(lw, axis=0, keepdims=True)
            e_inv = jnp.exp(-cs)
            p_tot = jnp.exp(tot)
            rt = r_ref[b] * jnp.exp(cs)
            at = an_ref[b] * jnp.exp(cs - lw)
            bt = bb_ref[b] * e_inv
            kt = kd_ref[b] * e_inv
            bh = bt * p_tot
            kh = kt * p_tot
            vv = v_ref[b]
            for g in range(NG):
                sl = slice(g * GW, (g + 1) * GW)
                chains.append((d, b, g))
                for lst, val in ((at_g, at), (rt_g, rt), (v_g, vv), (bt_g, bt), (kt_g, kt), (bh_g, bh), (kh_g, kh),
                                 (pt_g, p_tot)):
                    lst.append(val[:, sl])
    strict = [strict_d[d] for d, _, _ in chains]
    incl = [incl_d[d] for d, _, _ in chains]

    at_b = [b16(a) for a in at_g]
    v_b = [b16(v) for v in v_g]
    ar_b = [cat(a, b16(r), 0) for a, r in zip(at_b, rt_g)]
    sb = [_dot(ar, bd(b16(x)), NT) for ar, x in zip(ar_b, bt_g)]
    sk = [_dot(ar, bd(b16(x)), NT) for ar, x in zip(ar_b, kt_g)]
    m_ab = [x[:C] * m for x, m in zip(sb, strict)]
    n_rb = [x[C:] * m for x, m in zip(sb, incl)]
    m_ak = [x[:C] * m for x, m in zip(sk, strict)]
    n_rk = [x[C:] * m for x, m in zip(sk, incl)]
    v_bd = [bd(v) for v in v_b]
    mv = [_dot(b16(m), vb) for m, vb in zip(m_ak, v_bd)]

    x = [eye + m * same_blk[0] for m in m_ab]
    for lvl in range(len(same_blk)):
        inner = same_blk[lvl]
        outer = same_blk[lvl + 1] if lvl + 1 < len(same_blk) else 1.0
        sel = outer - inner
        x_b = [b16(xi) for xi in x]
        t = [_dot(xb, bd(b16(m * sel))) for xb, m in zip(x_b, m_ab)]
        x = [xi + _dot(b16(ti), bd(xb)) for xi, ti, xb in zip(x, t, x_b)]

    w_ab = [_dot(b16(xi), cat(bd(a), bd(b16(m)), 1)) for xi, a, m in zip(x, at_b, mv)]
    w_a_b = [b16(w[:, :GW]) for w in w_ab]
    w_b_b = [b16(w[:, GW:]) for w in w_ab]
    n_rb_b = [b16(n) for n in n_rb]
    y_q = [r + _dot(n, bd(w)) for r, n, w in zip(rt_g, n_rb_b, w_a_b)]
    y_c = [_dot(cat(n, b16(nk), 1), cat(bd(w), vb, 0)) for n, nk, w, vb in zip(n_rb_b, n_rk, w_b_b, v_bd)]
    bh_b = [b16(b) for b in bh_g]
    phi = [_dot(w, b, TN) * state_mask for w, b in zip(w_a_b, bh_b)]
    psi = [_dot(cat(w, v, 0), cat(b, b16(k), 0), TN) * state_mask
           for w, v, b, k in zip(w_b_b, v_b, bh_b, kh_g)]
    for i, (d, b, g) in enumerate(chains):
        s0 = s_ref[d, b, g]
        s0_b = b16(s0)
        dir_refs[d][-1][b, :, g * GW:(g + 1) * GW] = _dot(b16(y_q[i]), s0_b, NT) + y_c[i]
        s_ref[d, b, g] = s0 * pt_g[i] + _dot(s0_b, b16(phi[i])) + psi[i]


def _wkv_scan(r, v, an, lw, kd, bb):
    bsz, T, W = r.shape
    C = CHUNK
    nc = T // C
    gw = WKV_GROUP * HEAD
    fwd = pl.BlockSpec((bsz, C, W), lambda c: (0, c, 0))
    bwd = pl.BlockSpec((bsz, C, W), lambda c: (0, nc - 1 - c, 0))
    fwd_d = pl.BlockSpec((None, bsz, C, W), lambda c: (0, 0, c, 0))
    bwd_d = pl.BlockSpec((None, bsz, C, W), lambda c: (1, 0, nc - 1 - c, 0))
    out = jax.ShapeDtypeStruct((bsz, T, W), F32)
    return pl.pallas_call(
        _wkv_scan_body,
        grid=(nc,),
        in_specs=[fwd, fwd, fwd, bwd, bwd, bwd, fwd_d, fwd_d, fwd_d, bwd_d, bwd_d, bwd_d],
        out_specs=[fwd, bwd],
        out_shape=[out, out],
        scratch_shapes=[pltpu.VMEM((2, bsz, RWKV_HEADS // WKV_GROUP, gw, gw), F32)],
        compiler_params=_cparams(("arbitrary",)),
        name="wkv_scan",
    )(r, v, an, r, v, an, lw, kd, bb, lw, kd, bb)


def _win_attn_body(q_ref, kp_ref, kc_ref, kn_ref, qg_ref, kg_ref, slope_ref, sink_ref, o_ref):
    n = pl.program_id(1)
    nb = pl.num_programs(1)
    hq = _head_sum_matrix(ATTN_WIDTH)
    hk = _head_sum_matrix(KV_WIDTH)

    q = q_ref[:, :ATTN_WIDTH]
    z = q_ref[:, ATTN_WIDTH + 2 * KV_WIDTH:]
    q = q * lax.rsqrt(_dot_exact_rhs(q * q, hq) * (1.0 / HEAD) + RMS_EPS) * qg_ref[...] * (HEAD ** -0.5)

    def kv_of(ref):
        kx = ref[:, ATTN_WIDTH:ATTN_WIDTH + KV_WIDTH]
        kx = kx * lax.rsqrt(_dot_exact_rhs(kx * kx, hk) * (1.0 / HEAD) + RMS_EPS) * kg_ref[...]
        return kx, ref[:, ATTN_WIDTH + KV_WIDTH:ATTN_WIDTH + 2 * KV_WIDTH]

    ks, vs = zip(*(kv_of(ref) for ref in (kp_ref, kc_ref, kn_ref)))
    qi = lax.broadcasted_iota(jnp.int32, (BLOCK, BLOCK), 0)
    ki = lax.broadcasted_iota(jnp.int32, (BLOCK, BLOCK), 1)
    dist, valid = [], []
    for j in range(3):
        dj = jnp.abs(qi - (ki + (j - 1) * BLOCK))
        ok = dj <= WINDOW
        if j == 0:
            ok = ok & (n > 0)
        if j == 2:
            ok = ok & (n < nb - 1)
        dist.append(dj.astype(F32))
        valid.append(ok)

    for h in range(ATTN_Q_HEADS):
        g = h // ATTN_GROUP
        qh = q[:, h * HEAD:(h + 1) * HEAD]
        slope = slope_ref[h]
        sink = sink_ref[h]
        s = []
        for j in range(3):
            sj = _dot(qh.astype(BF16), ks[j][:, g * HEAD:(g + 1) * HEAD].astype(BF16), NT)
            s.append(jnp.where(valid[j], sj - slope * dist[j], NEG_INF))
        m = jnp.maximum(jnp.maximum(jnp.max(s[0], axis=-1, keepdims=True), jnp.max(s[1], axis=-1, keepdims=True)),
                        jnp.maximum(jnp.max(s[2], axis=-1, keepdims=True), sink))
        e = [jnp.exp(sj - m) for sj in s]
        den = (jnp.sum(e[0], axis=-1, keepdims=True) + jnp.sum(e[1], axis=-1, keepdims=True)
               + jnp.sum(e[2], axis=-1, keepdims=True) + jnp.exp(sink - m))
        inv = 1.0 / den
        o = jnp.zeros((BLOCK, HEAD), F32)
        for j in range(3):
            o = o + _dot((e[j] * inv).astype(BF16), vs[j][:, g * HEAD:(g + 1) * HEAD].astype(BF16))
        zh = z[:, h * HEAD:(h + 1) * HEAD]
        o_ref[:, h * HEAD:(h + 1) * HEAD] = o * (zh * _sigmoid(zh))


def _win_attn(ub, q_g, k_g, slopes, sink):
    bsz, T, _ = ub.shape
    nb = T // BLOCK
    blk = lambda f: pl.BlockSpec((None, BLOCK, B_COLS), f)
    smem = pl.BlockSpec(memory_space=pltpu.SMEM)
    return pl.pallas_call(
        _win_attn_body,
        grid=(bsz, nb),
        in_specs=[
            blk(lambda b, n: (b, n, 0)),
            blk(lambda b, n: (b, jnp.maximum(n - 1, 0), 0)),
            blk(lambda b, n: (b, n, 0)),
            blk(lambda b, n: (b, jnp.minimum(n + 1, nb - 1), 0)),
            pl.BlockSpec((1, ATTN_WIDTH), lambda b, n: (0, 0)),
            pl.BlockSpec((1, KV_WIDTH), lambda b, n: (0, 0)),
            smem, smem,
        ],
        out_specs=pl.BlockSpec((None, BLOCK, ATTN_WIDTH), lambda b, n: (b, n, 0)),
        out_shape=jax.ShapeDtypeStruct((bsz, T, ATTN_WIDTH), F32),
        compiler_params=_cparams(("parallel", "parallel")),
        name="win_attn",
    )(ub, ub, ub, ub, q_g, k_g, slopes, sink)


def _merge_out_body(x_ref, yf_ref, yb_ref, bonus_ref, gz_ref, ob_ref, ug_ref, p_ref, lnw_ref, lnb_ref,
                    pa_ref, pb_ref, wo_ref, pg_ref, gw_ref, pp_ref, o_ref):
    hsum = _head_sum_matrix(RWKV_WIDTH)
    y = yf_ref[...] + yb_ref[...]
    mean = _dot_exact_rhs(y, hsum) * (1.0 / HEAD)
    yc = y - mean
    var = _dot_exact_rhs(yc * yc, hsum) * (1.0 / HEAD)
    yn = yc * lax.rsqrt(var + GN_EPS) * lnw_ref[...] + lnb_ref[...]
    o_a = (yn + bonus_ref[...]) * gz_ref[...]
    y_a = _dot(o_a.astype(BF16), pa_ref[...])
    y_b = _dot(ob_ref[...].astype(BF16), pb_ref[...])
    merged = _sigmoid(ug_ref[:, :D_MODEL]) * y_a + _sigmoid(ug_ref[:, D_MODEL:]) * y_b
    x1 = x_ref[...] + _dot(merged.astype(BF16), wo_ref[...])
    ple = _dot(p_ref[...].astype(BF16), pp_ref[...])
    hn = x1 * lax.rsqrt(jnp.mean(x1 * x1, axis=-1, keepdims=True) + RMS_EPS) * pg_ref[...]
    o_ref[...] = x1 + _sigmoid(_dot(hn.astype(BF16), gw_ref[...])) * ple


def _merge_out(x2, yf, yb, bonus, gz, ob, ug, p2, lnw, lnb, pa, pb, wo, pg, gw, pp, tm=256):
    rows = x2.shape[0]
    W = RWKV_WIDTH
    tile = lambda n: pl.BlockSpec((tm, n), lambda i: (i, 0))
    full = lambda a: pl.BlockSpec(a.shape, lambda i: (0,) * a.ndim)
    return pl.pallas_call(
        _merge_out_body,
        grid=(rows // tm,),
        in_specs=[tile(D_MODEL), tile(W), tile(W), tile(W), tile(W), tile(ATTN_WIDTH),
                  tile(G_COLS), tile(PLE_DIM), full(lnw), full(lnb), full(pa), full(pb), full(wo), full(pg),
                  full(gw), full(pp)],
        out_specs=tile(D_MODEL),
        out_shape=jax.ShapeDtypeStruct((rows, D_MODEL), F32),
        compiler_params=_cparams(("parallel",)),
        name="merge_out",
    )(x2, yf, yb, bonus, gz, ob, ug, p2, lnw, lnb, pa, pb, wo, pg, gw, pp)


def _block_diag2(m):
    z = jnp.zeros_like(m[0])
    return jnp.concatenate([jnp.concatenate([m[0], z], axis=1), jnp.concatenate([z, m[1]], axis=1)], axis=0)


def kernel(x, p, norm_g, w_in, shift_mu, decay_w0, decay_up, iclr_a0, iclr_up, vres_down, vres_up, vres_v0, k_k, k_a, r_k, ln_x_w, ln_x_b, q_norm_g, k_norm_g, sink, proj_a, proj_b, w_out, ple_norm_g, ple_gate_w, ple_proj):
    bsz, T, _ = x.shape
    depth = w_in.shape[0]
    rows = bsz * T
    W = RWKV_WIDTH
    slopes = jnp.asarray(2.0 ** (-8.0 * jnp.arange(1, ATTN_Q_HEADS + 1, dtype=F32) / ATTN_Q_HEADS), F32)
    x2 = x.reshape(rows, D_MODEL)
    v_first = None
    for i in range(depth):
        splits = [A_COLS, B_COLS, G_COLS]
        w_i = w_in[i]
        if i > 0:
            pad = jnp.zeros((D_MODEL, LANES - VRES_RANK), F32)
            w_i = jnp.concatenate([w_i, vres_down[i - 1], pad], axis=1)
            splits.append(LANES)
        outs = _proj_in(x2, norm_g[i].reshape(1, D_MODEL), w_i.astype(BF16), splits)
        ua = outs[0].reshape(bsz, T, A_COLS)
        ub = outs[1].reshape(bsz, T, B_COLS)
        ug = outs[2]
        params = [shift_mu[i].reshape(1, A_COLS), decay_w0[i].reshape(1, 2 * W), _block_diag2(decay_up[i]),
                  iclr_a0[i].reshape(1, 2 * W), _block_diag2(iclr_up[i]), k_k[i].reshape(1, W),
                  k_a[i].reshape(1, W), r_k[i].reshape(1, W)]
        vres = None
        if i > 0:
            vu_pad = jnp.concatenate([vres_up[i - 1], jnp.zeros((LANES - VRES_RANK, W), F32)], axis=0)
            vres = (outs[3].reshape(bsz, T, LANES), vu_pad, vres_v0[i - 1].reshape(1, W), v_first)
        r, v, an, lw, kd, bb, bonus, gz = _rwkv_prep(ua, params, vres)
        if i == 0:
            v_first = v
        yf, yb = _wkv_scan(r, v, an, lw, kd, bb)
        ob = _win_attn(ub, jnp.tile(q_norm_g[i], ATTN_Q_HEADS).reshape(1, ATTN_WIDTH),
                       jnp.tile(k_norm_g[i], ATTN_KV_HEADS).reshape(1, KV_WIDTH), slopes, sink[i])
        x2 = _merge_out(x2, yf.reshape(rows, W), yb.reshape(rows, W), bonus.reshape(rows, W), gz.reshape(rows, W),
                        ob.reshape(rows, ATTN_WIDTH), ug, p[i].reshape(rows, PLE_DIM),
                        ln_x_w[i].reshape(1, W), ln_x_b[i].reshape(1, W), proj_a[i].astype(BF16),
                        proj_b[i].astype(BF16), w_out[i].astype(BF16), ple_norm_g[i].reshape(1, D_MODEL),
                        ple_gate_w[i].astype(BF16), ple_proj[i].astype(BF16))
    return x2.reshape(bsz, T, D_MODEL)
```

```python
import functools

import jax
import jax.numpy as jnp
from jax import lax
from jax.experimental import pallas as pl
from jax.experimental.pallas import tpu as pltpu

F32 = jnp.float32
BF16 = jnp.bfloat16

D_MODEL = 1024
PLE_DIM = 256
RWKV_WIDTH = 512
HEAD = 64
RWKV_HEADS = RWKV_WIDTH // HEAD
LOW_RANK = 64
VRES_RANK = 32
ATTN_WIDTH = 512
ATTN_Q_HEADS = ATTN_WIDTH // HEAD
ATTN_KV_HEADS = 2
ATTN_GROUP = ATTN_Q_HEADS // ATTN_KV_HEADS
KV_WIDTH = ATTN_KV_HEADS * HEAD
WINDOW = 128
BLOCK = 128
RMS_EPS = 1e-6
GN_EPS = 64e-5
NEG_INF = -1e30
A_COLS = 4 * RWKV_WIDTH + 4 * LOW_RANK
B_COLS = 2 * ATTN_WIDTH + 2 * KV_WIDTH
G_COLS = 2 * D_MODEL
LANES = 128
SUBLANES = 8
CHUNK = 64
INV_BASE = 2
WKV_GROUP = 2
VMEM_LIMIT = 48 * 1024 * 1024

NN = ((1,), (0,))
NT = ((1,), (1,))
TN = ((0,), (0,))


def _dot(a, b, dims=NN):
    return lax.dot_general(a, b, (dims, ((), ())), preferred_element_type=F32)


def _split2(x):
    hi = x.astype(BF16)
    lo = (x - hi.astype(F32)).astype(BF16)
    return hi, lo


def _dot3(a, b, dims=NN):
    ah, al = _split2(a)
    bh, bl = _split2(b)
    return _dot(ah, bh, dims) + (_dot(ah, bl, dims) + _dot(al, bh, dims))


def _dot_exact_lhs(a_bf16, b, dims=NN):
    b1 = b.astype(BF16)
    r1 = b - b1.astype(F32)
    b2 = r1.astype(BF16)
    b3 = (r1 - b2.astype(F32)).astype(BF16)
    return _dot(a_bf16, b1, dims) + (_dot(a_bf16, b2, dims) + _dot(a_bf16, b3, dims))


def _dot_exact_rhs(a, b_bf16):
    a1 = a.astype(BF16)
    r1 = a - a1.astype(F32)
    a2 = r1.astype(BF16)
    a3 = (r1 - a2.astype(F32)).astype(BF16)
    return _dot(a1, b_bf16) + (_dot(a2, b_bf16) + _dot(a3, b_bf16))


def _dot_split2_rhs(a, b_bf16):
    hi, lo = _split2(a)
    return _dot(hi, b_bf16) + _dot(lo, b_bf16)


def _sigmoid(x):
    return 1.0 / (1.0 + jnp.exp(-x))


def _head_sum_matrix(width):
    r = lax.broadcasted_iota(jnp.int32, (width, width), 0) // HEAD
    c = lax.broadcasted_iota(jnp.int32, (width, width), 1) // HEAD
    return (r == c).astype(BF16)


def _cparams(sem):
    return pltpu.CompilerParams(dimension_semantics=sem, vmem_limit_bytes=VMEM_LIMIT)


def _proj_in_body(x_ref, g_ref, w_ref, *out_refs):
    x = x_ref[...]
    h = x * lax.rsqrt(jnp.mean(x * x, axis=-1, keepdims=True) + RMS_EPS) * g_ref[...]
    hb = h.astype(BF16)
    off = 0
    for o_ref in out_refs:
        n = o_ref.shape[-1]
        o_ref[...] = _dot(hb, w_ref[:, off:off + n])
        off += n


def _proj_in(x2, g, w_bf16, splits, tm=256):
    rows = x2.shape[0]
    ncol = w_bf16.shape[1]
    assert sum(splits) == ncol and rows % tm == 0
    return pl.pallas_call(
        _proj_in_body,
        grid=(rows // tm,),
        in_specs=[
            pl.BlockSpec((tm, D_MODEL), lambda i: (i, 0)),
            pl.BlockSpec((1, D_MODEL), lambda i: (0, 0)),
            pl.BlockSpec((D_MODEL, ncol), lambda i: (0, 0)),
        ],
        out_specs=[pl.BlockSpec((tm, n), lambda i: (i, 0)) for n in splits],
        out_shape=[jax.ShapeDtypeStruct((rows, n), F32) for n in splits],
        compiler_params=_cparams(("parallel",)),
        name="proj_in",
    )(x2, g, w_bf16)


def _rwkv_prep_body(*refs, has_vres):
    if has_vres:
        (ua_ref, prev_ref, next_ref, mu_ref, w0_ref, dup_ref, a0_ref, iup_ref, kk_ref, ka_ref, rk_ref,
         hd_ref, vu_ref, v0_ref, vf_ref,
         r_o, v_o, an_o, lw_o, kd_o, bb_o, bonus_o, gz_o) = refs
    else:
        (ua_ref, prev_ref, next_ref, mu_ref, w0_ref, dup_ref, a0_ref, iup_ref, kk_ref, ka_ref, rk_ref,
         r_o, v_o, an_o, lw_o, kd_o, bb_o, bonus_o, gz_o) = refs
    tt = ua_ref.shape[0]
    i = pl.program_id(1)
    last = pl.num_programs(1) - 1
    row = lax.broadcasted_iota(jnp.int32, (tt, 1), 0)
    hsum = _head_sum_matrix(RWKV_WIDTH)

    def shifted(lo, hi):
        u = ua_ref[:, lo:hi]
        p_row = jnp.where(i == 0, 0.0, prev_ref[SUBLANES - 1:SUBLANES, lo:hi])
        n_row = jnp.where(i == last, 0.0, next_ref[0:1, lo:hi])
        prev = jnp.where(row == 0, p_row, pltpu.roll(u, 1, axis=0))
        nxt = jnp.where(row == tt - 1, n_row, pltpu.roll(u, tt - 1, axis=0))
        return u + mu_ref[:, lo:hi] * (0.5 * (prev + nxt) - u)

    W = RWKV_WIDTH
    r = shifted(0, W)
    k = shifted(W, 2 * W)
    v = shifted(2 * W, 3 * W)
    z = shifted(3 * W, 4 * W)
    low = shifted(4 * W, 4 * W + 4 * LOW_RANK)
    w_raw = w0_ref[...] + _dot3(jnp.tanh(low[:, :2 * LOW_RANK]), dup_ref[...])
    nw = -w_raw
    softplus = jnp.maximum(nw, 0.0) + jnp.log(1.0 + jnp.exp(-jnp.abs(nw)))
    lw = -jnp.exp(-softplus - 0.5)
    a = _sigmoid(a0_ref[...] + _dot3(low[:, 2 * LOW_RANK:], iup_ref[...]))
    if has_vres:
        mix = _sigmoid(v0_ref[...] + _dot3(hd_ref[...], vu_ref[...]))
        v = v + (vf_ref[...] - v) * mix
    kk = k * kk_ref[...]
    ss = _dot_exact_rhs(kk * kk, hsum)
    kk = kk / jnp.maximum(jnp.sqrt(ss), 1e-12)
    ka = ka_ref[...]
    ksum = jnp.zeros_like(k)
    for d in range(2):
        a_d = a[:, d * W:(d + 1) * W]
        kd = k * (1.0 + (a_d - 1.0) * ka)
        ksum = ksum + kd
        lw_o[d] = lw[:, d * W:(d + 1) * W]
        kd_o[d] = kd
        bb_o[d] = kk * a_d
    r_o[...] = r
    v_o[...] = v
    an_o[...] = -kk
    bonus_o[...] = _dot_exact_rhs(r * ksum * rk_ref[...], hsum) * v
    gz_o[...] = z * _sigmoid(z)


def _rwkv_prep(ua, params, vres, tt=256):
    bsz, T, _ = ua.shape
    W = RWKV_WIDTH
    nt = T // tt
    hb = tt // SUBLANES
    n8 = T // SUBLANES
    full = lambda shape: pl.BlockSpec(shape, lambda b, i: (0,) * len(shape))
    tile = lambda n: pl.BlockSpec((None, tt, n), lambda b, i: (b, i, 0))
    in_specs = [
        tile(A_COLS),
        pl.BlockSpec((None, SUBLANES, A_COLS), lambda b, i: (b, jnp.maximum(i * hb - 1, 0), 0)),
        pl.BlockSpec((None, SUBLANES, A_COLS), lambda b, i: (b, jnp.minimum((i + 1) * hb, n8 - 1), 0)),
        full((1, A_COLS)), full((1, 2 * W)), full((2 * LOW_RANK, 2 * W)), full((1, 2 * W)),
        full((2 * LOW_RANK, 2 * W)), full((1, W)), full((1, W)), full((1, W)),
    ]
    args = [ua, ua, ua] + list(params)
    if vres is not None:
        hd, vu_pad, v0, v_first = vres
        in_specs += [tile(LANES), full((LANES, W)), full((1, W)), tile(W)]
        args += [hd, vu_pad, v0, v_first]
    dir_tile = pl.BlockSpec((2, None, tt, W), lambda b, i: (0, b, i, 0))
    out_specs = [tile(W), tile(W), tile(W), dir_tile, dir_tile, dir_tile, tile(W), tile(W)]
    one = jax.ShapeDtypeStruct((bsz, T, W), F32)
    two = jax.ShapeDtypeStruct((2, bsz, T, W), F32)
    return pl.pallas_call(
        functools.partial(_rwkv_prep_body, has_vres=vres is not None),
        grid=(bsz, nt),
        in_specs=in_specs,
        out_specs=out_specs,
        out_shape=[one, one, one, two, two, two, one, one],
        compiler_params=_cparams(("parallel", "parallel")),
        name="rwkv_prep",
    )(*args)


def _wkv_scan_body(rf_ref, vf_ref, af_ref, rb_ref, vb_ref, ab_ref, lwf_ref, kdf_ref, bbf_ref,
                   lwb_ref, kdb_ref, bbb_ref, yf_ref, yb_ref, s_ref):
    c = pl.program_id(0)
    bsz, C, _ = rf_ref.shape
    G = WKV_GROUP
    GW = G * HEAD
    NG = RWKV_HEADS // G
    assert C == HEAD

    @pl.when(c == 0)
    def _():
        s_ref[...] = jnp.zeros_like(s_ref)

    row = lax.broadcasted_iota(jnp.int32, (C, GW), 0)
    lane = lax.broadcasted_iota(jnp.int32, (C, GW), 1)
    col = lane % HEAD
    eye = (row == col).astype(F32)
    same_blk = []
    size = INV_BASE
    while size < C:
        same_blk.append(((row // size) == (col // size)).astype(F32))
        size *= 2
    head_mask = [((lane // HEAD) == g).astype(BF16) for g in range(G)]
    srow = lax.broadcasted_iota(jnp.int32, (GW, GW), 0) // HEAD
    slane = lax.broadcasted_iota(jnp.int32, (GW, GW), 1) // HEAD
    state_mask = (srow == slane).astype(F32)
    trow = lax.broadcasted_iota(jnp.int32, (C, C), 0)
    tcol = lax.broadcasted_iota(jnp.int32, (C, C), 1)
    incl_d = [(col <= row).astype(F32), (col >= row).astype(F32)]
    strict_d = [(col < row).astype(F32), (col > row).astype(F32)]
    tri_d = [(tcol <= trow).astype(BF16), (tcol >= trow).astype(BF16)]

    b16 = lambda x: x.astype(BF16)
    bd = lambda xb: jnp.concatenate([xb * m for m in head_mask], axis=0)
    cat = lambda a, b, axis: jnp.concatenate([a, b], axis=axis)

    dir_refs = [(rf_ref, vf_ref, af_ref, lwf_ref, kdf_ref, bbf_ref, yf_ref),
                (rb_ref, vb_ref, ab_ref, lwb_ref, kdb_ref, bbb_ref, yb_ref)]
    chains = []
    at_g, rt_g, v_g, bt_g, kt_g, bh_g, kh_g, pt_g = [], [], [], [], [], [], [], []
    for d, (r_ref, v_ref, an_ref, lw_ref, kd_ref, bb_ref, _) in enumerate(dir_refs):
        for b in range(bsz):
            lw = lw_ref[b]
            cs = _dot_exact_lhs(tri_d[d], lw)
            tot = jnp.sum(lw, axis=0, keepdims=True)
            e_inv = jnp.exp(-cs)
            p_tot = jnp.exp(tot)
            rt = r_ref[b] * jnp.exp(cs)
            at = an_ref[b] * jnp.exp(cs - lw)
            bt = bb_ref[b] * e_inv
            kt = kd_ref[b] * e_inv
            bh = bt * p_tot
            kh = kt * p_tot
            vv = v_ref[b]
            for g in range(NG):
                sl = slice(g * GW, (g + 1) * GW)
                chains.append((d, b, g))
                for lst, val in ((at_g, at), (rt_g, rt), (v_g, vv), (bt_g, bt), (kt_g, kt), (bh_g, bh), (kh_g, kh),
                                 (pt_g, p_tot)):
                    lst.append(val[:, sl])
    strict = [strict_d[d] for d, _, _ in chains]
    incl = [incl_d[d] for d, _, _ in chains]

    at_b = [b16(a) for a in at_g]
    v_b = [b16(v) for v in v_g]
    ar_b = [cat(a, b16(r), 0) for a, r in zip(at_b, rt_g)]
    sb = [_dot(ar, bd(b16(x)), NT) for ar, x in zip(ar_b, bt_g)]
    sk = [_dot(ar, bd(b16(x)), NT) for ar, x in zip(ar_b, kt_g)]
    m_ab = [x[:C] * m for x, m in zip(sb, strict)]
    n_rb = [x[C:] * m for x, m in zip(sb, incl)]
    m_ak = [x[:C] * m for x, m in zip(sk, strict)]
    n_rk = [x[C:] * m for x, m in zip(sk, incl)]
    v_bd = [bd(v) for v in v_b]
    mv = [_dot(b16(m), vb) for m, vb in zip(m_ak, v_bd)]

    x = [eye + m * same_blk[0] for m in m_ab]
    for lvl in range(len(same_blk)):
        inner = same_blk[lvl]
        outer = same_blk[lvl + 1] if lvl + 1 < len(same_blk) else 1.0
        sel = outer - inner
        x_b = [b16(xi) for xi in x]
        t = [_dot(xb, bd(b16(m * sel))) for xb, m in zip(x_b, m_ab)]
        x = [xi + _dot(b16(ti), bd(xb)) for xi, ti, xb in zip(x, t, x_b)]

    w_ab = [_dot(b16(xi), cat(bd(a), bd(b16(m)), 1)) for xi, a, m in zip(x, at_b, mv)]
    w_a_b = [b16(w[:, :GW]) for w in w_ab]
    w_b_b = [b16(w[:, GW:]) for w in w_ab]
    n_rb_b = [b16(n) for n in n_rb]
    y_q = [r + _dot(n, bd(w)) for r, n, w in zip(rt_g, n_rb_b, w_a_b)]
    y_c = [_dot(cat(n, b16(nk), 1), cat(bd(w), vb, 0)) for n, nk, w, vb in zip(n_rb_b, n_rk, w_b_b, v_bd)]
    bh_b = [b16(b) for b in bh_g]
    phi = [_dot(w, b, TN) * state_mask for w, b in zip(w_a_b, bh_b)]
    psi = [_dot(cat(w, v, 0), cat(b, b16(k), 0), TN) * state_mask
           for w, v, b, k in zip(w_b_b, v_b, bh_b, kh_g)]
    for i, (d, b, g) in enumerate(chains):
        s0 = s_ref[d, b, g]
        s0_b = b16(s0)
        dir_refs[d][-1][b, :, g * GW:(g + 1) * GW] = _dot(b16(y_q[i]), s0_b, NT) + y_c[i]
        s_ref[d, b, g] = s0 * pt_g[i] + _dot(s0_b, b16(phi[i])) + psi[i]


def _wkv_scan(r, v, an, lw, kd, bb):
    bsz, T, W = r.shape
    C = CHUNK
    nc = T // C
    gw = WKV_GROUP * HEAD
    fwd = pl.BlockSpec((bsz, C, W), lambda c: (0, c, 0))
    bwd = pl.BlockSpec((bsz, C, W), lambda c: (0, nc - 1 - c, 0))
    fwd_d = pl.BlockSpec((None, bsz, C, W), lambda c: (0, 0, c, 0))
    bwd_d = pl.BlockSpec((None, bsz, C, W), lambda c: (1, 0, nc - 1 - c, 0))
    out = jax.ShapeDtypeStruct((bsz, T, W), F32)
    return pl.pallas_call(
        _wkv_scan_body,
        grid=(nc,),
        in_specs=[fwd, fwd, fwd, bwd, bwd, bwd, fwd_d, fwd_d, fwd_d, bwd_d, bwd_d, bwd_d],
        out_specs=[fwd, bwd],
        out_shape=[out, out],
        scratch_shapes=[pltpu.VMEM((2, bsz, RWKV_HEADS // WKV_GROUP, gw, gw), F32)],
        compiler_params=_cparams(("arbitrary",)),
        name="wkv_scan",
    )(r, v, an, r, v, an, lw, kd, bb, lw, kd, bb)


def _win_attn_body(q_ref, kp_ref, kc_ref, kn_ref, qg_ref, kg_ref, slope_ref, sink_ref, o_ref, bias_ref):
    first = (pl.program_id(0) == 0) & (pl.program_id(1) == 0)
    n = pl.program_id(1)
    nb = pl.num_programs(1)
    pairs = ATTN_Q_HEADS // 2
    rows2 = 2 * BLOCK

    @pl.when(first)
    def _():
        qi = lax.broadcasted_iota(jnp.int32, (rows2, 3 * BLOCK), 0) % BLOCK
        kpos = lax.broadcasted_iota(jnp.int32, (rows2, 3 * BLOCK), 1) - BLOCK
        upper = lax.broadcasted_iota(jnp.int32, (rows2, 3 * BLOCK), 0) >= BLOCK
        dist = jnp.abs(qi - kpos)
        for p in range(pairs):
            slope = jnp.where(upper, slope_ref[2 * p + 1], slope_ref[2 * p])
            bias_ref[p] = jnp.where(dist <= WINDOW, -slope * dist.astype(F32), NEG_INF)

    hq = _head_sum_matrix(ATTN_WIDTH)
    hk = _head_sum_matrix(KV_WIDTH)
    lane = lax.broadcasted_iota(jnp.int32, (BLOCK, LANES), 1)
    low = lane < HEAD
    srow = lax.broadcasted_iota(jnp.int32, (rows2, 1), 0)

    q = q_ref[:, :ATTN_WIDTH]
    q = q * lax.rsqrt(_dot_split2_rhs(q * q, hq) * (1.0 / HEAD) + RMS_EPS) * (qg_ref[...] * (HEAD ** -0.5))

    def dup(x, g):
        rolled = pltpu.roll(x, HEAD, axis=1)
        return (jnp.where(low, x, rolled) if g == 0 else jnp.where(low, rolled, x)).astype(BF16)

    k_dup, v_dup = [[], []], [[], []]
    for ref in (kp_ref, kc_ref, kn_ref):
        kx = ref[:, :KV_WIDTH]
        kx = kx * lax.rsqrt(_dot_split2_rhs(kx * kx, hk) * (1.0 / HEAD) + RMS_EPS) * kg_ref[...]
        vx = ref[:, KV_WIDTH:]
        for g in range(ATTN_KV_HEADS):
            k_dup[g].append(dup(kx, g))
            v_dup[g].append(dup(vx, g))
    k_cat = [jnp.concatenate(k_dup[g], axis=0) for g in range(ATTN_KV_HEADS)]
    v_cat = [jnp.concatenate(v_dup[g], axis=0) for g in range(ATTN_KV_HEADS)]
    edge_prev = jnp.where(n > 0, 0.0, NEG_INF)
    edge_next = jnp.where(n < nb - 1, 0.0, NEG_INF)

    for p in range(pairs):
        g = (2 * p) // ATTN_GROUP
        qp = q[:, p * LANES:(p + 1) * LANES]
        qs = jnp.concatenate([jnp.where(low, qp, 0.0), jnp.where(low, 0.0, qp)], axis=0).astype(BF16)
        s = _dot(qs, k_cat[g], NT) + bias_ref[p]
        s0 = s[:, :BLOCK] + edge_prev
        s1 = s[:, BLOCK:2 * BLOCK]
        s2 = s[:, 2 * BLOCK:] + edge_next
        sink = jnp.where(srow >= BLOCK, sink_ref[2 * p + 1], sink_ref[2 * p])
        m = jnp.maximum(jnp.max(jnp.maximum(jnp.maximum(s0, s1), s2), axis=-1, keepdims=True), sink)
        e0, e1, e2 = jnp.exp(s0 - m), jnp.exp(s1 - m), jnp.exp(s2 - m)
        den = jnp.sum(e0 + e1 + e2, axis=-1, keepdims=True) + jnp.exp(sink - m)
        e = jnp.concatenate([e0, e1, e2], axis=1).astype(BF16)
        o2 = _dot(e, v_cat[g]) * (1.0 / den)
        o = jnp.where(low, o2[:BLOCK], o2[BLOCK:])
        z_lo = ATTN_WIDTH + 2 * KV_WIDTH + p * LANES
        zp = q_ref[:, z_lo:z_lo + LANES]
        o_ref[:, p * LANES:(p + 1) * LANES] = o * (zp * _sigmoid(zp))


def _win_attn(ub, q_g, k_g, slopes, sink):
    bsz, T, _ = ub.shape
    nb = T // BLOCK
    kv_blk = ATTN_WIDTH // (2 * KV_WIDTH)
    kv = lambda f: pl.BlockSpec((None, BLOCK, 2 * KV_WIDTH), f)
    smem = pl.BlockSpec(memory_space=pltpu.SMEM)
    return pl.pallas_call(
        _win_attn_body,
        grid=(bsz, nb),
        in_specs=[
            pl.BlockSpec((None, BLOCK, B_COLS), lambda b, n: (b, n, 0)),
            kv(lambda b, n: (b, jnp.maximum(n - 1, 0), kv_blk)),
            kv(lambda b, n: (b, n, kv_blk)),
            kv(lambda b, n: (b, jnp.minimum(n + 1, nb - 1), kv_blk)),
            pl.BlockSpec((1, ATTN_WIDTH), lambda b, n: (0, 0)),
            pl.BlockSpec((1, KV_WIDTH), lambda b, n: (0, 0)),
            smem, smem,
        ],
        out_specs=pl.BlockSpec((None, BLOCK, ATTN_WIDTH), lambda b, n: (b, n, 0)),
        out_shape=jax.ShapeDtypeStruct((bsz, T, ATTN_WIDTH), F32),
        scratch_shapes=[pltpu.VMEM((ATTN_Q_HEADS // 2, 2 * BLOCK, 3 * BLOCK), F32)],
        compiler_params=_cparams(("arbitrary", "arbitrary")),
        name="win_attn",
    )(ub, ub, ub, ub, q_g, k_g, slopes, sink)


def _merge_out_body(x_ref, yf_ref, yb_ref, bonus_ref, gz_ref, ob_ref, ug_ref, p_ref, lnw_ref, lnb_ref,
                    pa_ref, pb_ref, wo_ref, pg_ref, gw_ref, pp_ref, o_ref):
    hsum = _head_sum_matrix(RWKV_WIDTH)
    y = yf_ref[...] + yb_ref[...]
    mean = _dot_exact_rhs(y, hsum) * (1.0 / HEAD)
    yc = y - mean
    var = _dot_exact_rhs(yc * yc, hsum) * (1.0 / HEAD)
    yn = yc * lax.rsqrt(var + GN_EPS) * lnw_ref[...] + lnb_ref[...]
    o_a = (yn + bonus_ref[...]) * gz_ref[...]
    y_a = _dot(o_a.astype(BF16), pa_ref[...])
    y_b = _dot(ob_ref[...].astype(BF16), pb_ref[...])
    merged = _sigmoid(ug_ref[:, :D_MODEL]) * y_a + _sigmoid(ug_ref[:, D_MODEL:]) * y_b
    x1 = x_ref[...] + _dot(merged.astype(BF16), wo_ref[...])
    ple = _dot(p_ref[...].astype(BF16), pp_ref[...])
    hn = x1 * lax.rsqrt(jnp.mean(x1 * x1, axis=-1, keepdims=True) + RMS_EPS) * pg_ref[...]
    o_ref[...] = x1 + _sigmoid(_dot(hn.astype(BF16), gw_ref[...])) * ple


def _merge_out(x2, yf, yb, bonus, gz, ob, ug, p2, lnw, lnb, pa, pb, wo, pg, gw, pp, tm=256):
    rows = x2.shape[0]
    W = RWKV_WIDTH
    tile = lambda n: pl.BlockSpec((tm, n), lambda i: (i, 0))
    full = lambda a: pl.BlockSpec(a.shape, lambda i: (0,) * a.ndim)
    return pl.pallas_call(
        _merge_out_body,
        grid=(rows // tm,),
        in_specs=[tile(D_MODEL), tile(W), tile(W), tile(W), tile(W), tile(ATTN_WIDTH),
                  tile(G_COLS), tile(PLE_DIM), full(lnw), full(lnb), full(pa), full(pb), full(wo), full(pg),
                  full(gw), full(pp)],
        out_specs=tile(D_MODEL),
        out_shape=jax.ShapeDtypeStruct((rows, D_MODEL), F32),
        compiler_params=_cparams(("parallel",)),
        name="merge_out",
    )(x2, yf, yb, bonus, gz, ob, ug, p2, lnw, lnb, pa, pb, wo, pg, gw, pp)


def _block_diag2(m):
    z = jnp.zeros_like(m[0])
    return jnp.concatenate([jnp.concatenate([m[0], z], axis=1), jnp.concatenate([z, m[1]], axis=1)], axis=0)


def kernel(x, p, norm_g, w_in, shift_mu, decay_w0, decay_up, iclr_a0, iclr_up, vres_down, vres_up, vres_v0, k_k, k_a, r_k, ln_x_w, ln_x_b, q_norm_g, k_norm_g, sink, proj_a, proj_b, w_out, ple_norm_g, ple_gate_w, ple_proj):
    bsz, T, _ = x.shape
    depth = w_in.shape[0]
    rows = bsz * T
    W = RWKV_WIDTH
    slopes = jnp.asarray(2.0 ** (-8.0 * jnp.arange(1, ATTN_Q_HEADS + 1, dtype=F32) / ATTN_Q_HEADS), F32)
    x2 = x.reshape(rows, D_MODEL)
    v_first = None
    for i in range(depth):
        splits = [A_COLS, B_COLS, G_COLS]
        w_i = w_in[i]
        if i > 0:
            pad = jnp.zeros((D_MODEL, LANES - VRES_RANK), F32)
            w_i = jnp.concatenate([w_i, vres_down[i - 1], pad], axis=1)
            splits.append(LANES)
        outs = _proj_in(x2, norm_g[i].reshape(1, D_MODEL), w_i.astype(BF16), splits)
        ua = outs[0].reshape(bsz, T, A_COLS)
        ub = outs[1].reshape(bsz, T, B_COLS)
        ug = outs[2]
        params = [shift_mu[i].reshape(1, A_COLS), decay_w0[i].reshape(1, 2 * W), _block_diag2(decay_up[i]),
                  iclr_a0[i].reshape(1, 2 * W), _block_diag2(iclr_up[i]), k_k[i].reshape(1, W),
                  k_a[i].reshape(1, W), r_k[i].reshape(1, W)]
        vres = None
        if i > 0:
            vu_pad = jnp.concatenate([vres_up[i - 1], jnp.zeros((LANES - VRES_RANK, W), F32)], axis=0)
            vres = (outs[3].reshape(bsz, T, LANES), vu_pad, vres_v0[i - 1].reshape(1, W), v_first)
        r, v, an, lw, kd, bb, bonus, gz = _rwkv_prep(ua, params, vres)
        if i == 0:
            v_first = v
        yf, yb = _wkv_scan(r, v, an, lw, kd, bb)
        ob = _win_attn(ub, jnp.tile(q_norm_g[i], ATTN_Q_HEADS).reshape(1, ATTN_WIDTH),
                       jnp.tile(k_norm_g[i], ATTN_KV_HEADS).reshape(1, KV_WIDTH), slopes, sink[i])
        x2 = _merge_out(x2, yf.reshape(rows, W), yb.reshape(rows, W), bonus.reshape(rows, W), gz.reshape(rows, W),
                        ob.reshape(rows, ATTN_WIDTH), ug, p[i].reshape(rows, PLE_DIM),
                        ln_x_w[i].reshape(1, W), ln_x_b[i].reshape(1, W), proj_a[i].astype(BF16),
                        proj_b[i].astype(BF16), w_out[i].astype(BF16), ple_norm_g[i].reshape(1, D_MODEL),
                        ple_gate_w[i].astype(BF16), ple_proj[i].astype(BF16))
    return x2.reshape(bsz, T, D_MODEL)
```

```python
import functools

import jax
import jax.numpy as jnp
from jax import lax
from jax.experimental import pallas as pl
from jax.experimental.pallas import tpu as pltpu

F32 = jnp.float32
BF16 = jnp.bfloat16

D_MODEL = 1024
PLE_DIM = 256
RWKV_WIDTH = 512
HEAD = 64
RWKV_HEADS = RWKV_WIDTH // HEAD
LOW_RANK = 64
VRES_RANK = 32
ATTN_WIDTH = 512
ATTN_Q_HEADS = ATTN_WIDTH // HEAD
ATTN_KV_HEADS = 2
ATTN_GROUP = ATTN_Q_HEADS // ATTN_KV_HEADS
KV_WIDTH = ATTN_KV_HEADS * HEAD
WINDOW = 128
BLOCK = 128
RMS_EPS = 1e-6
GN_EPS = 64e-5
NEG_INF = -1e30
A_COLS = 4 * RWKV_WIDTH + 4 * LOW_RANK
B_COLS = 2 * ATTN_WIDTH + 2 * KV_WIDTH
G_COLS = 2 * D_MODEL
LANES = 128
SUBLANES = 8
HALO = 16
CHUNK = 64
INV_BASE = 2
WKV_GROUP = 2
VMEM_LIMIT = 48 * 1024 * 1024

NN = ((1,), (0,))
NT = ((1,), (1,))
TN = ((0,), (0,))


def _dot(a, b, dims=NN):
    return lax.dot_general(a, b, (dims, ((), ())), preferred_element_type=F32)


def _split2(x):
    hi = x.astype(BF16)
    lo = (x - hi.astype(F32)).astype(BF16)
    return hi, lo


def _dot_exact_lhs(a_bf16, b, dims=NN):
    b1 = b.astype(BF16)
    r1 = b - b1.astype(F32)
    b2 = r1.astype(BF16)
    b3 = (r1 - b2.astype(F32)).astype(BF16)
    return _dot(a_bf16, b1, dims) + (_dot(a_bf16, b2, dims) + _dot(a_bf16, b3, dims))


def _dot_split2_rhs(a, b_bf16):
    hi, lo = _split2(a)
    return _dot(hi, b_bf16) + _dot(lo, b_bf16)


def _dot3_presplit(a, w_ref):
    ah, al = _split2(a)
    return _dot(ah, w_ref[0]) + (_dot(ah, w_ref[1]) + _dot(al, w_ref[0]))


def _sigmoid(x):
    return 1.0 / (1.0 + jnp.exp(-x))


def _head_sum_matrix(width):
    r = lax.broadcasted_iota(jnp.int32, (width, width), 0) // HEAD
    c = lax.broadcasted_iota(jnp.int32, (width, width), 1) // HEAD
    return (r == c).astype(BF16)


def _cparams(sem):
    return pltpu.CompilerParams(dimension_semantics=sem, vmem_limit_bytes=VMEM_LIMIT)


def _proj_prep_body(*refs, has_vres):
    if has_vres:
        (x_ref, xp_ref, xn_ref, g_ref, wa_ref, wbg_ref, mu_ref, w0_ref, dup_ref, a0_ref, iup_ref, kk_ref, ka_ref,
         rk_ref, vu_ref, v0_ref, vf_ref,
         ub_o, ug_o, r_o, v_o, an_o, lw_o, kd_o, bb_o, bonus_o, gz_o, hb_ref, ua_ref) = refs
    else:
        (x_ref, xp_ref, xn_ref, g_ref, wa_ref, wbg_ref, mu_ref, w0_ref, dup_ref, a0_ref, iup_ref, kk_ref, ka_ref,
         rk_ref,
         ub_o, ug_o, r_o, v_o, an_o, lw_o, kd_o, bb_o, bonus_o, gz_o, hb_ref, ua_ref) = refs
    tm = x_ref.shape[0]
    i = pl.program_id(1)
    last = pl.num_programs(1) - 1
    W = RWKV_WIDTH

    def normed(ref):
        x = ref[...]
        return (x * lax.rsqrt(jnp.mean(x * x, axis=-1, keepdims=True) + RMS_EPS) * g_ref[...]).astype(BF16)

    hb_ref[0:HALO] = normed(xp_ref)
    hb_ref[HALO:HALO + tm] = normed(x_ref)
    hb_ref[HALO + tm:] = normed(xn_ref)
    ua_ref[...] = _dot(hb_ref[...], wa_ref[...])
    hm = hb_ref[HALO:HALO + tm]
    ub_o[...] = _dot(hm, wbg_ref[:, :B_COLS])
    ug_o[...] = _dot(hm, wbg_ref[:, B_COLS:B_COLS + G_COLS])

    row = lax.broadcasted_iota(jnp.int32, (tm, 1), 0)
    no_prev = (row == 0) & (i == 0)
    no_next = (row == tm - 1) & (i == last)
    hsum = _head_sum_matrix(W)

    def shifted(lo, hi):
        u = ua_ref[HALO:HALO + tm, lo:hi]
        prev = jnp.where(no_prev, 0.0, ua_ref[HALO - 1:HALO - 1 + tm, lo:hi])
        nxt = jnp.where(no_next, 0.0, ua_ref[HALO + 1:HALO + 1 + tm, lo:hi])
        return u + mu_ref[:, lo:hi] * (0.5 * (prev + nxt) - u)

    r = shifted(0, W)
    k = shifted(W, 2 * W)
    v = shifted(2 * W, 3 * W)
    z = shifted(3 * W, 4 * W)
    low = shifted(4 * W, 4 * W + 4 * LOW_RANK)
    w_raw = w0_ref[...] + _dot3_presplit(jnp.tanh(low[:, :2 * LOW_RANK]), dup_ref)
    nw = -w_raw
    softplus = jnp.maximum(nw, 0.0) + jnp.log(1.0 + jnp.exp(-jnp.abs(nw)))
    lw = -jnp.exp(-softplus - 0.5)
    a = _sigmoid(a0_ref[...] + _dot3_presplit(low[:, 2 * LOW_RANK:], iup_ref))
    if has_vres:
        hd = _dot(hm, wbg_ref[:, B_COLS + G_COLS:])
        mix = _sigmoid(v0_ref[...] + _dot3_presplit(hd, vu_ref))
        v = v + (vf_ref[...] - v) * mix
    kk = k * kk_ref[...]
    ss = _dot_split2_rhs(kk * kk, hsum)
    kk = kk / jnp.maximum(jnp.sqrt(ss), 1e-12)
    ka = ka_ref[...]
    ksum = jnp.zeros_like(k)
    for d in range(2):
        a_d = a[:, d * W:(d + 1) * W]
        kd = k * (1.0 + (a_d - 1.0) * ka)
        ksum = ksum + kd
        lw_o[d] = lw[:, d * W:(d + 1) * W]
        kd_o[d] = kd
        bb_o[d] = kk * a_d
    r_o[...] = r
    v_o[...] = v
    an_o[...] = -kk
    bonus_o[...] = _dot_split2_rhs(r * ksum * rk_ref[...], hsum) * v
    gz_o[...] = z * _sigmoid(z)


def _proj_prep(x, g, wa, wbg, params, vres, tm=256):
    bsz, T, _ = x.shape
    W = RWKV_WIDTH
    nt = T // tm
    hb = tm // HALO
    nh = T // HALO
    const = lambda a: pl.BlockSpec(a.shape, lambda b, i: (0,) * a.ndim, pipeline_mode=pl.Buffered(1))
    tile = lambda n: pl.BlockSpec((None, tm, n), lambda b, i: (b, i, 0))
    in_specs = [
        tile(D_MODEL),
        pl.BlockSpec((None, HALO, D_MODEL), lambda b, i: (b, jnp.maximum(i * hb - 1, 0), 0)),
        pl.BlockSpec((None, HALO, D_MODEL), lambda b, i: (b, jnp.minimum((i + 1) * hb, nh - 1), 0)),
        const(g), const(wa), const(wbg),
    ] + [const(a) for a in params]
    args = [x, x, x, g, wa, wbg] + list(params)
    if vres is not None:
        vu_split, v0, v_first = vres
        in_specs += [const(vu_split), const(v0), tile(W)]
        args += [vu_split, v0, v_first]
    dir_tile = pl.BlockSpec((2, None, tm, W), lambda b, i: (0, b, i, 0))
    out_specs = [tile(B_COLS), tile(G_COLS), tile(W), tile(W), tile(W), dir_tile, dir_tile, dir_tile, tile(W), tile(W)]
    one = jax.ShapeDtypeStruct((bsz, T, W), F32)
    two = jax.ShapeDtypeStruct((2, bsz, T, W), F32)
    out_shape = [jax.ShapeDtypeStruct((bsz, T, B_COLS), F32), jax.ShapeDtypeStruct((bsz, T, G_COLS), F32),
                 one, one, one, two, two, two, one, one]
    return pl.pallas_call(
        functools.partial(_proj_prep_body, has_vres=vres is not None),
        grid=(bsz, nt),
        in_specs=in_specs,
        out_specs=out_specs,
        out_shape=out_shape,
        scratch_shapes=[pltpu.VMEM((tm + 2 * HALO, D_MODEL), BF16), pltpu.VMEM((tm + 2 * HALO, A_COLS), F32)],
        compiler_params=_cparams(("parallel", "parallel")),
        name="proj_prep",
    )(*args)


def _wkv_scan_body(rf_ref, vf_ref, af_ref, rb_ref, vb_ref, ab_ref, lwf_ref, kdf_ref, bbf_ref,
                   lwb_ref, kdb_ref, bbb_ref, yf_ref, yb_ref, s_ref):
    c = pl.program_id(0)
    bsz, C, _ = rf_ref.shape
    G = WKV_GROUP
    GW = G * HEAD
    NG = RWKV_HEADS // G
    assert C == HEAD

    @pl.when(c == 0)
    def _():
        s_ref[...] = jnp.zeros_like(s_ref)

    row = lax.broadcasted_iota(jnp.int32, (C, GW), 0)
    lane = lax.broadcasted_iota(jnp.int32, (C, GW), 1)
    col = lane % HEAD
    eye = (row == col).astype(F32)
    same_blk = []
    size = INV_BASE
    while size < C:
        same_blk.append(((row // size) == (col // size)).astype(F32))
        size *= 2
    head_mask = [((lane // HEAD) == g).astype(BF16) for g in range(G)]
    srow = lax.broadcasted_iota(jnp.int32, (GW, GW), 0) // HEAD
    slane = lax.broadcasted_iota(jnp.int32, (GW, GW), 1) // HEAD
    state_mask = (srow == slane).astype(F32)
    trow = lax.broadcasted_iota(jnp.int32, (C, C), 0)
    tcol = lax.broadcasted_iota(jnp.int32, (C, C), 1)
    incl_d = [(col <= row).astype(F32), (col >= row).astype(F32)]
    strict_d = [(col < row).astype(F32), (col > row).astype(F32)]
    tri_d = [(tcol <= trow).astype(BF16), (tcol >= trow).astype(BF16)]

    b16 = lambda x: x.astype(BF16)
    bd = lambda xb: jnp.concatenate([xb * m for m in head_mask], axis=0)
    cat = lambda a, b, axis: jnp.concatenate([a, b], axis=axis)

    dir_refs = [(rf_ref, vf_ref, af_ref, lwf_ref, kdf_ref, bbf_ref, yf_ref),
                (rb_ref, vb_ref, ab_ref, lwb_ref, kdb_ref, bbb_ref, yb_ref)]
    chains = []
    at_g, rt_g, v_g, bt_g, kt_g, bh_g, kh_g, pt_g = [], [], [], [], [], [], [], []
    for d, (r_ref, v_ref, an_ref, lw_ref, kd_ref, bb_ref, _) in enumerate(dir_refs):
        for b in range(bsz):
            lw = lw_ref[b]
            cs = _dot_exact_lhs(tri_d[d], lw)
            tot = jnp.sum(lw, axis=0, keepdims=True)
            e_inv = jnp.exp(-cs)
            p_tot = jnp.exp(tot)
            rt = r_ref[b] * jnp.exp(cs)
            at = an_ref[b] * jnp.exp(cs - lw)
            bt = bb_ref[b] * e_inv
            kt = kd_ref[b] * e_inv
            bh = bt * p_tot
            kh = kt * p_tot
            vv = v_ref[b]
            for g in range(NG):
                sl = slice(g * GW, (g + 1) * GW)
                chains.append((d, b, g))
                for lst, val in ((at_g, at), (rt_g, rt), (v_g, vv), (bt_g, bt), (kt_g, kt), (bh_g, bh), (kh_g, kh),
                                 (pt_g, p_tot)):
                    lst.append(val[:, sl])
    strict = [strict_d[d] for d, _, _ in chains]
    incl = [incl_d[d] for d, _, _ in chains]

    at_b = [b16(a) for a in at_g]
    v_b = [b16(v) for v in v_g]
    ar_b = [cat(a, b16(r), 0) for a, r in zip(at_b, rt_g)]
    sb = [_dot(ar, bd(b16(x)), NT) for ar, x in zip(ar_b, bt_g)]
    sk = [_dot(ar, bd(b16(x)), NT) for ar, x in zip(ar_b, kt_g)]
    m_ab = [x[:C] * m for x, m in zip(sb, strict)]
    n_rb = [x[C:] * m for x, m in zip(sb, incl)]
    m_ak = [x[:C] * m for x, m in zip(sk, strict)]
    n_rk = [x[C:] * m for x, m in zip(sk, incl)]
    v_bd = [bd(v) for v in v_b]
    mv = [_dot(b16(m), vb) for m, vb in zip(m_ak, v_bd)]

    x = [eye + m * same_blk[0] for m in m_ab]
    for lvl in range(len(same_blk)):
        inner = same_blk[lvl]
        outer = same_blk[lvl + 1] if lvl + 1 < len(same_blk) else 1.0
        sel = outer - inner
        x_b = [b16(xi) for xi in x]
        t = [_dot(xb, bd(b16(m * sel))) for xb, m in zip(x_b, m_ab)]
        x = [xi + _dot(b16(ti), bd(xb)) for xi, ti, xb in zip(x, t, x_b)]

    w_ab = [_dot(b16(xi), cat(bd(a), bd(b16(m)), 1)) for xi, a, m in zip(x, at_b, mv)]
    w_a_b = [b16(w[:, :GW]) for w in w_ab]
    w_b_b = [b16(w[:, GW:]) for w in w_ab]
    n_rb_b = [b16(n) for n in n_rb]
    y_q = [r + _dot(n, bd(w)) for r, n, w in zip(rt_g, n_rb_b, w_a_b)]
    y_c = [_dot(cat(n, b16(nk), 1), cat(bd(w), vb, 0)) for n, nk, w, vb in zip(n_rb_b, n_rk, w_b_b, v_bd)]
    bh_b = [b16(b) for b in bh_g]
    phi = [_dot(w, b, TN) * state_mask for w, b in zip(w_a_b, bh_b)]
    psi = [_dot(cat(w, v, 0), cat(b, b16(k), 0), TN) * state_mask
           for w, v, b, k in zip(w_b_b, v_b, bh_b, kh_g)]
    for i, (d, b, g) in enumerate(chains):
        s0 = s_ref[d, b, g]
        s0_b = b16(s0)
        dir_refs[d][-1][b, :, g * GW:(g + 1) * GW] = _dot(b16(y_q[i]), s0_b, NT) + y_c[i]
        s_ref[d, b, g] = s0 * pt_g[i] + _dot(s0_b, b16(phi[i])) + psi[i]


def _wkv_scan(r, v, an, lw, kd, bb):
    bsz, T, W = r.shape
    C = CHUNK
    nc = T // C
    gw = WKV_GROUP * HEAD
    fwd = pl.BlockSpec((bsz, C, W), lambda c: (0, c, 0))
    bwd = pl.BlockSpec((bsz, C, W), lambda c: (0, nc - 1 - c, 0))
    fwd_d = pl.BlockSpec((None, bsz, C, W), lambda c: (0, 0, c, 0))
    bwd_d = pl.BlockSpec((None, bsz, C, W), lambda c: (1, 0, nc - 1 - c, 0))
    out = jax.ShapeDtypeStruct((bsz, T, W), F32)
    return pl.pallas_call(
        _wkv_scan_body,
        grid=(nc,),
        in_specs=[fwd, fwd, fwd, bwd, bwd, bwd, fwd_d, fwd_d, fwd_d, bwd_d, bwd_d, bwd_d],
        out_specs=[fwd, bwd],
        out_shape=[out, out],
        scratch_shapes=[pltpu.VMEM((2, bsz, RWKV_HEADS // WKV_GROUP, gw, gw), F32)],
        compiler_params=_cparams(("arbitrary",)),
        name="wkv_scan",
    )(r, v, an, r, v, an, lw, kd, bb, lw, kd, bb)


def _win_attn_body(q_ref, kp_ref, kc_ref, kn_ref, qg_ref, kg_ref, slope_ref, sink_ref, o_ref, bias_ref):
    first = (pl.program_id(0) == 0) & (pl.program_id(1) == 0)
    n = pl.program_id(1)
    nb = pl.num_programs(1)
    pairs = ATTN_Q_HEADS // 2
    rows2 = 2 * BLOCK

    @pl.when(first)
    def _():
        qi = lax.broadcasted_iota(jnp.int32, (rows2, 3 * BLOCK), 0) % BLOCK
        kpos = lax.broadcasted_iota(jnp.int32, (rows2, 3 * BLOCK), 1) - BLOCK
        upper = lax.broadcasted_iota(jnp.int32, (rows2, 3 * BLOCK), 0) >= BLOCK
        dist = jnp.abs(qi - kpos)
        for p in range(pairs):
            slope = jnp.where(upper, slope_ref[2 * p + 1], slope_ref[2 * p])
            bias_ref[p] = jnp.where(dist <= WINDOW, -slope * dist.astype(F32), NEG_INF)

    hq = _head_sum_matrix(ATTN_WIDTH)
    hk = _head_sum_matrix(KV_WIDTH)
    lane = lax.broadcasted_iota(jnp.int32, (BLOCK, LANES), 1)
    low = lane < HEAD
    srow = lax.broadcasted_iota(jnp.int32, (rows2, 1), 0)

    q = q_ref[:, :ATTN_WIDTH]
    q = q * lax.rsqrt(_dot_split2_rhs(q * q, hq) * (1.0 / HEAD) + RMS_EPS) * (qg_ref[...] * (HEAD ** -0.5))

    def dup(x, g):
        rolled = pltpu.roll(x, HEAD, axis=1)
        return (jnp.where(low, x, rolled) if g == 0 else jnp.where(low, rolled, x)).astype(BF16)

    k_dup, v_dup = [[], []], [[], []]
    for ref in (kp_ref, kc_ref, kn_ref):
        kx = ref[:, :KV_WIDTH]
        kx = kx * lax.rsqrt(_dot_split2_rhs(kx * kx, hk) * (1.0 / HEAD) + RMS_EPS) * kg_ref[...]
        vx = ref[:, KV_WIDTH:]
        for g in range(ATTN_KV_HEADS):
            k_dup[g].append(dup(kx, g))
            v_dup[g].append(dup(vx, g))
    k_cat = [jnp.concatenate(k_dup[g], axis=0) for g in range(ATTN_KV_HEADS)]
    v_cat = [jnp.concatenate(v_dup[g], axis=0) for g in range(ATTN_KV_HEADS)]
    edge_prev = jnp.where(n > 0, 0.0, NEG_INF)
    edge_next = jnp.where(n < nb - 1, 0.0, NEG_INF)

    for p in range(pairs):
        g = (2 * p) // ATTN_GROUP
        qp = q[:, p * LANES:(p + 1) * LANES]
        qs = jnp.concatenate([jnp.where(low, qp, 0.0), jnp.where(low, 0.0, qp)], axis=0).astype(BF16)
        s = _dot(qs, k_cat[g], NT) + bias_ref[p]
        s0 = s[:, :BLOCK] + edge_prev
        s1 = s[:, BLOCK:2 * BLOCK]
        s2 = s[:, 2 * BLOCK:] + edge_next
        sink = jnp.where(srow >= BLOCK, sink_ref[2 * p + 1], sink_ref[2 * p])
        m = jnp.maximum(jnp.max(jnp.maximum(jnp.maximum(s0, s1), s2), axis=-1, keepdims=True), sink)
        e0, e1, e2 = jnp.exp(s0 - m), jnp.exp(s1 - m), jnp.exp(s2 - m)
        den = jnp.sum(e0 + e1 + e2, axis=-1, keepdims=True) + jnp.exp(sink - m)
        e = jnp.concatenate([e0, e1, e2], axis=1).astype(BF16)
        o2 = _dot(e, v_cat[g]) * (1.0 / den)
        o = jnp.where(low, o2[:BLOCK], o2[BLOCK:])
        z_lo = ATTN_WIDTH + 2 * KV_WIDTH + p * LANES
        zp = q_ref[:, z_lo:z_lo + LANES]
        o_ref[:, p * LANES:(p + 1) * LANES] = o * (zp * _sigmoid(zp))


def _win_attn(ub, q_g, k_g, slopes, sink):
    bsz, T, _ = ub.shape
    nb = T // BLOCK
    kv_blk = ATTN_WIDTH // (2 * KV_WIDTH)
    kv = lambda f: pl.BlockSpec((None, BLOCK, 2 * KV_WIDTH), f)
    smem = pl.BlockSpec(memory_space=pltpu.SMEM)
    return pl.pallas_call(
        _win_attn_body,
        grid=(bsz, nb),
        in_specs=[
            pl.BlockSpec((None, BLOCK, B_COLS), lambda b, n: (b, n, 0)),
            kv(lambda b, n: (b, jnp.maximum(n - 1, 0), kv_blk)),
            kv(lambda b, n: (b, n, kv_blk)),
            kv(lambda b, n: (b, jnp.minimum(n + 1, nb - 1), kv_blk)),
            pl.BlockSpec((1, ATTN_WIDTH), lambda b, n: (0, 0)),
            pl.BlockSpec((1, KV_WIDTH), lambda b, n: (0, 0)),
            smem, smem,
        ],
        out_specs=pl.BlockSpec((None, BLOCK, ATTN_WIDTH), lambda b, n: (b, n, 0)),
        out_shape=jax.ShapeDtypeStruct((bsz, T, ATTN_WIDTH), F32),
        scratch_shapes=[pltpu.VMEM((ATTN_Q_HEADS // 2, 2 * BLOCK, 3 * BLOCK), F32)],
        compiler_params=_cparams(("arbitrary", "arbitrary")),
        name="win_attn",
    )(ub, ub, ub, ub, q_g, k_g, slopes, sink)


def _merge_out_body(x_ref, yf_ref, yb_ref, bonus_ref, gz_ref, ob_ref, ug_ref, p_ref, lnw_ref, lnb_ref,
                    pa_ref, pb_ref, wo_ref, pg_ref, gw_ref, pp_ref, o_ref):
    hsum = _head_sum_matrix(RWKV_WIDTH)
    y = yf_ref[...] + yb_ref[...]
    mean = _dot_split2_rhs(y, hsum) * (1.0 / HEAD)
    yc = y - mean
    var = _dot_split2_rhs(yc * yc, hsum) * (1.0 / HEAD)
    yn = yc * lax.rsqrt(var + GN_EPS) * lnw_ref[...] + lnb_ref[...]
    o_a = (yn + bonus_ref[...]) * gz_ref[...]
    y_a = _dot(o_a.astype(BF16), pa_ref[...])
    y_b = _dot(ob_ref[...].astype(BF16), pb_ref[...])
    merged = _sigmoid(ug_ref[:, :D_MODEL]) * y_a + _sigmoid(ug_ref[:, D_MODEL:]) * y_b
    x1 = x_ref[...] + _dot(merged.astype(BF16), wo_ref[...])
    ple = _dot(p_ref[...].astype(BF16), pp_ref[...])
    hn = x1 * lax.rsqrt(jnp.mean(x1 * x1, axis=-1, keepdims=True) + RMS_EPS) * pg_ref[...]
    o_ref[...] = x1 + _sigmoid(_dot(hn.astype(BF16), gw_ref[...])) * ple


def _merge_out(x2, yf, yb, bonus, gz, ob, ug, p2, lnw, lnb, pa, pb, wo, pg, gw, pp, tm=256):
    rows = x2.shape[0]
    W = RWKV_WIDTH
    tile = lambda n: pl.BlockSpec((tm, n), lambda i: (i, 0))
    full = lambda a: pl.BlockSpec(a.shape, lambda i: (0,) * a.ndim)
    return pl.pallas_call(
        _merge_out_body,
        grid=(rows // tm,),
        in_specs=[tile(D_MODEL), tile(W), tile(W), tile(W), tile(W), tile(ATTN_WIDTH),
                  tile(G_COLS), tile(PLE_DIM), full(lnw), full(lnb), full(pa), full(pb), full(wo), full(pg),
                  full(gw), full(pp)],
        out_specs=tile(D_MODEL),
        out_shape=jax.ShapeDtypeStruct((rows, D_MODEL), F32),
        compiler_params=_cparams(("parallel",)),
        name="merge_out",
    )(x2, yf, yb, bonus, gz, ob, ug, p2, lnw, lnb, pa, pb, wo, pg, gw, pp)


def _block_diag2(m):
    z = jnp.zeros_like(m[0])
    return jnp.concatenate([jnp.concatenate([m[0], z], axis=1), jnp.concatenate([z, m[1]], axis=1)], axis=0)


def _presplit(w):
    hi = w.astype(BF16)
    return jnp.stack([hi, (w - hi.astype(F32)).astype(BF16)])


def kernel(x, p, norm_g, w_in, shift_mu, decay_w0, decay_up, iclr_a0, iclr_up, vres_down, vres_up, vres_v0, k_k, k_a, r_k, ln_x_w, ln_x_b, q_norm_g, k_norm_g, sink, proj_a, proj_b, w_out, ple_norm_g, ple_gate_w, ple_proj):
    bsz, T, _ = x.shape
    depth = w_in.shape[0]
    rows = bsz * T
    W = RWKV_WIDTH
    slopes = jnp.asarray(2.0 ** (-8.0 * jnp.arange(1, ATTN_Q_HEADS + 1, dtype=F32) / ATTN_Q_HEADS), F32)
    v_first = None
    for i in range(depth):
        wa = w_in[i][:, :A_COLS].astype(BF16)
        wbg = w_in[i][:, A_COLS:]
        if i > 0:
            wbg = jnp.concatenate([wbg, vres_down[i - 1], jnp.zeros((D_MODEL, LANES - VRES_RANK), F32)], axis=1)
        params = [shift_mu[i].reshape(1, A_COLS), decay_w0[i].reshape(1, 2 * W), _presplit(_block_diag2(decay_up[i])),
                  iclr_a0[i].reshape(1, 2 * W), _presplit(_block_diag2(iclr_up[i])), k_k[i].reshape(1, W),
                  k_a[i].reshape(1, W), r_k[i].reshape(1, W)]
        vres = None
        if i > 0:
            vu_pad = jnp.concatenate([vres_up[i - 1], jnp.zeros((LANES - VRES_RANK, W), F32)], axis=0)
            vres = (_presplit(vu_pad), vres_v0[i - 1].reshape(1, W), v_first)
        ub, ug, r, v, an, lw, kd, bb, bonus, gz = _proj_prep(x, norm_g[i].reshape(1, D_MODEL), wa, wbg.astype(BF16),
                                                             params, vres)
        if i == 0:
            v_first = v
        yf, yb = _wkv_scan(r, v, an, lw, kd, bb)
        ob = _win_attn(ub, jnp.tile(q_norm_g[i], ATTN_Q_HEADS).reshape(1, ATTN_WIDTH),
                       jnp.tile(k_norm_g[i], ATTN_KV_HEADS).reshape(1, KV_WIDTH), slopes, sink[i])
        x2 = _merge_out(x.reshape(rows, D_MODEL), yf.reshape(rows, W), yb.reshape(rows, W), bonus.reshape(rows, W),
                        gz.reshape(rows, W), ob.reshape(rows, ATTN_WIDTH), ug.reshape(rows, G_COLS),
                        p[i].reshape(rows, PLE_DIM), ln_x_w[i].reshape(1, W), ln_x_b[i].reshape(1, W),
                        proj_a[i].astype(BF16), proj_b[i].astype(BF16), w_out[i].astype(BF16),
                        ple_norm_g[i].reshape(1, D_MODEL), ple_gate_w[i].astype(BF16), ple_proj[i].astype(BF16))
        x = x2.reshape(bsz, T, D_MODEL)
    return x
```

```python
import functools

import jax
import jax.numpy as jnp
from jax import lax
from jax.experimental import pallas as pl
from jax.experimental.pallas import tpu as pltpu

F32 = jnp.float32
BF16 = jnp.bfloat16

D_MODEL = 1024
PLE_DIM = 256
RWKV_WIDTH = 512
HEAD = 64
RWKV_HEADS = RWKV_WIDTH // HEAD
LOW_RANK = 64
VRES_RANK = 32
ATTN_WIDTH = 512
ATTN_Q_HEADS = ATTN_WIDTH // HEAD
ATTN_KV_HEADS = 2
ATTN_GROUP = ATTN_Q_HEADS // ATTN_KV_HEADS
KV_WIDTH = ATTN_KV_HEADS * HEAD
WINDOW = 128
BLOCK = 128
RMS_EPS = 1e-6
GN_EPS = 64e-5
NEG_INF = -1e30
A_COLS = 4 * RWKV_WIDTH + 4 * LOW_RANK
B_COLS = 2 * ATTN_WIDTH + 2 * KV_WIDTH
G_COLS = 2 * D_MODEL
LANES = 128
SUBLANES = 8
HALO = 16
CHUNK = 64
INV_BASE = 2
WKV_GROUP = 2
VMEM_LIMIT = 48 * 1024 * 1024

NN = ((1,), (0,))
NT = ((1,), (1,))
TN = ((0,), (0,))


def _dot(a, b, dims=NN):
    return lax.dot_general(a, b, (dims, ((), ())), preferred_element_type=F32)


def _split2(x):
    hi = x.astype(BF16)
    lo = (x - hi.astype(F32)).astype(BF16)
    return hi, lo


def _dot_exact_lhs(a_bf16, b, dims=NN):
    b1 = b.astype(BF16)
    r1 = b - b1.astype(F32)
    b2 = r1.astype(BF16)
    b3 = (r1 - b2.astype(F32)).astype(BF16)
    return _dot(a_bf16, b1, dims) + (_dot(a_bf16, b2, dims) + _dot(a_bf16, b3, dims))


def _dot_split2_rhs(a, b_bf16):
    hi, lo = _split2(a)
    return _dot(hi, b_bf16) + _dot(lo, b_bf16)


def _dot3_presplit(a, w_ref):
    ah, al = _split2(a)
    return _dot(ah, w_ref[0]) + (_dot(ah, w_ref[1]) + _dot(al, w_ref[0]))


def _zero_after(x):
    bits = lax.bitcast_convert_type(x, jnp.int32)
    half = jnp.int32(16)
    return lax.shift_right_logical(lax.shift_right_logical(bits, half), half).astype(F32)


def _sigmoid(x):
    return 1.0 / (1.0 + jnp.exp(-x))


def _head_sum_matrix(width):
    r = lax.broadcasted_iota(jnp.int32, (width, width), 0) // HEAD
    c = lax.broadcasted_iota(jnp.int32, (width, width), 1) // HEAD
    return (r == c).astype(BF16)


def _cparams(sem):
    return pltpu.CompilerParams(dimension_semantics=sem, vmem_limit_bytes=VMEM_LIMIT)


def _proj_prep_body(*refs, has_vres):
    if has_vres:
        (x_ref, xp_ref, xn_ref, g_ref, wa_ref, wbg_ref, mu_ref, w0_ref, dup_ref, a0_ref, iup_ref, kk_ref, ka_ref,
         rk_ref, vu_ref, v0_ref, vf_ref,
         ub_o, ug_o, r_o, v_o, an_o, lw_o, kd_o, bb_o, bonus_o, gz_o, hb_ref, ua_ref) = refs
    else:
        (x_ref, xp_ref, xn_ref, g_ref, wa_ref, wbg_ref, mu_ref, w0_ref, dup_ref, a0_ref, iup_ref, kk_ref, ka_ref,
         rk_ref,
         ub_o, ug_o, r_o, v_o, an_o, lw_o, kd_o, bb_o, bonus_o, gz_o, hb_ref, ua_ref) = refs
    tm = x_ref.shape[0]
    i = pl.program_id(1)
    last = pl.num_programs(1) - 1
    W = RWKV_WIDTH

    def normed(ref):
        x = ref[...]
        return (x * lax.rsqrt(jnp.mean(x * x, axis=-1, keepdims=True) + RMS_EPS) * g_ref[...]).astype(BF16)

    hb_ref[0:HALO] = normed(xp_ref)
    hb_ref[HALO:HALO + tm] = normed(x_ref)
    hb_ref[HALO + tm:] = normed(xn_ref)
    ua_ref[...] = _dot(hb_ref[...], wa_ref[...])
    hm = hb_ref[HALO:HALO + tm]
    ub_o[...] = _dot(hm, wbg_ref[:, :B_COLS])
    ug_o[...] = _dot(hm, wbg_ref[:, B_COLS:B_COLS + G_COLS])

    row = lax.broadcasted_iota(jnp.int32, (tm, 1), 0)
    no_prev = (row == 0) & (i == 0)
    no_next = (row == tm - 1) & (i == last)
    hsum = _head_sum_matrix(W)

    def shifted(lo, hi):
        u = ua_ref[HALO:HALO + tm, lo:hi]
        prev = jnp.where(no_prev, 0.0, ua_ref[HALO - 1:HALO - 1 + tm, lo:hi])
        nxt = jnp.where(no_next, 0.0, ua_ref[HALO + 1:HALO + 1 + tm, lo:hi])
        return u + mu_ref[:, lo:hi] * (0.5 * (prev + nxt) - u)

    r = shifted(0, W)
    k = shifted(W, 2 * W)
    v = shifted(2 * W, 3 * W)
    z = shifted(3 * W, 4 * W)
    low = shifted(4 * W, 4 * W + 4 * LOW_RANK)
    w_raw = w0_ref[...] + _dot3_presplit(jnp.tanh(low[:, :2 * LOW_RANK]), dup_ref)
    nw = -w_raw
    softplus = jnp.maximum(nw, 0.0) + jnp.log(1.0 + jnp.exp(-jnp.abs(nw)))
    lw = -jnp.exp(-softplus - 0.5)
    a = _sigmoid(a0_ref[...] + _dot3_presplit(low[:, 2 * LOW_RANK:], iup_ref))
    if has_vres:
        hd = _dot(hm, wbg_ref[:, B_COLS + G_COLS:])
        mix = _sigmoid(v0_ref[...] + _dot3_presplit(hd, vu_ref))
        v = v + (vf_ref[...] - v) * mix
    kk = k * kk_ref[...]
    ss = _dot_split2_rhs(kk * kk, hsum)
    kk = kk / jnp.maximum(jnp.sqrt(ss), 1e-12)
    ka = ka_ref[...]
    ksum = jnp.zeros_like(k)
    for d in range(2):
        a_d = a[:, d * W:(d + 1) * W]
        kd = k * (1.0 + (a_d - 1.0) * ka)
        ksum = ksum + kd
        lw_o[d] = lw[:, d * W:(d + 1) * W]
        kd_o[d] = kd
        bb_o[d] = kk * a_d
    r_o[...] = r
    v_o[...] = v
    an_o[...] = -kk
    bonus_o[...] = _dot_split2_rhs(r * ksum * rk_ref[...], hsum) * v
    gz_o[...] = z * _sigmoid(z)


def _proj_prep(x, g, wa, wbg, params, vres, tm=256):
    bsz, T, _ = x.shape
    W = RWKV_WIDTH
    nt = T // tm
    hb = tm // HALO
    nh = T // HALO
    const = lambda a: pl.BlockSpec(a.shape, lambda b, i: (0,) * a.ndim, pipeline_mode=pl.Buffered(1))
    tile = lambda n: pl.BlockSpec((None, tm, n), lambda b, i: (b, i, 0))
    in_specs = [
        tile(D_MODEL),
        pl.BlockSpec((None, HALO, D_MODEL), lambda b, i: (b, jnp.maximum(i * hb - 1, 0), 0)),
        pl.BlockSpec((None, HALO, D_MODEL), lambda b, i: (b, jnp.minimum((i + 1) * hb, nh - 1), 0)),
        const(g), const(wa), const(wbg),
    ] + [const(a) for a in params]
    args = [x, x, x, g, wa, wbg] + list(params)
    if vres is not None:
        vu_split, v0, v_first = vres
        in_specs += [const(vu_split), const(v0), tile(W)]
        args += [vu_split, v0, v_first]
    dir_tile = pl.BlockSpec((2, None, tm, W), lambda b, i: (0, b, i, 0))
    out_specs = [tile(B_COLS), tile(G_COLS), tile(W), tile(W), tile(W), dir_tile, dir_tile, dir_tile, tile(W), tile(W)]
    one = jax.ShapeDtypeStruct((bsz, T, W), F32)
    two = jax.ShapeDtypeStruct((2, bsz, T, W), F32)
    out_shape = [jax.ShapeDtypeStruct((bsz, T, B_COLS), F32), jax.ShapeDtypeStruct((bsz, T, G_COLS), F32),
                 one, one, one, two, two, two, one, one]
    return pl.pallas_call(
        functools.partial(_proj_prep_body, has_vres=vres is not None),
        grid=(bsz, nt),
        in_specs=in_specs,
        out_specs=out_specs,
        out_shape=out_shape,
        scratch_shapes=[pltpu.VMEM((tm + 2 * HALO, D_MODEL), BF16), pltpu.VMEM((tm + 2 * HALO, A_COLS), F32)],
        compiler_params=_cparams(("parallel", "parallel")),
        name="proj_prep",
    )(*args)


def _wkv_step(side_work, rf_ref, vf_ref, af_ref, rb_ref, vb_ref, ab_ref, lwf_ref, kdf_ref, bbf_ref,
              lwb_ref, kdb_ref, bbb_ref, yf_ref, yb_ref, s_ref):
    bsz, C, _ = rf_ref.shape
    G = WKV_GROUP
    GW = G * HEAD
    NG = RWKV_HEADS // G
    assert C == HEAD

    row = lax.broadcasted_iota(jnp.int32, (C, GW), 0)
    lane = lax.broadcasted_iota(jnp.int32, (C, GW), 1)
    col = lane % HEAD
    eye = (row == col).astype(F32)
    same_blk = []
    size = INV_BASE
    while size < C:
        same_blk.append(((row // size) == (col // size)).astype(F32))
        size *= 2
    head_mask = [((lane // HEAD) == g).astype(BF16) for g in range(G)]
    srow = lax.broadcasted_iota(jnp.int32, (GW, GW), 0) // HEAD
    slane = lax.broadcasted_iota(jnp.int32, (GW, GW), 1) // HEAD
    state_mask = (srow == slane).astype(F32)
    trow = lax.broadcasted_iota(jnp.int32, (C, C), 0)
    tcol = lax.broadcasted_iota(jnp.int32, (C, C), 1)
    incl_d = [(col <= row).astype(F32), (col >= row).astype(F32)]
    strict_d = [(col < row).astype(F32), (col > row).astype(F32)]
    tri_d = [(tcol <= trow).astype(BF16), (tcol >= trow).astype(BF16)]

    b16 = lambda x: x.astype(BF16)
    bd = lambda xb: jnp.concatenate([xb * m for m in head_mask], axis=0)
    cat = lambda a, b, axis: jnp.concatenate([a, b], axis=axis)

    dir_refs = [(rf_ref, vf_ref, af_ref, lwf_ref, kdf_ref, bbf_ref, yf_ref),
                (rb_ref, vb_ref, ab_ref, lwb_ref, kdb_ref, bbb_ref, yb_ref)]
    chains = []
    at_g, rt_g, v_g, bt_g, kt_g, bh_g, kh_g, pt_g = [], [], [], [], [], [], [], []
    for d, (r_ref, v_ref, an_ref, lw_ref, kd_ref, bb_ref, _) in enumerate(dir_refs):
        for b in range(bsz):
            lw = lw_ref[b]
            cs = _dot_exact_lhs(tri_d[d], lw)
            tot = jnp.sum(lw, axis=0, keepdims=True)
            e_inv = jnp.exp(-cs)
            p_tot = jnp.exp(tot)
            rt = r_ref[b] * jnp.exp(cs)
            at = an_ref[b] * jnp.exp(cs - lw)
            bt = bb_ref[b] * e_inv
            kt = kd_ref[b] * e_inv
            bh = bt * p_tot
            kh = kt * p_tot
            vv = v_ref[b]
            for g in range(NG):
                sl = slice(g * GW, (g + 1) * GW)
                chains.append((d, b, g))
                for lst, val in ((at_g, at), (rt_g, rt), (v_g, vv), (bt_g, bt), (kt_g, kt), (bh_g, bh), (kh_g, kh),
                                 (pt_g, p_tot)):
                    lst.append(val[:, sl])
    strict = [strict_d[d] for d, _, _ in chains]
    incl = [incl_d[d] for d, _, _ in chains]

    at_b = [b16(a) for a in at_g]
    v_b = [b16(v) for v in v_g]
    ar_b = [cat(a, b16(r), 0) for a, r in zip(at_b, rt_g)]
    sb = [_dot(ar, bd(b16(x)), NT) for ar, x in zip(ar_b, bt_g)]
    sk = [_dot(ar, bd(b16(x)), NT) for ar, x in zip(ar_b, kt_g)]
    m_ab = [x[:C] * m for x, m in zip(sb, strict)]
    n_rb = [x[C:] * m for x, m in zip(sb, incl)]
    m_ak = [x[:C] * m for x, m in zip(sk, strict)]
    n_rk = [x[C:] * m for x, m in zip(sk, incl)]
    v_bd = [bd(v) for v in v_b]
    mv = [_dot(b16(m), vb) for m, vb in zip(m_ak, v_bd)]

    x = [eye + m * same_blk[0] for m in m_ab]
    for lvl in range(len(same_blk)):
        inner = same_blk[lvl]
        outer = same_blk[lvl + 1] if lvl + 1 < len(same_blk) else 1.0
        sel = outer - inner
        if lvl in side_work:
            done = side_work[lvl](_zero_after(x[0][:SUBLANES]))
        x_b = [b16(xi) for xi in x]
        t = [_dot(xb, bd(b16(m * sel))) for xb, m in zip(x_b, m_ab)]
        x = [xi + _dot(b16(ti), bd(xb)) for xi, ti, xb in zip(x, t, x_b)]
        if lvl in side_work:
            hold = jnp.tile(done, (C // SUBLANES, 1))
            x = [xi + hold for xi in x]

    w_ab = [_dot(b16(xi), cat(bd(a), bd(b16(m)), 1)) for xi, a, m in zip(x, at_b, mv)]
    w_a_b = [b16(w[:, :GW]) for w in w_ab]
    w_b_b = [b16(w[:, GW:]) for w in w_ab]
    n_rb_b = [b16(n) for n in n_rb]
    y_q = [r + _dot(n, bd(w)) for r, n, w in zip(rt_g, n_rb_b, w_a_b)]
    y_c = [_dot(cat(n, b16(nk), 1), cat(bd(w), vb, 0)) for n, nk, w, vb in zip(n_rb_b, n_rk, w_b_b, v_bd)]
    bh_b = [b16(b) for b in bh_g]
    phi = [_dot(w, b, TN) * state_mask for w, b in zip(w_a_b, bh_b)]
    psi = [_dot(cat(w, v, 0), cat(b, b16(k), 0), TN) * state_mask
           for w, v, b, k in zip(w_b_b, v_b, bh_b, kh_g)]
    for i, (d, b, g) in enumerate(chains):
        s0 = s_ref[d, b, g]
        s0_b = b16(s0)
        dir_refs[d][-1][b, :, g * GW:(g + 1) * GW] = _dot(b16(y_q[i]), s0_b, NT) + y_c[i]
        s_ref[d, b, g] = s0 * pt_g[i] + _dot(s0_b, b16(phi[i])) + psi[i]


def _attn_bias_init(slope_ref, bias_ref):
    rows2 = 2 * BLOCK
    qi = lax.broadcasted_iota(jnp.int32, (rows2, 3 * BLOCK), 0) % BLOCK
    kpos = lax.broadcasted_iota(jnp.int32, (rows2, 3 * BLOCK), 1) - BLOCK
    upper = lax.broadcasted_iota(jnp.int32, (rows2, 3 * BLOCK), 0) >= BLOCK
    dist = jnp.abs(qi - kpos)
    for p in range(ATTN_Q_HEADS // 2):
        slope = jnp.where(upper, slope_ref[2 * p + 1], slope_ref[2 * p])
        bias_ref[p] = jnp.where(dist <= WINDOW, -slope * dist.astype(F32), NEG_INF)


def _attn_step(n, nb, q_ref, kp_ref, kc_ref, kn_ref, qg_ref, kg_ref, slope_ref, sink_ref, o_ref, bias_ref):
    pairs = ATTN_Q_HEADS // 2
    rows2 = 2 * BLOCK
    hq = _head_sum_matrix(ATTN_WIDTH)
    hk = _head_sum_matrix(KV_WIDTH)
    lane = lax.broadcasted_iota(jnp.int32, (BLOCK, LANES), 1)
    low = lane < HEAD
    srow = lax.broadcasted_iota(jnp.int32, (rows2, 1), 0)

    q = q_ref[:, :ATTN_WIDTH]
    q = q * lax.rsqrt(_dot_split2_rhs(q * q, hq) * (1.0 / HEAD) + RMS_EPS) * (qg_ref[...] * (HEAD ** -0.5))

    def dup(x, g):
        rolled = pltpu.roll(x, HEAD, axis=1)
        return (jnp.where(low, x, rolled) if g == 0 else jnp.where(low, rolled, x)).astype(BF16)

    k_dup, v_dup = [[], []], [[], []]
    for ref in (kp_ref, kc_ref, kn_ref):
        kx = ref[:, :KV_WIDTH]
        kx = kx * lax.rsqrt(_dot_split2_rhs(kx * kx, hk) * (1.0 / HEAD) + RMS_EPS) * kg_ref[...]
        vx = ref[:, KV_WIDTH:]
        for g in range(ATTN_KV_HEADS):
            k_dup[g].append(dup(kx, g))
            v_dup[g].append(dup(vx, g))
    k_cat = [jnp.concatenate(k_dup[g], axis=0) for g in range(ATTN_KV_HEADS)]
    v_cat = [jnp.concatenate(v_dup[g], axis=0) for g in range(ATTN_KV_HEADS)]
    edge_prev = jnp.where(n > 0, 0.0, NEG_INF)
    edge_next = jnp.where(n < nb - 1, 0.0, NEG_INF)

    def head_pair(p, start):
        g = (2 * p) // ATTN_GROUP
        qp = q[:, p * LANES:(p + 1) * LANES] + jnp.tile(start, (BLOCK // SUBLANES, 1))
        qs = jnp.concatenate([jnp.where(low, qp, 0.0), jnp.where(low, 0.0, qp)], axis=0).astype(BF16)
        s = _dot(qs, k_cat[g], NT) + bias_ref[p]
        s0 = s[:, :BLOCK] + edge_prev
        s1 = s[:, BLOCK:2 * BLOCK]
        s2 = s[:, 2 * BLOCK:] + edge_next
        sink = jnp.where(srow >= BLOCK, sink_ref[2 * p + 1], sink_ref[2 * p])
        m = jnp.maximum(jnp.max(jnp.maximum(jnp.maximum(s0, s1), s2), axis=-1, keepdims=True), sink)
        e0, e1, e2 = jnp.exp(s0 - m), jnp.exp(s1 - m), jnp.exp(s2 - m)
        den = jnp.sum(e0 + e1 + e2, axis=-1, keepdims=True) + jnp.exp(sink - m)
        e = jnp.concatenate([e0, e1, e2], axis=1).astype(BF16)
        o2 = _dot(e, v_cat[g]) * (1.0 / den)
        o = jnp.where(low, o2[:BLOCK], o2[BLOCK:])
        z_lo = ATTN_WIDTH + 2 * KV_WIDTH + p * LANES
        zp = q_ref[:, z_lo:z_lo + LANES]
        og = o * (zp * _sigmoid(zp))
        o_ref[:, p * LANES:(p + 1) * LANES] = og
        return _zero_after(og[:SUBLANES])

    return head_pair


N_WKV_IN = 12
N_ATTN_IN = 8
ATTN_JOIN_LEVELS = (1, 2, 3, 4)


def _wkv_attn_body(*refs, nb):
    wkv_in = refs[:N_WKV_IN]
    attn_in = refs[N_WKV_IN:N_WKV_IN + N_ATTN_IN]
    yf_ref, yb_ref, o_ref, s_ref, bias_ref = refs[N_WKV_IN + N_ATTN_IN:]
    c = pl.program_id(0)

    @pl.when(c == 0)
    def _():
        s_ref[...] = jnp.zeros_like(s_ref)
        _attn_bias_init(attn_in[6], bias_ref)

    head_pair = _attn_step(c % nb, nb, *attn_in, o_ref, bias_ref)
    side_work = {lvl: functools.partial(head_pair, p) for p, lvl in enumerate(ATTN_JOIN_LEVELS)}
    _wkv_step(side_work, *wkv_in, yf_ref, yb_ref, s_ref)


def _wkv_attn(r, v, an, lw, kd, bb, ub, q_g, k_g, slopes, sink):
    bsz, T, W = r.shape
    C = CHUNK
    nc = T // C
    nb = T // BLOCK
    assert nc == bsz * nb
    gw = WKV_GROUP * HEAD
    fwd = pl.BlockSpec((bsz, C, W), lambda c: (0, c, 0))
    bwd = pl.BlockSpec((bsz, C, W), lambda c: (0, nc - 1 - c, 0))
    fwd_d = pl.BlockSpec((None, bsz, C, W), lambda c: (0, 0, c, 0))
    bwd_d = pl.BlockSpec((None, bsz, C, W), lambda c: (1, 0, nc - 1 - c, 0))
    kv_blk = ATTN_WIDTH // (2 * KV_WIDTH)
    kv = lambda f: pl.BlockSpec((None, BLOCK, 2 * KV_WIDTH), f)
    smem = pl.BlockSpec(memory_space=pltpu.SMEM)
    attn_specs = [
        pl.BlockSpec((None, BLOCK, B_COLS), lambda c: (c // nb, c % nb, 0)),
        kv(lambda c: (c // nb, jnp.maximum(c % nb - 1, 0), kv_blk)),
        kv(lambda c: (c // nb, c % nb, kv_blk)),
        kv(lambda c: (c // nb, jnp.minimum(c % nb + 1, nb - 1), kv_blk)),
        pl.BlockSpec((1, ATTN_WIDTH), lambda c: (0, 0)),
        pl.BlockSpec((1, KV_WIDTH), lambda c: (0, 0)),
        smem, smem,
    ]
    out = jax.ShapeDtypeStruct((bsz, T, W), F32)
    return pl.pallas_call(
        functools.partial(_wkv_attn_body, nb=nb),
        grid=(nc,),
        in_specs=[fwd, fwd, fwd, bwd, bwd, bwd, fwd_d, fwd_d, fwd_d, bwd_d, bwd_d, bwd_d] + attn_specs,
        out_specs=[fwd, bwd, pl.BlockSpec((None, BLOCK, ATTN_WIDTH), lambda c: (c // nb, c % nb, 0))],
        out_shape=[out, out, jax.ShapeDtypeStruct((bsz, T, ATTN_WIDTH), F32)],
        scratch_shapes=[pltpu.VMEM((2, bsz, RWKV_HEADS // WKV_GROUP, gw, gw), F32),
                        pltpu.VMEM((ATTN_Q_HEADS // 2, 2 * BLOCK, 3 * BLOCK), F32)],
        compiler_params=_cparams(("arbitrary",)),
        name="wkv_attn",
    )(r, v, an, r, v, an, lw, kd, bb, lw, kd, bb, ub, ub, ub, ub, q_g, k_g, slopes, sink)


def _merge_out_body(x_ref, yf_ref, yb_ref, bonus_ref, gz_ref, ob_ref, ug_ref, p_ref, lnw_ref, lnb_ref,
                    pa_ref, pb_ref, wo_ref, pg_ref, gw_ref, pp_ref, o_ref):
    hsum = _head_sum_matrix(RWKV_WIDTH)
    y = yf_ref[...] + yb_ref[...]
    mean = _dot_split2_rhs(y, hsum) * (1.0 / HEAD)
    yc = y - mean
    var = _dot_split2_rhs(yc * yc, hsum) * (1.0 / HEAD)
    yn = yc * lax.rsqrt(var + GN_EPS) * lnw_ref[...] + lnb_ref[...]
    o_a = (yn + bonus_ref[...]) * gz_ref[...]
    y_a = _dot(o_a.astype(BF16), pa_ref[...])
    y_b = _dot(ob_ref[...].astype(BF16), pb_ref[...])
    merged = _sigmoid(ug_ref[:, :D_MODEL]) * y_a + _sigmoid(ug_ref[:, D_MODEL:]) * y_b
    x1 = x_ref[...] + _dot(merged.astype(BF16), wo_ref[...])
    ple = _dot(p_ref[...].astype(BF16), pp_ref[...])
    hn = x1 * lax.rsqrt(jnp.mean(x1 * x1, axis=-1, keepdims=True) + RMS_EPS) * pg_ref[...]
    o_ref[...] = x1 + _sigmoid(_dot(hn.astype(BF16), gw_ref[...])) * ple


def _merge_out(x2, yf, yb, bonus, gz, ob, ug, p2, lnw, lnb, pa, pb, wo, pg, gw, pp, tm=256):
    rows = x2.shape[0]
    W = RWKV_WIDTH
    tile = lambda n: pl.BlockSpec((tm, n), lambda i: (i, 0))
    full = lambda a: pl.BlockSpec(a.shape, lambda i: (0,) * a.ndim)
    return pl.pallas_call(
        _merge_out_body,
        grid=(rows // tm,),
        in_specs=[tile(D_MODEL), tile(W), tile(W), tile(W), tile(W), tile(ATTN_WIDTH),
                  tile(G_COLS), tile(PLE_DIM), full(lnw), full(lnb), full(pa), full(pb), full(wo), full(pg),
                  full(gw), full(pp)],
        out_specs=tile(D_MODEL),
        out_shape=jax.ShapeDtypeStruct((rows, D_MODEL), F32),
        compiler_params=_cparams(("parallel",)),
        name="merge_out",
    )(x2, yf, yb, bonus, gz, ob, ug, p2, lnw, lnb, pa, pb, wo, pg, gw, pp)


def _block_diag2(m):
    z = jnp.zeros_like(m[0])
    return jnp.concatenate([jnp.concatenate([m[0], z], axis=1), jnp.concatenate([z, m[1]], axis=1)], axis=0)


def _presplit(w):
    hi = w.astype(BF16)
    return jnp.stack([hi, (w - hi.astype(F32)).astype(BF16)])


def kernel(x, p, norm_g, w_in, shift_mu, decay_w0, decay_up, iclr_a0, iclr_up, vres_down, vres_up, vres_v0, k_k, k_a, r_k, ln_x_w, ln_x_b, q_norm_g, k_norm_g, sink, proj_a, proj_b, w_out, ple_norm_g, ple_gate_w, ple_proj):
    bsz, T, _ = x.shape
    depth = w_in.shape[0]
    rows = bsz * T
    W = RWKV_WIDTH
    slopes = jnp.asarray(2.0 ** (-8.0 * jnp.arange(1, ATTN_Q_HEADS + 1, dtype=F32) / ATTN_Q_HEADS), F32)
    v_first = None
    for i in range(depth):
        wa = w_in[i][:, :A_COLS].astype(BF16)
        wbg = w_in[i][:, A_COLS:]
        if i > 0:
            wbg = jnp.concatenate([wbg, vres_down[i - 1], jnp.zeros((D_MODEL, LANES - VRES_RANK), F32)], axis=1)
        params = [shift_mu[i].reshape(1, A_COLS), decay_w0[i].reshape(1, 2 * W), _presplit(_block_diag2(decay_up[i])),
                  iclr_a0[i].reshape(1, 2 * W), _presplit(_block_diag2(iclr_up[i])), k_k[i].reshape(1, W),
                  k_a[i].reshape(1, W), r_k[i].reshape(1, W)]
        vres = None
        if i > 0:
            vu_pad = jnp.concatenate([vres_up[i - 1], jnp.zeros((LANES - VRES_RANK, W), F32)], axis=0)
            vres = (_presplit(vu_pad), vres_v0[i - 1].reshape(1, W), v_first)
        ub, ug, r, v, an, lw, kd, bb, bonus, gz = _proj_prep(x, norm_g[i].reshape(1, D_MODEL), wa, wbg.astype(BF16),
                                                             params, vres)
        if i == 0:
            v_first = v
        yf, yb, ob = _wkv_attn(r, v, an, lw, kd, bb, ub, jnp.tile(q_norm_g[i], ATTN_Q_HEADS).reshape(1, ATTN_WIDTH),
                               jnp.tile(k_norm_g[i], ATTN_KV_HEADS).reshape(1, KV_WIDTH), slopes, sink[i])
        x2 = _merge_out(x.reshape(rows, D_MODEL), yf.reshape(rows, W), yb.reshape(rows, W), bonus.reshape(rows, W),
                        gz.reshape(rows, W), ob.reshape(rows, ATTN_WIDTH), ug.reshape(rows, G_COLS),
                        p[i].reshape(rows, PLE_DIM), ln_x_w[i].reshape(1, W), ln_x_b[i].reshape(1, W),
                        proj_a[i].astype(BF16), proj_b[i].astype(BF16), w_out[i].astype(BF16),
                        ple_norm_g[i].reshape(1, D_MODEL), ple_gate_w[i].astype(BF16), ple_proj[i].astype(BF16))
        x = x2.reshape(bsz, T, D_MODEL)
    return x
```

```python
import functools

import jax
import jax.numpy as jnp
from jax import lax
from jax.experimental import pallas as pl
from jax.experimental.pallas import tpu as pltpu

F32 = jnp.float32
BF16 = jnp.bfloat16

D_MODEL = 1024
PLE_DIM = 256
RWKV_WIDTH = 512
HEAD = 64
RWKV_HEADS = RWKV_WIDTH // HEAD
LOW_RANK = 64
VRES_RANK = 32
ATTN_WIDTH = 512
ATTN_Q_HEADS = ATTN_WIDTH // HEAD
ATTN_KV_HEADS = 2
ATTN_GROUP = ATTN_Q_HEADS // ATTN_KV_HEADS
KV_WIDTH = ATTN_KV_HEADS * HEAD
WINDOW = 128
BLOCK = 128
RMS_EPS = 1e-6
GN_EPS = 64e-5
NEG_INF = -1e30
A_COLS = 4 * RWKV_WIDTH + 4 * LOW_RANK
B_COLS = 2 * ATTN_WIDTH + 2 * KV_WIDTH
G_COLS = 2 * D_MODEL
LANES = 128
SUBLANES = 8
HALO = 16
CHUNK = 64
INV_BASE = 2
WKV_GROUP = 2
PROJ_TILE = 256
MERGE_TILE = 512
VMEM_LIMIT = 48 * 1024 * 1024

NN = ((1,), (0,))
NT = ((1,), (1,))
TN = ((0,), (0,))


def _dot(a, b, dims=NN):
    return lax.dot_general(a, b, (dims, ((), ())), preferred_element_type=F32)


def _split2(x):
    hi = x.astype(BF16)
    lo = (x - hi.astype(F32)).astype(BF16)
    return hi, lo


def _dot_exact_lhs(a_bf16, b, dims=NN):
    b1 = b.astype(BF16)
    r1 = b - b1.astype(F32)
    b2 = r1.astype(BF16)
    b3 = (r1 - b2.astype(F32)).astype(BF16)
    return _dot(a_bf16, b1, dims) + (_dot(a_bf16, b2, dims) + _dot(a_bf16, b3, dims))


def _dot_split2_rhs(a, b_bf16):
    hi, lo = _split2(a)
    return _dot(hi, b_bf16) + _dot(lo, b_bf16)


def _dot3_presplit(a, w_ref):
    ah, al = _split2(a)
    return _dot(ah, w_ref[0]) + (_dot(ah, w_ref[1]) + _dot(al, w_ref[0]))


def _zero_after(x):
    bits = lax.bitcast_convert_type(x, jnp.int32)
    half = jnp.int32(16)
    return lax.shift_right_logical(lax.shift_right_logical(bits, half), half).astype(F32)


def _sigmoid(x):
    return 1.0 / (1.0 + jnp.exp(-x))


def _head_sum_matrix(width):
    r = lax.broadcasted_iota(jnp.int32, (width, width), 0) // HEAD
    c = lax.broadcasted_iota(jnp.int32, (width, width), 1) // HEAD
    return (r == c).astype(BF16)


def _cparams(sem):
    return pltpu.CompilerParams(dimension_semantics=sem, vmem_limit_bytes=VMEM_LIMIT)


def _proj_prep_body(*refs, has_vres):
    if has_vres:
        (x_ref, xp_ref, xn_ref, g_ref, w_ref, mu_ref, w0_ref, dup_ref, a0_ref, iup_ref, kk_ref, ka_ref,
         rk_ref, vd_ref, vu_ref, v0_ref, vf_ref,
         ub_o, ug_o, r_o, v_o, an_o, lw_o, kd_o, bb_o, bonus_o, gz_o, hb_ref, ua_ref) = refs
    else:
        (x_ref, xp_ref, xn_ref, g_ref, w_ref, mu_ref, w0_ref, dup_ref, a0_ref, iup_ref, kk_ref, ka_ref,
         rk_ref,
         ub_o, ug_o, r_o, v_o, an_o, lw_o, kd_o, bb_o, bonus_o, gz_o, hb_ref, ua_ref) = refs
    tm = x_ref.shape[0]
    i = pl.program_id(1)
    last = pl.num_programs(1) - 1
    W = RWKV_WIDTH

    def normed(ref):
        x = ref[...]
        return (x * lax.rsqrt(jnp.mean(x * x, axis=-1, keepdims=True) + RMS_EPS) * g_ref[...]).astype(BF16)

    hb_ref[0:HALO] = normed(xp_ref)
    hb_ref[HALO:HALO + tm] = normed(x_ref)
    hb_ref[HALO + tm:] = normed(xn_ref)
    ua_ref[...] = _dot(hb_ref[...], w_ref[:, :A_COLS])
    hm = hb_ref[HALO:HALO + tm]

    def project_b(lo, hi):
        ub_o[:, lo:hi] = _dot(hm, w_ref[:, A_COLS + lo:A_COLS + hi])

    def project_g(lo, hi):
        ug_o[:, lo:hi] = _dot(hm, w_ref[:, A_COLS + B_COLS + lo:A_COLS + B_COLS + hi]).astype(ug_o.dtype)

    row = lax.broadcasted_iota(jnp.int32, (tm, 1), 0)
    no_prev = (row == 0) & (i == 0)
    no_next = (row == tm - 1) & (i == last)
    hsum = _head_sum_matrix(W)

    def shifted(lo, hi):
        u = ua_ref[HALO:HALO + tm, lo:hi]
        prev = jnp.where(no_prev, 0.0, ua_ref[HALO - 1:HALO - 1 + tm, lo:hi])
        nxt = jnp.where(no_next, 0.0, ua_ref[HALO + 1:HALO + 1 + tm, lo:hi])
        return u + mu_ref[:, lo:hi] * (0.5 * (prev + nxt) - u)

    r = shifted(0, W)
    project_b(0, ATTN_WIDTH)
    k = shifted(W, 2 * W)
    project_b(ATTN_WIDTH, B_COLS)
    v = shifted(2 * W, 3 * W)
    project_g(0, D_MODEL // 2)
    z = shifted(3 * W, 4 * W)
    project_g(D_MODEL // 2, D_MODEL)
    low = shifted(4 * W, 4 * W + 4 * LOW_RANK)
    w_raw = w0_ref[...] + _dot3_presplit(jnp.tanh(low[:, :2 * LOW_RANK]), dup_ref)
    project_g(D_MODEL, D_MODEL + D_MODEL // 2)
    nw = -w_raw
    softplus = jnp.maximum(nw, 0.0) + jnp.log(1.0 + jnp.exp(-jnp.abs(nw)))
    lw = -jnp.exp(-softplus - 0.5)
    a = _sigmoid(a0_ref[...] + _dot3_presplit(low[:, 2 * LOW_RANK:], iup_ref))
    if has_vres:
        hd = _dot(hm, vd_ref[...])
        mix = _sigmoid(v0_ref[...] + _dot3_presplit(hd, vu_ref))
        v = v + (vf_ref[...] - v) * mix
    project_g(D_MODEL + D_MODEL // 2, G_COLS)
    kk = k * kk_ref[...]
    ss = _dot_split2_rhs(kk * kk, hsum)
    kk = kk / jnp.maximum(jnp.sqrt(ss), 1e-12)
    ka = ka_ref[...]
    ksum = jnp.zeros_like(k)
    for d in range(2):
        a_d = a[:, d * W:(d + 1) * W]
        kd = k * (1.0 + (a_d - 1.0) * ka)
        ksum = ksum + kd
        lw_o[d] = lw[:, d * W:(d + 1) * W]
        kd_o[d] = kd
        bb_o[d] = kk * a_d
    r_o[...] = r
    v_o[...] = v
    an_o[...] = -kk
    bonus_o[...] = _dot_split2_rhs(r * ksum * rk_ref[...], hsum) * v
    gz_o[...] = z * _sigmoid(z)


def _proj_prep(x, g, w, params, vres, tm=PROJ_TILE):
    bsz, T, _ = x.shape
    W = RWKV_WIDTH
    nt = T // tm
    hb = tm // HALO
    nh = T // HALO
    const = lambda a: pl.BlockSpec(a.shape, lambda b, i: (0,) * a.ndim, pipeline_mode=pl.Buffered(1))
    tile = lambda n: pl.BlockSpec((None, tm, n), lambda b, i: (b, i, 0))
    in_specs = [
        tile(D_MODEL),
        pl.BlockSpec((None, HALO, D_MODEL), lambda b, i: (b, jnp.maximum(i * hb - 1, 0), 0)),
        pl.BlockSpec((None, HALO, D_MODEL), lambda b, i: (b, jnp.minimum((i + 1) * hb, nh - 1), 0)),
        const(g), const(w),
    ] + [const(a) for a in params]
    args = [x, x, x, g, w] + list(params)
    if vres is not None:
        vd_pad, vu_split, v0, v_first = vres
        in_specs += [const(vd_pad), const(vu_split), const(v0), tile(W)]
        args += [vd_pad, vu_split, v0, v_first]
    dir_tile = pl.BlockSpec((2, None, tm, W), lambda b, i: (0, b, i, 0))
    out_specs = [tile(B_COLS), tile(G_COLS), tile(W), tile(W), tile(W), dir_tile, dir_tile, dir_tile, tile(W), tile(W)]
    one = jax.ShapeDtypeStruct((bsz, T, W), F32)
    two = jax.ShapeDtypeStruct((2, bsz, T, W), F32)
    out_shape = [jax.ShapeDtypeStruct((bsz, T, B_COLS), F32), jax.ShapeDtypeStruct((bsz, T, G_COLS), BF16),
                 one, one, one, two, two, two, one, one]
    return pl.pallas_call(
        functools.partial(_proj_prep_body, has_vres=vres is not None),
        grid=(bsz, nt),
        in_specs=in_specs,
        out_specs=out_specs,
        out_shape=out_shape,
        scratch_shapes=[pltpu.VMEM((tm + 2 * HALO, D_MODEL), BF16), pltpu.VMEM((tm + 2 * HALO, A_COLS), F32)],
        compiler_params=_cparams(("parallel", "parallel")),
        name="proj_prep",
    )(*args)


def _wkv_step(side_work, rf_ref, vf_ref, af_ref, rb_ref, vb_ref, ab_ref, lwf_ref, kdf_ref, bbf_ref,
              lwb_ref, kdb_ref, bbb_ref, yf_ref, yb_ref, s_ref):
    bsz, C, _ = rf_ref.shape
    G = WKV_GROUP
    GW = G * HEAD
    NG = RWKV_HEADS // G
    assert C == HEAD

    row = lax.broadcasted_iota(jnp.int32, (C, GW), 0)
    lane = lax.broadcasted_iota(jnp.int32, (C, GW), 1)
    col = lane % HEAD
    eye = (row == col).astype(F32)
    same_blk = []
    size = INV_BASE
    while size < C:
        same_blk.append(((row // size) == (col // size)).astype(F32))
        size *= 2
    head_mask = [((lane // HEAD) == g).astype(BF16) for g in range(G)]
    srow = lax.broadcasted_iota(jnp.int32, (GW, GW), 0) // HEAD
    slane = lax.broadcasted_iota(jnp.int32, (GW, GW), 1) // HEAD
    state_mask = (srow == slane).astype(F32)
    trow = lax.broadcasted_iota(jnp.int32, (C, C), 0)
    tcol = lax.broadcasted_iota(jnp.int32, (C, C), 1)
    incl_d = [(col <= row).astype(F32), (col >= row).astype(F32)]
    strict_d = [(col < row).astype(F32), (col > row).astype(F32)]
    tri_d = [(tcol <= trow).astype(BF16), (tcol >= trow).astype(BF16)]

    b16 = lambda x: x.astype(BF16)
    bd = lambda xb: jnp.concatenate([xb * m for m in head_mask], axis=0)
    cat = lambda a, b, axis: jnp.concatenate([a, b], axis=axis)

    dir_refs = [(rf_ref, vf_ref, af_ref, lwf_ref, kdf_ref, bbf_ref, yf_ref),
                (rb_ref, vb_ref, ab_ref, lwb_ref, kdb_ref, bbb_ref, yb_ref)]
    chains = []
    at_g, rt_g, v_g, bt_g, kt_g, bh_g, kh_g, pt_g = [], [], [], [], [], [], [], []
    for d, (r_ref, v_ref, an_ref, lw_ref, kd_ref, bb_ref, _) in enumerate(dir_refs):
        for b in range(bsz):
            lw = lw_ref[b]
            cs = _dot_exact_lhs(tri_d[d], lw)
            tot = jnp.sum(lw, axis=0, keepdims=True)
            e_inv = jnp.exp(-cs)
            p_tot = jnp.exp(tot)
            rt = r_ref[b] * jnp.exp(cs)
            at = an_ref[b] * jnp.exp(cs - lw)
            bt = bb_ref[b] * e_inv
            kt = kd_ref[b] * e_inv
            bh = bt * p_tot
            kh = kt * p_tot
            vv = v_ref[b]
            for g in range(NG):
                sl = slice(g * GW, (g + 1) * GW)
                chains.append((d, b, g))
                for lst, val in ((at_g, at), (rt_g, rt), (v_g, vv), (bt_g, bt), (kt_g, kt), (bh_g, bh), (kh_g, kh),
                                 (pt_g, p_tot)):
                    lst.append(val[:, sl])
    strict = [strict_d[d] for d, _, _ in chains]
    incl = [incl_d[d] for d, _, _ in chains]

    at_b = [b16(a) for a in at_g]
    v_b = [b16(v) for v in v_g]
    ar_b = [cat(a, b16(r), 0) for a, r in zip(at_b, rt_g)]
    sb = [_dot(ar, bd(b16(x)), NT) for ar, x in zip(ar_b, bt_g)]
    sk = [_dot(ar, bd(b16(x)), NT) for ar, x in zip(ar_b, kt_g)]
    m_ab = [x[:C] * m for x, m in zip(sb, strict)]
    n_rb = [x[C:] * m for x, m in zip(sb, incl)]
    m_ak = [x[:C] * m for x, m in zip(sk, strict)]
    n_rk = [x[C:] * m for x, m in zip(sk, incl)]
    v_bd = [bd(v) for v in v_b]
    mv = [_dot(b16(m), vb) for m, vb in zip(m_ak, v_bd)]

    x = [eye + m * same_blk[0] for m in m_ab]
    for lvl in range(len(same_blk)):
        inner = same_blk[lvl]
        outer = same_blk[lvl + 1] if lvl + 1 < len(same_blk) else 1.0
        sel = outer - inner
        if lvl in side_work:
            done = side_work[lvl](_zero_after(x[0][:SUBLANES]))
        x_b = [b16(xi) for xi in x]
        t = [_dot(xb, bd(b16(m * sel))) for xb, m in zip(x_b, m_ab)]
        x = [xi + _dot(b16(ti), bd(xb)) for xi, ti, xb in zip(x, t, x_b)]
        if lvl in side_work:
            hold = jnp.tile(done, (C // SUBLANES, 1))
            x = [xi + hold for xi in x]

    w_ab = [_dot(b16(xi), cat(bd(a), bd(b16(m)), 1)) for xi, a, m in zip(x, at_b, mv)]
    w_a_b = [b16(w[:, :GW]) for w in w_ab]
    w_b_b = [b16(w[:, GW:]) for w in w_ab]
    n_rb_b = [b16(n) for n in n_rb]
    y_q = [r + _dot(n, bd(w)) for r, n, w in zip(rt_g, n_rb_b, w_a_b)]
    y_c = [_dot(cat(n, b16(nk), 1), cat(bd(w), vb, 0)) for n, nk, w, vb in zip(n_rb_b, n_rk, w_b_b, v_bd)]
    bh_b = [b16(b) for b in bh_g]
    phi = [_dot(w, b, TN) * state_mask for w, b in zip(w_a_b, bh_b)]
    psi = [_dot(cat(w, v, 0), cat(b, b16(k), 0), TN) * state_mask
           for w, v, b, k in zip(w_b_b, v_b, bh_b, kh_g)]
    for i, (d, b, g) in enumerate(chains):
        s0 = s_ref[d, b, g]
        s0_b = b16(s0)
        dir_refs[d][-1][b, :, g * GW:(g + 1) * GW] = _dot(b16(y_q[i]), s0_b, NT) + y_c[i]
        s_ref[d, b, g] = s0 * pt_g[i] + _dot(s0_b, b16(phi[i])) + psi[i]


def _attn_bias_init(slope_ref, bias_ref):
    rows2 = 2 * BLOCK
    qi = lax.broadcasted_iota(jnp.int32, (rows2, 3 * BLOCK), 0) % BLOCK
    kpos = lax.broadcasted_iota(jnp.int32, (rows2, 3 * BLOCK), 1) - BLOCK
    upper = lax.broadcasted_iota(jnp.int32, (rows2, 3 * BLOCK), 0) >= BLOCK
    dist = jnp.abs(qi - kpos)
    for p in range(ATTN_Q_HEADS // 2):
        slope = jnp.where(upper, slope_ref[2 * p + 1], slope_ref[2 * p])
        bias_ref[p] = jnp.where(dist <= WINDOW, -slope * dist.astype(F32), NEG_INF)


def _attn_step(n, nb, q_ref, kp_ref, kc_ref, kn_ref, qg_ref, kg_ref, slope_ref, sink_ref, o_ref, bias_ref):
    pairs = ATTN_Q_HEADS // 2
    rows2 = 2 * BLOCK
    hq = _head_sum_matrix(ATTN_WIDTH)
    hk = _head_sum_matrix(KV_WIDTH)
    lane = lax.broadcasted_iota(jnp.int32, (BLOCK, LANES), 1)
    low = lane < HEAD
    srow = lax.broadcasted_iota(jnp.int32, (rows2, 1), 0)

    q = q_ref[:, :ATTN_WIDTH]
    q = q * lax.rsqrt(_dot_split2_rhs(q * q, hq) * (1.0 / HEAD) + RMS_EPS) * (qg_ref[...] * (HEAD ** -0.5))

    def dup(x, g):
        rolled = pltpu.roll(x, HEAD, axis=1)
        return (jnp.where(low, x, rolled) if g == 0 else jnp.where(low, rolled, x)).astype(BF16)

    k_dup, v_dup = [[], []], [[], []]
    for ref in (kp_ref, kc_ref, kn_ref):
        kx = ref[:, :KV_WIDTH]
        kx = kx * lax.rsqrt(_dot_split2_rhs(kx * kx, hk) * (1.0 / HEAD) + RMS_EPS) * kg_ref[...]
        vx = ref[:, KV_WIDTH:]
        for g in range(ATTN_KV_HEADS):
            k_dup[g].append(dup(kx, g))
            v_dup[g].append(dup(vx, g))
    k_cat = [jnp.concatenate(k_dup[g], axis=0) for g in range(ATTN_KV_HEADS)]
    v_cat = [jnp.concatenate(v_dup[g], axis=0) for g in range(ATTN_KV_HEADS)]
    edge_prev = jnp.where(n > 0, 0.0, NEG_INF)
    edge_next = jnp.where(n < nb - 1, 0.0, NEG_INF)

    def head_pair(p, start):
        g = (2 * p) // ATTN_GROUP
        qp = q[:, p * LANES:(p + 1) * LANES] + jnp.tile(start, (BLOCK // SUBLANES, 1))
        qs = jnp.concatenate([jnp.where(low, qp, 0.0), jnp.where(low, 0.0, qp)], axis=0).astype(BF16)
        s = _dot(qs, k_cat[g], NT) + bias_ref[p]
        s0 = s[:, :BLOCK] + edge_prev
        s1 = s[:, BLOCK:2 * BLOCK]
        s2 = s[:, 2 * BLOCK:] + edge_next
        sink = jnp.where(srow >= BLOCK, sink_ref[2 * p + 1], sink_ref[2 * p])
        m = jnp.maximum(jnp.max(jnp.maximum(jnp.maximum(s0, s1), s2), axis=-1, keepdims=True), sink)
        e0, e1, e2 = jnp.exp(s0 - m), jnp.exp(s1 - m), jnp.exp(s2 - m)
        den = jnp.sum(e0 + e1 + e2, axis=-1, keepdims=True) + jnp.exp(sink - m)
        e = jnp.concatenate([e0, e1, e2], axis=1).astype(BF16)
        o2 = _dot(e, v_cat[g]) * (1.0 / den)
        o = jnp.where(low, o2[:BLOCK], o2[BLOCK:])
        z_lo = ATTN_WIDTH + 2 * KV_WIDTH + p * LANES
        zp = q_ref[:, z_lo:z_lo + LANES]
        og = o * (zp * _sigmoid(zp))
        o_ref[:, p * LANES:(p + 1) * LANES] = og.astype(o_ref.dtype)
        return _zero_after(og[:SUBLANES])

    return head_pair


N_WKV_IN = 12
N_ATTN_IN = 8
ATTN_JOIN_LEVELS = (1, 2, 3, 4)


def _wkv_attn_body(*refs, nb):
    wkv_in = refs[:N_WKV_IN]
    attn_in = refs[N_WKV_IN:N_WKV_IN + N_ATTN_IN]
    yf_ref, yb_ref, o_ref, s_ref, bias_ref = refs[N_WKV_IN + N_ATTN_IN:]
    c = pl.program_id(0)

    @pl.when(c == 0)
    def _():
        s_ref[...] = jnp.zeros_like(s_ref)
        _attn_bias_init(attn_in[6], bias_ref)

    head_pair = _attn_step(c % nb, nb, *attn_in, o_ref, bias_ref)
    side_work = {lvl: functools.partial(head_pair, p) for p, lvl in enumerate(ATTN_JOIN_LEVELS)}
    _wkv_step(side_work, *wkv_in, yf_ref, yb_ref, s_ref)


def _wkv_attn(r, v, an, lw, kd, bb, ub, q_g, k_g, slopes, sink):
    bsz, T, W = r.shape
    C = CHUNK
    nc = T // C
    nb = T // BLOCK
    assert nc == bsz * nb
    gw = WKV_GROUP * HEAD
    fwd = pl.BlockSpec((bsz, C, W), lambda c: (0, c, 0))
    bwd = pl.BlockSpec((bsz, C, W), lambda c: (0, nc - 1 - c, 0))
    fwd_d = pl.BlockSpec((None, bsz, C, W), lambda c: (0, 0, c, 0))
    bwd_d = pl.BlockSpec((None, bsz, C, W), lambda c: (1, 0, nc - 1 - c, 0))
    kv_blk = ATTN_WIDTH // (2 * KV_WIDTH)
    kv = lambda f: pl.BlockSpec((None, BLOCK, 2 * KV_WIDTH), f)
    smem = pl.BlockSpec(memory_space=pltpu.SMEM)
    attn_specs = [
        pl.BlockSpec((None, BLOCK, B_COLS), lambda c: (c // nb, c % nb, 0)),
        kv(lambda c: (c // nb, jnp.maximum(c % nb - 1, 0), kv_blk)),
        kv(lambda c: (c // nb, c % nb, kv_blk)),
        kv(lambda c: (c // nb, jnp.minimum(c % nb + 1, nb - 1), kv_blk)),
        pl.BlockSpec((1, ATTN_WIDTH), lambda c: (0, 0)),
        pl.BlockSpec((1, KV_WIDTH), lambda c: (0, 0)),
        smem, smem,
    ]
    out = jax.ShapeDtypeStruct((bsz, T, W), F32)
    return pl.pallas_call(
        functools.partial(_wkv_attn_body, nb=nb),
        grid=(nc,),
        in_specs=[fwd, fwd, fwd, bwd, bwd, bwd, fwd_d, fwd_d, fwd_d, bwd_d, bwd_d, bwd_d] + attn_specs,
        out_specs=[fwd, bwd, pl.BlockSpec((None, BLOCK, ATTN_WIDTH), lambda c: (c // nb, c % nb, 0))],
        out_shape=[out, out, jax.ShapeDtypeStruct((bsz, T, ATTN_WIDTH), BF16)],
        scratch_shapes=[pltpu.VMEM((2, bsz, RWKV_HEADS // WKV_GROUP, gw, gw), F32),
                        pltpu.VMEM((ATTN_Q_HEADS // 2, 2 * BLOCK, 3 * BLOCK), F32)],
        compiler_params=_cparams(("arbitrary",)),
        name="wkv_attn",
    )(r, v, an, r, v, an, lw, kd, bb, lw, kd, bb, ub, ub, ub, ub, q_g, k_g, slopes, sink)


def _merge_out_body(x_ref, yf_ref, yb_ref, bonus_ref, gz_ref, ob_ref, ug_ref, p_ref, lnw_ref, lnb_ref,
                    pa_ref, pb_ref, wo_ref, pg_ref, gw_ref, pp_ref, o_ref):
    hsum = _head_sum_matrix(RWKV_WIDTH)
    y = yf_ref[...] + yb_ref[...]
    mean = _dot_split2_rhs(y, hsum) * (1.0 / HEAD)
    yc = y - mean
    var = _dot_split2_rhs(yc * yc, hsum) * (1.0 / HEAD)
    yn = yc * lax.rsqrt(var + GN_EPS) * lnw_ref[...] + lnb_ref[...]
    o_a = (yn + bonus_ref[...]) * gz_ref[...]
    y_a = _dot(o_a.astype(BF16), pa_ref[...])
    y_b = _dot(ob_ref[...], pb_ref[...])
    merged = (_sigmoid(ug_ref[:, :D_MODEL].astype(F32)) * y_a
              + _sigmoid(ug_ref[:, D_MODEL:].astype(F32)) * y_b)
    x1 = x_ref[...] + _dot(merged.astype(BF16), wo_ref[...])
    ple = _dot(p_ref[...].astype(BF16), pp_ref[...])
    hn = x1 * lax.rsqrt(jnp.mean(x1 * x1, axis=-1, keepdims=True) + RMS_EPS) * pg_ref[...]
    o_ref[...] = x1 + _sigmoid(_dot(hn.astype(BF16), gw_ref[...])) * ple


def _merge_out(x2, yf, yb, bonus, gz, ob, ug, p2, lnw, lnb, pa, pb, wo, pg, gw, pp, tm=MERGE_TILE):
    rows = x2.shape[0]
    W = RWKV_WIDTH
    tile = lambda n: pl.BlockSpec((tm, n), lambda i: (i, 0))
    full = lambda a: pl.BlockSpec(a.shape, lambda i: (0,) * a.ndim)
    return pl.pallas_call(
        _merge_out_body,
        grid=(rows // tm,),
        in_specs=[tile(D_MODEL), tile(W), tile(W), tile(W), tile(W), tile(ATTN_WIDTH),
                  tile(G_COLS), tile(PLE_DIM), full(lnw), full(lnb), full(pa), full(pb), full(wo), full(pg),
                  full(gw), full(pp)],
        out_specs=tile(D_MODEL),
        out_shape=jax.ShapeDtypeStruct((rows, D_MODEL), F32),
        compiler_params=_cparams(("parallel",)),
        name="merge_out",
    )(x2, yf, yb, bonus, gz, ob, ug, p2, lnw, lnb, pa, pb, wo, pg, gw, pp)


def _block_diag2(m):
    z = jnp.zeros_like(m[0])
    return jnp.concatenate([jnp.concatenate([m[0], z], axis=1), jnp.concatenate([z, m[1]], axis=1)], axis=0)


def _presplit(w):
    hi = w.astype(BF16)
    return jnp.stack([hi, (w - hi.astype(F32)).astype(BF16)])


def kernel(x, p, norm_g, w_in, shift_mu, decay_w0, decay_up, iclr_a0, iclr_up, vres_down, vres_up, vres_v0, k_k, k_a, r_k, ln_x_w, ln_x_b, q_norm_g, k_norm_g, sink, proj_a, proj_b, w_out, ple_norm_g, ple_gate_w, ple_proj):
    bsz, T, _ = x.shape
    depth = w_in.shape[0]
    rows = bsz * T
    W = RWKV_WIDTH
    slopes = jnp.asarray(2.0 ** (-8.0 * jnp.arange(1, ATTN_Q_HEADS + 1, dtype=F32) / ATTN_Q_HEADS), F32)
    v_first = None
    for i in range(depth):
        params = [shift_mu[i].reshape(1, A_COLS), decay_w0[i].reshape(1, 2 * W), _presplit(_block_diag2(decay_up[i])),
                  iclr_a0[i].reshape(1, 2 * W), _presplit(_block_diag2(iclr_up[i])), k_k[i].reshape(1, W),
                  k_a[i].reshape(1, W), r_k[i].reshape(1, W)]
        vres = None
        if i > 0:
            vu_pad = jnp.concatenate([vres_up[i - 1], jnp.zeros((LANES - VRES_RANK, W), F32)], axis=0)
            vd_pad = jnp.concatenate([vres_down[i - 1], jnp.zeros((D_MODEL, LANES - VRES_RANK), F32)], axis=1)
            vres = (vd_pad.astype(BF16), _presplit(vu_pad), vres_v0[i - 1].reshape(1, W), v_first)
        ub, ug, r, v, an, lw, kd, bb, bonus, gz = _proj_prep(x, norm_g[i].reshape(1, D_MODEL), w_in[i].astype(BF16),
                                                             params, vres)
        if i == 0:
            v_first = v
        yf, yb, ob = _wkv_attn(r, v, an, lw, kd, bb, ub, jnp.tile(q_norm_g[i], ATTN_Q_HEADS).reshape(1, ATTN_WIDTH),
                               jnp.tile(k_norm_g[i], ATTN_KV_HEADS).reshape(1, KV_WIDTH), slopes, sink[i])
        x2 = _merge_out(x.reshape(rows, D_MODEL), yf.reshape(rows, W), yb.reshape(rows, W), bonus.reshape(rows, W),
                        gz.reshape(rows, W), ob.reshape(rows, ATTN_WIDTH), ug.reshape(rows, G_COLS),
                        p[i].reshape(rows, PLE_DIM), ln_x_w[i].reshape(1, W), ln_x_b[i].reshape(1, W),
                        proj_a[i].astype(BF16), proj_b[i].astype(BF16), w_out[i].astype(BF16),
                        ple_norm_g[i].reshape(1, D_MODEL), ple_gate_w[i].astype(BF16), ple_proj[i].astype(BF16))
        x = x2.reshape(bsz, T, D_MODEL)
    return x
```

```python
import functools

import jax
import jax.numpy as jnp
from jax import lax
from jax.experimental import pallas as pl
from jax.experimental.pallas import tpu as pltpu

F32 = jnp.float32
BF16 = jnp.bfloat16

D_MODEL = 1024
PLE_DIM = 256
RWKV_WIDTH = 512
HEAD = 64
RWKV_HEADS = RWKV_WIDTH // HEAD
LOW_RANK = 64
VRES_RANK = 32
ATTN_WIDTH = 512
ATTN_Q_HEADS = ATTN_WIDTH // HEAD
ATTN_KV_HEADS = 2
ATTN_GROUP = ATTN_Q_HEADS // ATTN_KV_HEADS
KV_WIDTH = ATTN_KV_HEADS * HEAD
WINDOW = 128
BLOCK = 128
RMS_EPS = 1e-6
GN_EPS = 64e-5
NEG_INF = -1e30
A_COLS = 4 * RWKV_WIDTH + 4 * LOW_RANK
B_COLS = 2 * ATTN_WIDTH + 2 * KV_WIDTH
G_COLS = 2 * D_MODEL
LANES = 128
SUBLANES = 8
HALO = 16
CHUNK = 64
INV_BASE = 2
WKV_GROUP = 2
PROJ_TILE = 256
MERGE_TILE = 512
VMEM_LIMIT = 48 * 1024 * 1024

NN = ((1,), (0,))
NT = ((1,), (1,))
TN = ((0,), (0,))


def _dot(a, b, dims=NN):
    return lax.dot_general(a, b, (dims, ((), ())), preferred_element_type=F32)


def _split2(x):
    hi = x.astype(BF16)
    lo = (x - hi.astype(F32)).astype(BF16)
    return hi, lo


def _dot_exact_lhs(a_bf16, b, dims=NN):
    b1 = b.astype(BF16)
    r1 = b - b1.astype(F32)
    b2 = r1.astype(BF16)
    b3 = (r1 - b2.astype(F32)).astype(BF16)
    return _dot(a_bf16, b1, dims) + (_dot(a_bf16, b2, dims) + _dot(a_bf16, b3, dims))


def _dot_split2_rhs(a, b_bf16):
    hi, lo = _split2(a)
    return _dot(hi, b_bf16) + _dot(lo, b_bf16)


def _dot3_presplit(a, w_ref):
    ah, al = _split2(a)
    return _dot(ah, w_ref[0]) + (_dot(ah, w_ref[1]) + _dot(al, w_ref[0]))


def _zero_after(x):
    bits = lax.bitcast_convert_type(x, jnp.int32)
    half = jnp.int32(16)
    return lax.shift_right_logical(lax.shift_right_logical(bits, half), half).astype(F32)


def _sigmoid(x):
    return 1.0 / (1.0 + jnp.exp(-x))


def _head_sum_matrix(width):
    r = lax.broadcasted_iota(jnp.int32, (width, width), 0) // HEAD
    c = lax.broadcasted_iota(jnp.int32, (width, width), 1) // HEAD
    return (r == c).astype(BF16)


def _cparams(sem):
    return pltpu.CompilerParams(dimension_semantics=sem, vmem_limit_bytes=VMEM_LIMIT)


def _proj_prep_body(*refs, has_vres):
    if has_vres:
        (x_ref, xp_ref, xn_ref, g_ref, w_ref, mu_ref, w0_ref, dup_ref, a0_ref, iup_ref, kk_ref, ka_ref,
         rk_ref, vd_ref, vu_ref, v0_ref, vf_ref,
         ub_o, ug_o, r_o, v_o, an_o, lw_o, kd_o, bb_o, bonus_o, gz_o, hb_ref, ua_ref) = refs
    else:
        (x_ref, xp_ref, xn_ref, g_ref, w_ref, mu_ref, w0_ref, dup_ref, a0_ref, iup_ref, kk_ref, ka_ref,
         rk_ref,
         ub_o, ug_o, r_o, v_o, an_o, lw_o, kd_o, bb_o, bonus_o, gz_o, hb_ref, ua_ref) = refs
    tm = x_ref.shape[0]
    i = pl.program_id(1)
    last = pl.num_programs(1) - 1
    W = RWKV_WIDTH

    def normed(ref):
        x = ref[...]
        return (x * lax.rsqrt(jnp.mean(x * x, axis=-1, keepdims=True) + RMS_EPS) * g_ref[...]).astype(BF16)

    hb_ref[0:HALO] = normed(xp_ref)
    hb_ref[HALO:HALO + tm] = normed(x_ref)
    hb_ref[HALO + tm:] = normed(xn_ref)
    ua_ref[...] = _dot(hb_ref[...], w_ref[:, :A_COLS])
    hm = hb_ref[HALO:HALO + tm]

    def project_b(lo, hi):
        ub_o[:, lo:hi] = _dot(hm, w_ref[:, A_COLS + lo:A_COLS + hi])

    def project_g(lo, hi):
        ug_o[:, lo:hi] = _dot(hm, w_ref[:, A_COLS + B_COLS + lo:A_COLS + B_COLS + hi]).astype(ug_o.dtype)

    row = lax.broadcasted_iota(jnp.int32, (tm, 1), 0)
    no_prev = (row == 0) & (i == 0)
    no_next = (row == tm - 1) & (i == last)
    hsum = _head_sum_matrix(W)

    def shifted(lo, hi):
        u = ua_ref[HALO:HALO + tm, lo:hi]
        prev = jnp.where(no_prev, 0.0, ua_ref[HALO - 1:HALO - 1 + tm, lo:hi])
        nxt = jnp.where(no_next, 0.0, ua_ref[HALO + 1:HALO + 1 + tm, lo:hi])
        return u + mu_ref[:, lo:hi] * (0.5 * (prev + nxt) - u)

    r = shifted(0, W)
    project_b(0, ATTN_WIDTH)
    k = shifted(W, 2 * W)
    project_b(ATTN_WIDTH, B_COLS)
    v = shifted(2 * W, 3 * W)
    project_g(0, D_MODEL // 2)
    z = shifted(3 * W, 4 * W)
    project_g(D_MODEL // 2, D_MODEL)
    low = shifted(4 * W, 4 * W + 4 * LOW_RANK)
    w_raw = w0_ref[...] + _dot3_presplit(jnp.tanh(low[:, :2 * LOW_RANK]), dup_ref)
    project_g(D_MODEL, D_MODEL + D_MODEL // 2)
    nw = -w_raw
    softplus = jnp.maximum(nw, 0.0) + jnp.log(1.0 + jnp.exp(-jnp.abs(nw)))
    lw = -jnp.exp(-softplus - 0.5)
    a = _sigmoid(a0_ref[...] + _dot3_presplit(low[:, 2 * LOW_RANK:], iup_ref))
    if has_vres:
        hd = _dot(hm, vd_ref[...])
        mix = _sigmoid(v0_ref[...] + _dot3_presplit(hd, vu_ref))
        v = v + (vf_ref[...] - v) * mix
    project_g(D_MODEL + D_MODEL // 2, G_COLS)
    kk = k * kk_ref[...]
    ss = _dot_split2_rhs(kk * kk, hsum)
    kk = kk / jnp.maximum(jnp.sqrt(ss), 1e-12)
    ka = ka_ref[...]
    ksum = jnp.zeros_like(k)
    for d in range(2):
        a_d = a[:, d * W:(d + 1) * W]
        kd = k * (1.0 + (a_d - 1.0) * ka)
        ksum = ksum + kd
        lw_o[d] = lw[:, d * W:(d + 1) * W]
        kd_o[d] = kd
        bb_o[d] = kk * a_d
    r_o[...] = r
    v_o[...] = v
    an_o[...] = -kk
    bonus_o[...] = _dot_split2_rhs(r * ksum * rk_ref[...], hsum) * v
    gz_o[...] = z * _sigmoid(z)


def _proj_prep(x, g, w, params, vres, tm=PROJ_TILE):
    bsz, T, _ = x.shape
    W = RWKV_WIDTH
    nt = T // tm
    hb = tm // HALO
    nh = T // HALO
    const = lambda a: pl.BlockSpec(a.shape, lambda b, i: (0,) * a.ndim, pipeline_mode=pl.Buffered(1))
    tile = lambda n: pl.BlockSpec((None, tm, n), lambda b, i: (b, i, 0))
    in_specs = [
        tile(D_MODEL),
        pl.BlockSpec((None, HALO, D_MODEL), lambda b, i: (b, jnp.maximum(i * hb - 1, 0), 0)),
        pl.BlockSpec((None, HALO, D_MODEL), lambda b, i: (b, jnp.minimum((i + 1) * hb, nh - 1), 0)),
        const(g), const(w),
    ] + [const(a) for a in params]
    args = [x, x, x, g, w] + list(params)
    if vres is not None:
        vd_pad, vu_split, v0, v_first = vres
        in_specs += [const(vd_pad), const(vu_split), const(v0), tile(W)]
        args += [vd_pad, vu_split, v0, v_first]
    dir_tile = pl.BlockSpec((2, None, tm, W), lambda b, i: (0, b, i, 0))
    out_specs = [tile(B_COLS), tile(G_COLS), tile(W), tile(W), tile(W), dir_tile, dir_tile, dir_tile, tile(W), tile(W)]
    one = jax.ShapeDtypeStruct((bsz, T, W), F32)
    two = jax.ShapeDtypeStruct((2, bsz, T, W), F32)
    out_shape = [jax.ShapeDtypeStruct((bsz, T, B_COLS), F32), jax.ShapeDtypeStruct((bsz, T, G_COLS), BF16),
                 one, one, one, two, two, two, one, one]
    return pl.pallas_call(
        functools.partial(_proj_prep_body, has_vres=vres is not None),
        grid=(bsz, nt),
        in_specs=in_specs,
        out_specs=out_specs,
        out_shape=out_shape,
        scratch_shapes=[pltpu.VMEM((tm + 2 * HALO, D_MODEL), BF16), pltpu.VMEM((tm + 2 * HALO, A_COLS), F32)],
        compiler_params=_cparams(("parallel", "parallel")),
        name="proj_prep",
    )(*args)


def _wkv_step(side_work, rf_ref, vf_ref, af_ref, rb_ref, vb_ref, ab_ref, lwf_ref, kdf_ref, bbf_ref,
              lwb_ref, kdb_ref, bbb_ref, yf_ref, yb_ref, s_ref):
    bsz, C, _ = rf_ref.shape
    G = WKV_GROUP
    GW = G * HEAD
    NG = RWKV_HEADS // G
    assert C == HEAD

    row = lax.broadcasted_iota(jnp.int32, (C, GW), 0)
    lane = lax.broadcasted_iota(jnp.int32, (C, GW), 1)
    col = lane % HEAD
    eye = (row == col).astype(F32)
    same_blk = []
    size = INV_BASE
    while size < C:
        same_blk.append(((row // size) == (col // size)).astype(F32))
        size *= 2
    head_mask = [((lane // HEAD) == g).astype(BF16) for g in range(G)]
    srow = lax.broadcasted_iota(jnp.int32, (GW, GW), 0) // HEAD
    slane = lax.broadcasted_iota(jnp.int32, (GW, GW), 1) // HEAD
    state_mask = (srow == slane).astype(F32)
    trow = lax.broadcasted_iota(jnp.int32, (C, C), 0)
    tcol = lax.broadcasted_iota(jnp.int32, (C, C), 1)
    incl_d = [(col <= row).astype(F32), (col >= row).astype(F32)]
    strict_d = [(col < row).astype(F32), (col > row).astype(F32)]
    tri_d = [(tcol <= trow).astype(BF16), (tcol >= trow).astype(BF16)]

    b16 = lambda x: x.astype(BF16)
    bd = lambda xb: jnp.concatenate([xb * m for m in head_mask], axis=0)
    cat = lambda a, b, axis: jnp.concatenate([a, b], axis=axis)

    dir_refs = [(rf_ref, vf_ref, af_ref, lwf_ref, kdf_ref, bbf_ref, yf_ref),
                (rb_ref, vb_ref, ab_ref, lwb_ref, kdb_ref, bbb_ref, yb_ref)]
    chains = []
    at_g, rt_g, v_g, bt_g, kt_g, bh_g, kh_g, pt_g = [], [], [], [], [], [], [], []
    for d, (r_ref, v_ref, an_ref, lw_ref, kd_ref, bb_ref, _) in enumerate(dir_refs):
        for b in range(bsz):
            lw = lw_ref[b]
            cs = _dot_exact_lhs(tri_d[d], lw)
            tot = jnp.sum(lw, axis=0, keepdims=True)
            e_inv = jnp.exp(-cs)
            p_tot = jnp.exp(tot)
            rt = r_ref[b] * jnp.exp(cs)
            at = an_ref[b] * jnp.exp(cs - lw)
            bt = bb_ref[b] * e_inv
            kt = kd_ref[b] * e_inv
            bh = bt * p_tot
            kh = kt * p_tot
            vv = v_ref[b]
            for g in range(NG):
                sl = slice(g * GW, (g + 1) * GW)
                chains.append((d, b, g))
                for lst, val in ((at_g, at), (rt_g, rt), (v_g, vv), (bt_g, bt), (kt_g, kt), (bh_g, bh), (kh_g, kh),
                                 (pt_g, p_tot)):
                    lst.append(val[:, sl])
    strict = [strict_d[d] for d, _, _ in chains]
    incl = [incl_d[d] for d, _, _ in chains]

    at_b = [b16(a) for a in at_g]
    v_b = [b16(v) for v in v_g]
    ar_b = [cat(a, b16(r), 0) for a, r in zip(at_b, rt_g)]
    sbk = [_dot(ar, cat(bd(b16(xb)), bd(b16(xk)), 0), NT) for ar, xb, xk in zip(ar_b, bt_g, kt_g)]
    sb = [x[:, :GW] for x in sbk]
    sk = [x[:, GW:] for x in sbk]
    m_ab = [x[:C] * m for x, m in zip(sb, strict)]
    n_rb = [x[C:] * m for x, m in zip(sb, incl)]
    m_ak = [x[:C] * m for x, m in zip(sk, strict)]
    n_rk = [x[C:] * m for x, m in zip(sk, incl)]
    v_bd = [bd(v) for v in v_b]
    mv = [_dot(b16(m), vb) for m, vb in zip(m_ak, v_bd)]

    x = [eye + m * same_blk[0] for m in m_ab]
    for lvl in range(len(same_blk)):
        inner = same_blk[lvl]
        outer = same_blk[lvl + 1] if lvl + 1 < len(same_blk) else 1.0
        sel = outer - inner
        if lvl in side_work:
            done = side_work[lvl](_zero_after(x[0][:SUBLANES]))
        x_b = [b16(xi) for xi in x]
        t = [_dot(xb, bd(b16(m * sel))) for xb, m in zip(x_b, m_ab)]
        x = [xi + _dot(b16(ti), bd(xb)) for xi, ti, xb in zip(x, t, x_b)]
        if lvl in side_work:
            hold = jnp.tile(done, (C // SUBLANES, 1))
            x = [xi + hold for xi in x]

    w_ab = [_dot(b16(xi), cat(bd(a), bd(b16(m)), 1)) for xi, a, m in zip(x, at_b, mv)]
    w_a_b = [b16(w[:, :GW]) for w in w_ab]
    w_b_b = [b16(w[:, GW:]) for w in w_ab]
    zero_bd = jnp.zeros((GW, GW), BF16)
    yqc = [_dot(cat(b16(n), b16(nk), 1), cat(cat(bd(wa), bd(wb), 1), cat(zero_bd, vb, 1), 0))
           for n, nk, wa, wb, vb in zip(n_rb, n_rk, w_a_b, w_b_b, v_bd)]
    y_q = [r + x[:, :GW] for r, x in zip(rt_g, yqc)]
    y_c = [x[:, GW:] for x in yqc]
    zero_c = jnp.zeros((C, GW), BF16)
    php = [_dot(cat(cat(wa, wb, 1), cat(zero_c, v, 1), 0), cat(b16(b), b16(k), 0), TN)
           for wa, wb, v, b, k in zip(w_a_b, w_b_b, v_b, bh_g, kh_g)]
    phi = [x[:GW] * state_mask for x in php]
    psi = [x[GW:] * state_mask for x in php]
    for i, (d, b, g) in enumerate(chains):
        s0 = s_ref[d, b, g]
        s0_b = b16(s0)
        dir_refs[d][-1][b, :, g * GW:(g + 1) * GW] = _dot(b16(y_q[i]), s0_b, NT) + y_c[i]
        s_ref[d, b, g] = s0 * pt_g[i] + _dot(s0_b, b16(phi[i])) + psi[i]


def _attn_bias_init(slope_ref, bias_ref):
    rows2 = 2 * BLOCK
    qi = lax.broadcasted_iota(jnp.int32, (rows2, 3 * BLOCK), 0) % BLOCK
    kpos = lax.broadcasted_iota(jnp.int32, (rows2, 3 * BLOCK), 1) - BLOCK
    upper = lax.broadcasted_iota(jnp.int32, (rows2, 3 * BLOCK), 0) >= BLOCK
    dist = jnp.abs(qi - kpos)
    for p in range(ATTN_Q_HEADS // 2):
        slope = jnp.where(upper, slope_ref[2 * p + 1], slope_ref[2 * p])
        bias_ref[p] = jnp.where(dist <= WINDOW, -slope * dist.astype(F32), NEG_INF)


def _attn_step(n, nb, q_ref, kp_ref, kc_ref, kn_ref, qg_ref, kg_ref, slope_ref, sink_ref, o_ref, bias_ref):
    pairs = ATTN_Q_HEADS // 2
    rows2 = 2 * BLOCK
    hq = _head_sum_matrix(ATTN_WIDTH)
    hk = _head_sum_matrix(KV_WIDTH)
    lane = lax.broadcasted_iota(jnp.int32, (BLOCK, LANES), 1)
    low = lane < HEAD
    srow = lax.broadcasted_iota(jnp.int32, (rows2, 1), 0)

    q = q_ref[:, :ATTN_WIDTH]
    q = q * lax.rsqrt(_dot_split2_rhs(q * q, hq) * (1.0 / HEAD) + RMS_EPS) * (qg_ref[...] * (HEAD ** -0.5))

    def dup(x, g):
        rolled = pltpu.roll(x, HEAD, axis=1)
        return (jnp.where(low, x, rolled) if g == 0 else jnp.where(low, rolled, x)).astype(BF16)

    k_dup, v_dup = [[], []], [[], []]
    for ref in (kp_ref, kc_ref, kn_ref):
        kx = ref[:, :KV_WIDTH]
        kx = kx * lax.rsqrt(_dot_split2_rhs(kx * kx, hk) * (1.0 / HEAD) + RMS_EPS) * kg_ref[...]
        vx = ref[:, KV_WIDTH:]
        for g in range(ATTN_KV_HEADS):
            k_dup[g].append(dup(kx, g))
            v_dup[g].append(dup(vx, g))
    k_cat = [jnp.concatenate(k_dup[g], axis=0) for g in range(ATTN_KV_HEADS)]
    v_cat = [jnp.concatenate(v_dup[g], axis=0) for g in range(ATTN_KV_HEADS)]
    edge_prev = jnp.where(n > 0, 0.0, NEG_INF)
    edge_next = jnp.where(n < nb - 1, 0.0, NEG_INF)

    def head_pair(p, start):
        g = (2 * p) // ATTN_GROUP
        qp = q[:, p * LANES:(p + 1) * LANES] + jnp.tile(start, (BLOCK // SUBLANES, 1))
        qs = jnp.concatenate([jnp.where(low, qp, 0.0), jnp.where(low, 0.0, qp)], axis=0).astype(BF16)
        s = _dot(qs, k_cat[g], NT) + bias_ref[p]
        s0 = s[:, :BLOCK] + edge_prev
        s1 = s[:, BLOCK:2 * BLOCK]
        s2 = s[:, 2 * BLOCK:] + edge_next
        sink = jnp.where(srow >= BLOCK, sink_ref[2 * p + 1], sink_ref[2 * p])
        m = jnp.maximum(jnp.max(jnp.maximum(jnp.maximum(s0, s1), s2), axis=-1, keepdims=True), sink)
        e0, e1, e2 = jnp.exp(s0 - m), jnp.exp(s1 - m), jnp.exp(s2 - m)
        den = jnp.sum(e0 + e1 + e2, axis=-1, keepdims=True) + jnp.exp(sink - m)
        e = jnp.concatenate([e0, e1, e2], axis=1).astype(BF16)
        o2 = _dot(e, v_cat[g]) * (1.0 / den)
        o = jnp.where(low, o2[:BLOCK], o2[BLOCK:])
        z_lo = ATTN_WIDTH + 2 * KV_WIDTH + p * LANES
        zp = q_ref[:, z_lo:z_lo + LANES]
        og = o * (zp * _sigmoid(zp))
        o_ref[:, p * LANES:(p + 1) * LANES] = og.astype(o_ref.dtype)
        return _zero_after(og[:SUBLANES])

    return head_pair


N_WKV_IN = 12
N_ATTN_IN = 8
ATTN_JOIN_LEVELS = (1, 2, 3, 4)


def _wkv_attn_body(*refs, nb):
    wkv_in = refs[:N_WKV_IN]
    attn_in = refs[N_WKV_IN:N_WKV_IN + N_ATTN_IN]
    yf_ref, yb_ref, o_ref, s_ref, bias_ref = refs[N_WKV_IN + N_ATTN_IN:]
    c = pl.program_id(0)

    @pl.when(c == 0)
    def _():
        s_ref[...] = jnp.zeros_like(s_ref)
        _attn_bias_init(attn_in[6], bias_ref)

    head_pair = _attn_step(c % nb, nb, *attn_in, o_ref, bias_ref)
    side_work = {lvl: functools.partial(head_pair, p) for p, lvl in enumerate(ATTN_JOIN_LEVELS)}
    _wkv_step(side_work, *wkv_in, yf_ref, yb_ref, s_ref)


def _wkv_attn(r, v, an, lw, kd, bb, ub, q_g, k_g, slopes, sink):
    bsz, T, W = r.shape
    C = CHUNK
    nc = T // C
    nb = T // BLOCK
    assert nc == bsz * nb
    gw = WKV_GROUP * HEAD
    fwd = pl.BlockSpec((bsz, C, W), lambda c: (0, c, 0))
    bwd = pl.BlockSpec((bsz, C, W), lambda c: (0, nc - 1 - c, 0))
    fwd_d = pl.BlockSpec((None, bsz, C, W), lambda c: (0, 0, c, 0))
    bwd_d = pl.BlockSpec((None, bsz, C, W), lambda c: (1, 0, nc - 1 - c, 0))
    kv_blk = ATTN_WIDTH // (2 * KV_WIDTH)
    kv = lambda f: pl.BlockSpec((None, BLOCK, 2 * KV_WIDTH), f)
    smem = pl.BlockSpec(memory_space=pltpu.SMEM)
    attn_specs = [
        pl.BlockSpec((None, BLOCK, B_COLS), lambda c: (c // nb, c % nb, 0)),
        kv(lambda c: (c // nb, jnp.maximum(c % nb - 1, 0), kv_blk)),
        kv(lambda c: (c // nb, c % nb, kv_blk)),
        kv(lambda c: (c // nb, jnp.minimum(c % nb + 1, nb - 1), kv_blk)),
        pl.BlockSpec((1, ATTN_WIDTH), lambda c: (0, 0)),
        pl.BlockSpec((1, KV_WIDTH), lambda c: (0, 0)),
        smem, smem,
    ]
    out = jax.ShapeDtypeStruct((bsz, T, W), F32)
    return pl.pallas_call(
        functools.partial(_wkv_attn_body, nb=nb),
        grid=(nc,),
        in_specs=[fwd, fwd, fwd, bwd, bwd, bwd, fwd_d, fwd_d, fwd_d, bwd_d, bwd_d, bwd_d] + attn_specs,
        out_specs=[fwd, bwd, pl.BlockSpec((None, BLOCK, ATTN_WIDTH), lambda c: (c // nb, c % nb, 0))],
        out_shape=[out, out, jax.ShapeDtypeStruct((bsz, T, ATTN_WIDTH), BF16)],
        scratch_shapes=[pltpu.VMEM((2, bsz, RWKV_HEADS // WKV_GROUP, gw, gw), F32),
                        pltpu.VMEM((ATTN_Q_HEADS // 2, 2 * BLOCK, 3 * BLOCK), F32)],
        compiler_params=_cparams(("arbitrary",)),
        name="wkv_attn",
    )(r, v, an, r, v, an, lw, kd, bb, lw, kd, bb, ub, ub, ub, ub, q_g, k_g, slopes, sink)


def _merge_out_body(x_ref, yf_ref, yb_ref, bonus_ref, gz_ref, ob_ref, ug_ref, p_ref, lnw_ref, lnb_ref,
                    pa_ref, pb_ref, wo_ref, pg_ref, gw_ref, pp_ref, o_ref):
    hsum = _head_sum_matrix(RWKV_WIDTH)
    y = yf_ref[...] + yb_ref[...]
    mean = _dot_split2_rhs(y, hsum) * (1.0 / HEAD)
    yc = y - mean
    var = _dot_split2_rhs(yc * yc, hsum) * (1.0 / HEAD)
    yn = yc * lax.rsqrt(var + GN_EPS) * lnw_ref[...] + lnb_ref[...]
    o_a = (yn + bonus_ref[...]) * gz_ref[...]
    y_a = _dot(o_a.astype(BF16), pa_ref[...])
    y_b = _dot(ob_ref[...], pb_ref[...])
    merged = (_sigmoid(ug_ref[:, :D_MODEL].astype(F32)) * y_a
              + _sigmoid(ug_ref[:, D_MODEL:].astype(F32)) * y_b)
    x1 = x_ref[...] + _dot(merged.astype(BF16), wo_ref[...])
    ple = _dot(p_ref[...].astype(BF16), pp_ref[...])
    hn = x1 * lax.rsqrt(jnp.mean(x1 * x1, axis=-1, keepdims=True) + RMS_EPS) * pg_ref[...]
    o_ref[...] = x1 + _sigmoid(_dot(hn.astype(BF16), gw_ref[...])) * ple


def _merge_out(x2, yf, yb, bonus, gz, ob, ug, p3, layer, lnw, lnb, pa, pb, wo, pg, gw, pp, tm=MERGE_TILE):
    rows = x2.shape[0]
    W = RWKV_WIDTH
    tile = lambda n: pl.BlockSpec((tm, n), lambda i: (i, 0))
    full = lambda a: pl.BlockSpec(a.shape, lambda i: (0,) * a.ndim)
    return pl.pallas_call(
        _merge_out_body,
        grid=(rows // tm,),
        in_specs=[tile(D_MODEL), tile(W), tile(W), tile(W), tile(W), tile(ATTN_WIDTH),
                  tile(G_COLS), pl.BlockSpec((None, tm, PLE_DIM), lambda i: (layer, i, 0)), full(lnw), full(lnb), full(pa), full(pb), full(wo), full(pg),
                  full(gw), full(pp)],
        out_specs=tile(D_MODEL),
        out_shape=jax.ShapeDtypeStruct((rows, D_MODEL), F32),
        compiler_params=_cparams(("parallel",)),
        name="merge_out",
    )(x2, yf, yb, bonus, gz, ob, ug, p3, lnw, lnb, pa, pb, wo, pg, gw, pp)


def _block_diag2(m):
    z = jnp.zeros_like(m[0])
    return jnp.concatenate([jnp.concatenate([m[0], z], axis=1), jnp.concatenate([z, m[1]], axis=1)], axis=0)


def _presplit(w):
    hi = w.astype(BF16)
    return jnp.stack([hi, (w - hi.astype(F32)).astype(BF16)])


def kernel(x, p, norm_g, w_in, shift_mu, decay_w0, decay_up, iclr_a0, iclr_up, vres_down, vres_up, vres_v0, k_k, k_a, r_k, ln_x_w, ln_x_b, q_norm_g, k_norm_g, sink, proj_a, proj_b, w_out, ple_norm_g, ple_gate_w, ple_proj):
    bsz, T, _ = x.shape
    depth = w_in.shape[0]
    rows = bsz * T
    W = RWKV_WIDTH
    slopes = jnp.asarray(2.0 ** (-8.0 * jnp.arange(1, ATTN_Q_HEADS + 1, dtype=F32) / ATTN_Q_HEADS), F32)
    v_first = None
    for i in range(depth):
        params = [shift_mu[i].reshape(1, A_COLS), decay_w0[i].reshape(1, 2 * W), _presplit(_block_diag2(decay_up[i])),
                  iclr_a0[i].reshape(1, 2 * W), _presplit(_block_diag2(iclr_up[i])), k_k[i].reshape(1, W),
                  k_a[i].reshape(1, W), r_k[i].reshape(1, W)]
        vres = None
        if i > 0:
            vu_pad = jnp.concatenate([vres_up[i - 1], jnp.zeros((LANES - VRES_RANK, W), F32)], axis=0)
            vd_pad = jnp.concatenate([vres_down[i - 1], jnp.zeros((D_MODEL, LANES - VRES_RANK), F32)], axis=1)
            vres = (vd_pad.astype(BF16), _presplit(vu_pad), vres_v0[i - 1].reshape(1, W), v_first)
        ub, ug, r, v, an, lw, kd, bb, bonus, gz = _proj_prep(x, norm_g[i].reshape(1, D_MODEL), w_in[i].astype(BF16),
                                                             params, vres)
        if i == 0:
            v_first = v
        yf, yb, ob = _wkv_attn(r, v, an, lw, kd, bb, ub, jnp.tile(q_norm_g[i], ATTN_Q_HEADS).reshape(1, ATTN_WIDTH),
                               jnp.tile(k_norm_g[i], ATTN_KV_HEADS).reshape(1, KV_WIDTH), slopes, sink[i])
        x2 = _merge_out(x.reshape(rows, D_MODEL), yf.reshape(rows, W), yb.reshape(rows, W), bonus.reshape(rows, W),
                        gz.reshape(rows, W), ob.reshape(rows, ATTN_WIDTH), ug.reshape(rows, G_COLS),
                        p.reshape(depth, rows, PLE_DIM), i, ln_x_w[i].reshape(1, W), ln_x_b[i].reshape(1, W),
                        proj_a[i].astype(BF16), proj_b[i].astype(BF16), w_out[i].astype(BF16),
                        ple_norm_g[i].reshape(1, D_MODEL), ple_gate_w[i].astype(BF16), ple_proj[i].astype(BF16))
        x = x2.reshape(bsz, T, D_MODEL)
    return x
```

```python
import functools

import jax
import jax.numpy as jnp
from jax import lax
from jax.experimental import pallas as pl
from jax.experimental.pallas import tpu as pltpu

F32 = jnp.float32
BF16 = jnp.bfloat16

D_MODEL = 1024
PLE_DIM = 256
RWKV_WIDTH = 512
HEAD = 64
RWKV_HEADS = RWKV_WIDTH // HEAD
LOW_RANK = 64
VRES_RANK = 32
ATTN_WIDTH = 512
ATTN_Q_HEADS = ATTN_WIDTH // HEAD
ATTN_KV_HEADS = 2
ATTN_GROUP = ATTN_Q_HEADS // ATTN_KV_HEADS
KV_WIDTH = ATTN_KV_HEADS * HEAD
WINDOW = 128
BLOCK = 128
RMS_EPS = 1e-6
GN_EPS = 64e-5
NEG_INF = -1e30
EXP_MINUS_HALF = 0.6065306597126334
A_COLS = 4 * RWKV_WIDTH + 4 * LOW_RANK
B_COLS = 2 * ATTN_WIDTH + 2 * KV_WIDTH
G_COLS = 2 * D_MODEL
LANES = 128
SUBLANES = 8
HALO = 16
CHUNK = 64
INV_BASE = 2
WKV_GROUP = 2
PROJ_TILE = 256
MERGE_TILE = 512
VMEM_LIMIT = 48 * 1024 * 1024

NN = ((1,), (0,))
NT = ((1,), (1,))
TN = ((0,), (0,))


def _dot(a, b, dims=NN):
    return lax.dot_general(a, b, (dims, ((), ())), preferred_element_type=F32)


def _split2(x):
    hi = x.astype(BF16)
    lo = (x - hi.astype(F32)).astype(BF16)
    return hi, lo


def _dot_exact_lhs(a_bf16, b, dims=NN):
    b1 = b.astype(BF16)
    r1 = b - b1.astype(F32)
    b2 = r1.astype(BF16)
    b3 = (r1 - b2.astype(F32)).astype(BF16)
    return _dot(a_bf16, b1, dims) + (_dot(a_bf16, b2, dims) + _dot(a_bf16, b3, dims))


def _dot_split2_rhs(a, b_bf16):
    hi, lo = _split2(a)
    return _dot(hi, b_bf16) + _dot(lo, b_bf16)


def _dot3_presplit(a, w_ref):
    ah, al = _split2(a)
    return _dot(ah, w_ref[0]) + (_dot(ah, w_ref[1]) + _dot(al, w_ref[0]))


def _zero_after(x):
    bits = lax.bitcast_convert_type(x, jnp.int32)
    half = jnp.int32(16)
    return lax.shift_right_logical(lax.shift_right_logical(bits, half), half).astype(F32)


def _sigmoid(x):
    return 1.0 / (1.0 + jnp.exp(-x))


def _head_sum_matrix(width):
    r = lax.broadcasted_iota(jnp.int32, (width, width), 0) // HEAD
    c = lax.broadcasted_iota(jnp.int32, (width, width), 1) // HEAD
    return (r == c).astype(BF16)


def _cparams(sem):
    return pltpu.CompilerParams(dimension_semantics=sem, vmem_limit_bytes=VMEM_LIMIT)


def _proj_prep_body(*refs, has_vres):
    if has_vres:
        (x_ref, xp_ref, xn_ref, g_ref, w_ref, mu_ref, w0_ref, dup_ref, a0_ref, iup_ref, kk_ref, ka_ref,
         rk_ref, vd_ref, vu_ref, v0_ref, vf_ref,
         ub_o, ug_o, r_o, v_o, an_o, lw_o, kd_o, bb_o, bonus_o, gz_o, hb_ref, ua_ref) = refs
    else:
        (x_ref, xp_ref, xn_ref, g_ref, w_ref, mu_ref, w0_ref, dup_ref, a0_ref, iup_ref, kk_ref, ka_ref,
         rk_ref,
         ub_o, ug_o, r_o, v_o, an_o, lw_o, kd_o, bb_o, bonus_o, gz_o, hb_ref, ua_ref) = refs
    tm = x_ref.shape[0]
    i = pl.program_id(1)
    last = pl.num_programs(1) - 1
    W = RWKV_WIDTH

    def normed(ref):
        x = ref[...]
        return (x * lax.rsqrt(jnp.mean(x * x, axis=-1, keepdims=True) + RMS_EPS) * g_ref[...]).astype(BF16)

    hb_ref[0:HALO] = jnp.where(i > 0, normed(xp_ref), jnp.zeros((), BF16))
    hb_ref[HALO:HALO + tm] = normed(x_ref)
    hb_ref[HALO + tm:] = jnp.where(i < last, normed(xn_ref), jnp.zeros((), BF16))
    ua_ref[...] = _dot(hb_ref[...], w_ref[:, :A_COLS])
    hm = hb_ref[HALO:HALO + tm]

    def project_b(lo, hi):
        ub_o[:, lo:hi] = _dot(hm, w_ref[:, A_COLS + lo:A_COLS + hi])

    def project_g(lo, hi):
        ug_o[:, lo:hi] = _dot(hm, w_ref[:, A_COLS + B_COLS + lo:A_COLS + B_COLS + hi]).astype(ug_o.dtype)

    hsum = _head_sum_matrix(W)

    def shifted(lo, hi):
        u = ua_ref[HALO:HALO + tm, lo:hi]
        prev = ua_ref[HALO - 1:HALO - 1 + tm, lo:hi]
        nxt = ua_ref[HALO + 1:HALO + 1 + tm, lo:hi]
        return u + mu_ref[:, lo:hi] * (0.5 * (prev + nxt) - u)

    r = shifted(0, W)
    project_b(0, ATTN_WIDTH)
    k = shifted(W, 2 * W)
    project_b(ATTN_WIDTH, B_COLS)
    v = shifted(2 * W, 3 * W)
    project_g(0, D_MODEL // 2)
    z = shifted(3 * W, 4 * W)
    project_g(D_MODEL // 2, D_MODEL)
    low = shifted(4 * W, 4 * W + 4 * LOW_RANK)
    w_raw = w0_ref[...] + _dot3_presplit(jnp.tanh(low[:, :2 * LOW_RANK]), dup_ref)
    project_g(D_MODEL, D_MODEL + D_MODEL // 2)
    lw = -EXP_MINUS_HALF * _sigmoid(w_raw)
    a = _sigmoid(a0_ref[...] + _dot3_presplit(low[:, 2 * LOW_RANK:], iup_ref))
    if has_vres:
        hd = _dot(hm, vd_ref[...])
        mix = _sigmoid(v0_ref[...] + _dot3_presplit(hd, vu_ref))
        v = v + (vf_ref[...] - v) * mix
    project_g(D_MODEL + D_MODEL // 2, G_COLS)
    kk = k * kk_ref[...]
    ss = _dot_split2_rhs(kk * kk, hsum)
    kk = kk * lax.rsqrt(jnp.maximum(ss, 1e-24))
    ka = ka_ref[...]
    ksum = jnp.zeros_like(k)
    for d in range(2):
        a_d = a[:, d * W:(d + 1) * W]
        kd = k * (1.0 + (a_d - 1.0) * ka)
        ksum = ksum + kd
        lw_o[d] = lw[:, d * W:(d + 1) * W]
        kd_o[d] = kd
        bb_o[d] = kk * a_d
    r_o[...] = r
    v_o[...] = v
    an_o[...] = -kk
    bonus_o[...] = _dot_split2_rhs(r * ksum * rk_ref[...], hsum) * v
    gz_o[...] = z * _sigmoid(z)


def _proj_prep(x, g, w, params, vres, tm=PROJ_TILE):
    bsz, T, _ = x.shape
    W = RWKV_WIDTH
    nt = T // tm
    hb = tm // HALO
    nh = T // HALO
    const = lambda a: pl.BlockSpec(a.shape, lambda b, i: (0,) * a.ndim, pipeline_mode=pl.Buffered(1))
    tile = lambda n: pl.BlockSpec((None, tm, n), lambda b, i: (b, i, 0))
    in_specs = [
        tile(D_MODEL),
        pl.BlockSpec((None, HALO, D_MODEL), lambda b, i: (b, jnp.maximum(i * hb - 1, 0), 0)),
        pl.BlockSpec((None, HALO, D_MODEL), lambda b, i: (b, jnp.minimum((i + 1) * hb, nh - 1), 0)),
        const(g), const(w),
    ] + [const(a) for a in params]
    args = [x, x, x, g, w] + list(params)
    if vres is not None:
        vd_pad, vu_split, v0, v_first = vres
        in_specs += [const(vd_pad), const(vu_split), const(v0), tile(W)]
        args += [vd_pad, vu_split, v0, v_first]
    dir_tile = pl.BlockSpec((2, None, tm, W), lambda b, i: (0, b, i, 0))
    out_specs = [tile(B_COLS), tile(G_COLS), tile(W), tile(W), tile(W), dir_tile, dir_tile, dir_tile, tile(W), tile(W)]
    one = jax.ShapeDtypeStruct((bsz, T, W), F32)
    two = jax.ShapeDtypeStruct((2, bsz, T, W), F32)
    out_shape = [jax.ShapeDtypeStruct((bsz, T, B_COLS), F32), jax.ShapeDtypeStruct((bsz, T, G_COLS), BF16),
                 one, one, one, two, two, two, one, one]
    return pl.pallas_call(
        functools.partial(_proj_prep_body, has_vres=vres is not None),
        grid=(bsz, nt),
        in_specs=in_specs,
        out_specs=out_specs,
        out_shape=out_shape,
        scratch_shapes=[pltpu.VMEM((tm + 2 * HALO, D_MODEL), BF16), pltpu.VMEM((tm + 2 * HALO, A_COLS), F32)],
        compiler_params=_cparams(("parallel", "parallel")),
        name="proj_prep",
    )(*args)


def _wkv_step(side_work, rf_ref, vf_ref, af_ref, rb_ref, vb_ref, ab_ref, lwf_ref, kdf_ref, bbf_ref,
              lwb_ref, kdb_ref, bbb_ref, yf_ref, yb_ref, s_ref):
    bsz, C, _ = rf_ref.shape
    G = WKV_GROUP
    GW = G * HEAD
    NG = RWKV_HEADS // G
    assert C == HEAD

    row = lax.broadcasted_iota(jnp.int32, (C, GW), 0)
    lane = lax.broadcasted_iota(jnp.int32, (C, GW), 1)
    col = lane % HEAD
    eye = (row == col).astype(F32)
    same_blk = []
    size = INV_BASE
    while size < C:
        same_blk.append(((row // size) == (col // size)).astype(F32))
        size *= 2
    head_mask = [((lane // HEAD) == g).astype(BF16) for g in range(G)]
    srow = lax.broadcasted_iota(jnp.int32, (GW, GW), 0) // HEAD
    slane = lax.broadcasted_iota(jnp.int32, (GW, GW), 1) // HEAD
    state_mask = (srow == slane).astype(F32)
    trow = lax.broadcasted_iota(jnp.int32, (C, C), 0)
    tcol = lax.broadcasted_iota(jnp.int32, (C, C), 1)
    incl_d = [(col <= row).astype(F32), (col >= row).astype(F32)]
    strict_d = [(col < row).astype(F32), (col > row).astype(F32)]
    tri_d = [(tcol <= trow).astype(BF16), (tcol >= trow).astype(BF16)]

    b16 = lambda x: x.astype(BF16)
    bd = lambda xb: jnp.concatenate([xb * m for m in head_mask], axis=0)
    cat = lambda a, b, axis: jnp.concatenate([a, b], axis=axis)

    dir_refs = [(rf_ref, vf_ref, af_ref, lwf_ref, kdf_ref, bbf_ref, yf_ref),
                (rb_ref, vb_ref, ab_ref, lwb_ref, kdb_ref, bbb_ref, yb_ref)]
    chains = []
    at_g, rt_g, v_g, bt_g, kt_g, bh_g, kh_g, pt_g = [], [], [], [], [], [], [], []
    for d, (r_ref, v_ref, an_ref, lw_ref, kd_ref, bb_ref, _) in enumerate(dir_refs):
        for b in range(bsz):
            lw = lw_ref[b]
            cs = _dot_exact_lhs(tri_d[d], lw)
            tot = jnp.sum(lw, axis=0, keepdims=True)
            e_inv = jnp.exp(-cs)
            p_tot = jnp.exp(tot)
            rt = r_ref[b] * jnp.exp(cs)
            at = an_ref[b] * jnp.exp(cs - lw)
            bt = bb_ref[b] * e_inv
            kt = kd_ref[b] * e_inv
            bh = bt * p_tot
            kh = kt * p_tot
            vv = v_ref[b]
            for g in range(NG):
                sl = slice(g * GW, (g + 1) * GW)
                chains.append((d, b, g))
                for lst, val in ((at_g, at), (rt_g, rt), (v_g, vv), (bt_g, bt), (kt_g, kt), (bh_g, bh), (kh_g, kh),
                                 (pt_g, p_tot)):
                    lst.append(val[:, sl])
    strict = [strict_d[d] for d, _, _ in chains]
    incl = [incl_d[d] for d, _, _ in chains]

    at_b = [b16(a) for a in at_g]
    v_b = [b16(v) for v in v_g]
    ar_b = [cat(a, b16(r), 0) for a, r in zip(at_b, rt_g)]
    sbk = [_dot(ar, cat(bd(b16(xb)), bd(b16(xk)), 0), NT) for ar, xb, xk in zip(ar_b, bt_g, kt_g)]
    sb = [x[:, :GW] for x in sbk]
    sk = [x[:, GW:] for x in sbk]
    m_ab = [x[:C] * m for x, m in zip(sb, strict)]
    n_rb = [x[C:] * m for x, m in zip(sb, incl)]
    m_ak = [x[:C] * m for x, m in zip(sk, strict)]
    n_rk = [x[C:] * m for x, m in zip(sk, incl)]
    v_bd = [bd(v) for v in v_b]
    mv = [_dot(b16(m), vb) for m, vb in zip(m_ak, v_bd)]

    x = [eye + m * same_blk[0] for m in m_ab]
    for lvl in range(len(same_blk)):
        inner = same_blk[lvl]
        outer = same_blk[lvl + 1] if lvl + 1 < len(same_blk) else 1.0
        sel = outer - inner
        if lvl in side_work:
            done = side_work[lvl](_zero_after(x[0][:SUBLANES]))
        x_b = [b16(xi) for xi in x]
        t = [_dot(xb, bd(b16(m * sel))) for xb, m in zip(x_b, m_ab)]
        x = [xi + _dot(b16(ti), bd(xb)) for xi, ti, xb in zip(x, t, x_b)]
        if lvl in side_work:
            hold = jnp.tile(done, (C // SUBLANES, 1))
            x = [xi + hold for xi in x]

    w_ab = [_dot(b16(xi), cat(bd(a), bd(b16(m)), 1)) for xi, a, m in zip(x, at_b, mv)]
    w_a_b = [b16(w[:, :GW]) for w in w_ab]
    w_b_b = [b16(w[:, GW:]) for w in w_ab]
    zero_bd = jnp.zeros((GW, GW), BF16)
    yqc = [_dot(cat(b16(n), b16(nk), 1), cat(cat(bd(wa), bd(wb), 1), cat(zero_bd, vb, 1), 0))
           for n, nk, wa, wb, vb in zip(n_rb, n_rk, w_a_b, w_b_b, v_bd)]
    y_q = [r + x[:, :GW] for r, x in zip(rt_g, yqc)]
    y_c = [x[:, GW:] for x in yqc]
    zero_c = jnp.zeros((C, GW), BF16)
    php = [_dot(cat(cat(wa, wb, 1), cat(zero_c, v, 1), 0), cat(b16(b), b16(k), 0), TN)
           for wa, wb, v, b, k in zip(w_a_b, w_b_b, v_b, bh_g, kh_g)]
    phi = [x[:GW] * state_mask for x in php]
    psi = [x[GW:] * state_mask for x in php]
    for i, (d, b, g) in enumerate(chains):
        s0 = s_ref[d, b, g]
        s0_b = b16(s0)
        dir_refs[d][-1][b, :, g * GW:(g + 1) * GW] = _dot(b16(y_q[i]), s0_b, NT) + y_c[i]
        s_ref[d, b, g] = s0 * pt_g[i] + _dot(s0_b, b16(phi[i])) + psi[i]


def _attn_bias_init(slope_ref, bias_ref):
    rows2 = 2 * BLOCK
    qi = lax.broadcasted_iota(jnp.int32, (rows2, 3 * BLOCK), 0) % BLOCK
    kpos = lax.broadcasted_iota(jnp.int32, (rows2, 3 * BLOCK), 1) - BLOCK
    upper = lax.broadcasted_iota(jnp.int32, (rows2, 3 * BLOCK), 0) >= BLOCK
    dist = jnp.abs(qi - kpos)
    for p in range(ATTN_Q_HEADS // 2):
        slope = jnp.where(upper, slope_ref[2 * p + 1], slope_ref[2 * p])
        bias_ref[p] = jnp.where(dist <= WINDOW, -slope * dist.astype(F32), NEG_INF)


def _attn_step(n, nb, q_ref, kp_ref, kc_ref, kn_ref, qg_ref, kg_ref, slope_ref, sink_ref, o_ref, bias_ref):
    pairs = ATTN_Q_HEADS // 2
    rows2 = 2 * BLOCK
    hq = _head_sum_matrix(ATTN_WIDTH)
    hk = _head_sum_matrix(KV_WIDTH)
    lane = lax.broadcasted_iota(jnp.int32, (BLOCK, LANES), 1)
    low = lane < HEAD
    srow = lax.broadcasted_iota(jnp.int32, (rows2, 1), 0)

    q = q_ref[:, :ATTN_WIDTH]
    q = q * lax.rsqrt(_dot_split2_rhs(q * q, hq) * (1.0 / HEAD) + RMS_EPS) * (qg_ref[...] * (HEAD ** -0.5))

    def dup(x, g):
        rolled = pltpu.roll(x, HEAD, axis=1)
        return (jnp.where(low, x, rolled) if g == 0 else jnp.where(low, rolled, x)).astype(BF16)

    k_dup, v_dup = [[], []], [[], []]
    for ref in (kp_ref, kc_ref, kn_ref):
        kx = ref[:, :KV_WIDTH]
        kx = kx * lax.rsqrt(_dot_split2_rhs(kx * kx, hk) * (1.0 / HEAD) + RMS_EPS) * kg_ref[...]
        vx = ref[:, KV_WIDTH:]
        for g in range(ATTN_KV_HEADS):
            k_dup[g].append(dup(kx, g))
            v_dup[g].append(dup(vx, g))
    k_cat = [jnp.concatenate(k_dup[g], axis=0) for g in range(ATTN_KV_HEADS)]
    v_cat = [jnp.concatenate(v_dup[g], axis=0) for g in range(ATTN_KV_HEADS)]
    edge_prev = jnp.where(n > 0, 0.0, NEG_INF)
    edge_next = jnp.where(n < nb - 1, 0.0, NEG_INF)

    def head_pair(p, start):
        g = (2 * p) // ATTN_GROUP
        qp = q[:, p * LANES:(p + 1) * LANES] + jnp.tile(start, (BLOCK // SUBLANES, 1))
        qs = jnp.concatenate([jnp.where(low, qp, 0.0), jnp.where(low, 0.0, qp)], axis=0).astype(BF16)
        s = _dot(qs, k_cat[g], NT) + bias_ref[p]
        s0 = s[:, :BLOCK] + edge_prev
        s1 = s[:, BLOCK:2 * BLOCK]
        s2 = s[:, 2 * BLOCK:] + edge_next
        sink = jnp.where(srow >= BLOCK, sink_ref[2 * p + 1], sink_ref[2 * p])
        m = jnp.maximum(jnp.max(jnp.maximum(jnp.maximum(s0, s1), s2), axis=-1, keepdims=True), sink)
        e0, e1, e2 = jnp.exp(s0 - m), jnp.exp(s1 - m), jnp.exp(s2 - m)
        den = jnp.sum(e0 + e1 + e2, axis=-1, keepdims=True) + jnp.exp(sink - m)
        e = jnp.concatenate([e0, e1, e2], axis=1).astype(BF16)
        o2 = _dot(e, v_cat[g]) * (1.0 / den)
        o = jnp.where(low, o2[:BLOCK], o2[BLOCK:])
        z_lo = ATTN_WIDTH + 2 * KV_WIDTH + p * LANES
        zp = q_ref[:, z_lo:z_lo + LANES]
        og = o * (zp * _sigmoid(zp))
        o_ref[:, p * LANES:(p + 1) * LANES] = og.astype(o_ref.dtype)
        return _zero_after(og[:SUBLANES])

    return head_pair


N_WKV_IN = 12
N_ATTN_IN = 8
ATTN_JOIN_LEVELS = (1, 2, 3, 4)


def _wkv_attn_body(*refs, nb):
    wkv_in = refs[:N_WKV_IN]
    attn_in = refs[N_WKV_IN:N_WKV_IN + N_ATTN_IN]
    yf_ref, yb_ref, o_ref, s_ref, bias_ref = refs[N_WKV_IN + N_ATTN_IN:]
    c = pl.program_id(0)

    @pl.when(c == 0)
    def _():
        s_ref[...] = jnp.zeros_like(s_ref)
        _attn_bias_init(attn_in[6], bias_ref)

    head_pair = _attn_step(c % nb, nb, *attn_in, o_ref, bias_ref)
    side_work = {lvl: functools.partial(head_pair, p) for p, lvl in enumerate(ATTN_JOIN_LEVELS)}
    _wkv_step(side_work, *wkv_in, yf_ref, yb_ref, s_ref)


def _wkv_attn(r, v, an, lw, kd, bb, ub, q_g, k_g, slopes, sink):
    bsz, T, W = r.shape
    C = CHUNK
    nc = T // C
    nb = T // BLOCK
    assert nc == bsz * nb
    gw = WKV_GROUP * HEAD
    fwd = pl.BlockSpec((bsz, C, W), lambda c: (0, c, 0))
    bwd = pl.BlockSpec((bsz, C, W), lambda c: (0, nc - 1 - c, 0))
    fwd_d = pl.BlockSpec((None, bsz, C, W), lambda c: (0, 0, c, 0))
    bwd_d = pl.BlockSpec((None, bsz, C, W), lambda c: (1, 0, nc - 1 - c, 0))
    kv_blk = ATTN_WIDTH // (2 * KV_WIDTH)
    kv = lambda f: pl.BlockSpec((None, BLOCK, 2 * KV_WIDTH), f)
    smem = pl.BlockSpec(memory_space=pltpu.SMEM)
    attn_specs = [
        pl.BlockSpec((None, BLOCK, B_COLS), lambda c: (c // nb, c % nb, 0)),
        kv(lambda c: (c // nb, jnp.maximum(c % nb - 1, 0), kv_blk)),
        kv(lambda c: (c // nb, c % nb, kv_blk)),
        kv(lambda c: (c // nb, jnp.minimum(c % nb + 1, nb - 1), kv_blk)),
        pl.BlockSpec((1, ATTN_WIDTH), lambda c: (0, 0)),
        pl.BlockSpec((1, KV_WIDTH), lambda c: (0, 0)),
        smem, smem,
    ]
    out = jax.ShapeDtypeStruct((bsz, T, W), F32)
    return pl.pallas_call(
        functools.partial(_wkv_attn_body, nb=nb),
        grid=(nc,),
        in_specs=[fwd, fwd, fwd, bwd, bwd, bwd, fwd_d, fwd_d, fwd_d, bwd_d, bwd_d, bwd_d] + attn_specs,
        out_specs=[fwd, bwd, pl.BlockSpec((None, BLOCK, ATTN_WIDTH), lambda c: (c // nb, c % nb, 0))],
        out_shape=[out, out, jax.ShapeDtypeStruct((bsz, T, ATTN_WIDTH), BF16)],
        scratch_shapes=[pltpu.VMEM((2, bsz, RWKV_HEADS // WKV_GROUP, gw, gw), F32),
                        pltpu.VMEM((ATTN_Q_HEADS // 2, 2 * BLOCK, 3 * BLOCK), F32)],
        compiler_params=_cparams(("arbitrary",)),
        name="wkv_attn",
    )(r, v, an, r, v, an, lw, kd, bb, lw, kd, bb, ub, ub, ub, ub, q_g, k_g, slopes, sink)


def _merge_out_body(x_ref, yf_ref, yb_ref, bonus_ref, gz_ref, ob_ref, ug_ref, p_ref, lnw_ref, lnb_ref,
                    pa_ref, pb_ref, wo_ref, pg_ref, gw_ref, pp_ref, o_ref):
    hsum = _head_sum_matrix(RWKV_WIDTH)
    y = yf_ref[...] + yb_ref[...]
    mean = _dot_split2_rhs(y, hsum) * (1.0 / HEAD)
    yc = y - mean
    var = _dot_split2_rhs(yc * yc, hsum) * (1.0 / HEAD)
    yn = yc * lax.rsqrt(var + GN_EPS) * lnw_ref[...] + lnb_ref[...]
    o_a = (yn + bonus_ref[...]) * gz_ref[...]
    y_a = _dot(o_a.astype(BF16), pa_ref[...])
    y_b = _dot(ob_ref[...], pb_ref[...])
    merged = (_sigmoid(ug_ref[:, :D_MODEL].astype(F32)) * y_a
              + _sigmoid(ug_ref[:, D_MODEL:].astype(F32)) * y_b)
    x1 = x_ref[...] + _dot(merged.astype(BF16), wo_ref[...])
    ple = _dot(p_ref[...].astype(BF16), pp_ref[...])
    hn = x1 * lax.rsqrt(jnp.mean(x1 * x1, axis=-1, keepdims=True) + RMS_EPS) * pg_ref[...]
    o_ref[...] = x1 + _sigmoid(_dot(hn.astype(BF16), gw_ref[...])) * ple


def _merge_out(x2, yf, yb, bonus, gz, ob, ug, p3, layer, lnw, lnb, pa, pb, wo, pg, gw, pp, tm=MERGE_TILE):
    rows = x2.shape[0]
    W = RWKV_WIDTH
    tile = lambda n: pl.BlockSpec((tm, n), lambda i: (i, 0))
    full = lambda a: pl.BlockSpec(a.shape, lambda i: (0,) * a.ndim)
    return pl.pallas_call(
        _merge_out_body,
        grid=(rows // tm,),
        in_specs=[tile(D_MODEL), tile(W), tile(W), tile(W), tile(W), tile(ATTN_WIDTH),
                  tile(G_COLS), pl.BlockSpec((None, tm, PLE_DIM), lambda i: (layer, i, 0)), full(lnw), full(lnb), full(pa), full(pb), full(wo), full(pg),
                  full(gw), full(pp)],
        out_specs=tile(D_MODEL),
        out_shape=jax.ShapeDtypeStruct((rows, D_MODEL), F32),
        compiler_params=_cparams(("parallel",)),
        name="merge_out",
    )(x2, yf, yb, bonus, gz, ob, ug, p3, lnw, lnb, pa, pb, wo, pg, gw, pp)


def _block_diag2(m):
    z = jnp.zeros_like(m[0])
    return jnp.concatenate([jnp.concatenate([m[0], z], axis=1), jnp.concatenate([z, m[1]], axis=1)], axis=0)


def _presplit(w):
    hi = w.astype(BF16)
    return jnp.stack([hi, (w - hi.astype(F32)).astype(BF16)])


def kernel(x, p, norm_g, w_in, shift_mu, decay_w0, decay_up, iclr_a0, iclr_up, vres_down, vres_up, vres_v0, k_k, k_a, r_k, ln_x_w, ln_x_b, q_norm_g, k_norm_g, sink, proj_a, proj_b, w_out, ple_norm_g, ple_gate_w, ple_proj):
    bsz, T, _ = x.shape
    depth = w_in.shape[0]
    rows = bsz * T
    W = RWKV_WIDTH
    slopes = jnp.asarray(2.0 ** (-8.0 * jnp.arange(1, ATTN_Q_HEADS + 1, dtype=F32) / ATTN_Q_HEADS), F32)
    v_first = None
    for i in range(depth):
        params = [shift_mu[i].reshape(1, A_COLS), decay_w0[i].reshape(1, 2 * W), _presplit(_block_diag2(decay_up[i])),
                  iclr_a0[i].reshape(1, 2 * W), _presplit(_block_diag2(iclr_up[i])), k_k[i].reshape(1, W),
                  k_a[i].reshape(1, W), r_k[i].reshape(1, W)]
        vres = None
        if i > 0:
            vu_pad = jnp.concatenate([vres_up[i - 1], jnp.zeros((LANES - VRES_RANK, W), F32)], axis=0)
            vd_pad = jnp.concatenate([vres_down[i - 1], jnp.zeros((D_MODEL, LANES - VRES_RANK), F32)], axis=1)
            vres = (vd_pad.astype(BF16), _presplit(vu_pad), vres_v0[i - 1].reshape(1, W), v_first)
        ub, ug, r, v, an, lw, kd, bb, bonus, gz = _proj_prep(x, norm_g[i].reshape(1, D_MODEL), w_in[i].astype(BF16),
                                                             params, vres)
        if i == 0:
            v_first = v
        yf, yb, ob = _wkv_attn(r, v, an, lw, kd, bb, ub, jnp.tile(q_norm_g[i], ATTN_Q_HEADS).reshape(1, ATTN_WIDTH),
                               jnp.tile(k_norm_g[i], ATTN_KV_HEADS).reshape(1, KV_WIDTH), slopes, sink[i])
        x2 = _merge_out(x.reshape(rows, D_MODEL), yf.reshape(rows, W), yb.reshape(rows, W), bonus.reshape(rows, W),
                        gz.reshape(rows, W), ob.reshape(rows, ATTN_WIDTH), ug.reshape(rows, G_COLS),
                        p.reshape(depth, rows, PLE_DIM), i, ln_x_w[i].reshape(1, W), ln_x_b[i].reshape(1, W),
                        proj_a[i].astype(BF16), proj_b[i].astype(BF16), w_out[i].astype(BF16),
                        ple_norm_g[i].reshape(1, D_MODEL), ple_gate_w[i].astype(BF16), ple_proj[i].astype(BF16))
        x = x2.reshape(bsz, T, D_MODEL)
    return x
```

```python
import functools

import jax
import jax.numpy as jnp
from jax import lax
from jax.experimental import pallas as pl
from jax.experimental.pallas import tpu as pltpu

F32 = jnp.float32
BF16 = jnp.bfloat16

D_MODEL = 1024
PLE_DIM = 256
RWKV_WIDTH = 512
HEAD = 64
RWKV_HEADS = RWKV_WIDTH // HEAD
LOW_RANK = 64
VRES_RANK = 32
ATTN_WIDTH = 512
ATTN_Q_HEADS = ATTN_WIDTH // HEAD
ATTN_KV_HEADS = 2
ATTN_GROUP = ATTN_Q_HEADS // ATTN_KV_HEADS
KV_WIDTH = ATTN_KV_HEADS * HEAD
WINDOW = 128
BLOCK = 128
RMS_EPS = 1e-6
GN_EPS = 64e-5
NEG_INF = -1e30
EXP_MINUS_HALF = 0.6065306597126334
A_COLS = 4 * RWKV_WIDTH + 4 * LOW_RANK
B_COLS = 2 * ATTN_WIDTH + 2 * KV_WIDTH
G_COLS = 2 * D_MODEL
LANES = 128
SUBLANES = 8
HALO = 16
CHUNK = 64
INV_BASE = 2
WKV_GROUP = 2
PROJ_TILE = 256
MERGE_TILE = 512
VMEM_LIMIT = 48 * 1024 * 1024

NN = ((1,), (0,))
NT = ((1,), (1,))
TN = ((0,), (0,))


def _dot(a, b, dims=NN):
    return lax.dot_general(a, b, (dims, ((), ())), preferred_element_type=F32)


def _split2(x):
    hi = x.astype(BF16)
    lo = (x - hi.astype(F32)).astype(BF16)
    return hi, lo


def _dot_exact_lhs(a_bf16, b, dims=NN):
    b1 = b.astype(BF16)
    r1 = b - b1.astype(F32)
    b2 = r1.astype(BF16)
    b3 = (r1 - b2.astype(F32)).astype(BF16)
    return _dot(a_bf16, b1, dims) + (_dot(a_bf16, b2, dims) + _dot(a_bf16, b3, dims))


def _dot_split2_rhs(a, b_bf16):
    hi, lo = _split2(a)
    return _dot(hi, b_bf16) + _dot(lo, b_bf16)


def _head_sums(a, hsum_bf16):
    return _dot(a.astype(BF16), hsum_bf16)


def _zero_after(x):
    bits = lax.bitcast_convert_type(x, jnp.int32)
    half = jnp.int32(16)
    return lax.shift_right_logical(lax.shift_right_logical(bits, half), half).astype(F32)


def _sigmoid(x):
    return 1.0 / (1.0 + jnp.exp(-x))


def _head_sum_matrix(width):
    r = lax.broadcasted_iota(jnp.int32, (width, width), 0) // HEAD
    c = lax.broadcasted_iota(jnp.int32, (width, width), 1) // HEAD
    return (r == c).astype(BF16)


def _cparams(sem):
    return pltpu.CompilerParams(dimension_semantics=sem, vmem_limit_bytes=VMEM_LIMIT)


def _proj_prep_body(*refs, has_vres):
    if has_vres:
        (x_ref, xp_ref, xn_ref, g_ref, w_ref, mu_ref, w0_ref, dup_ref, a0_ref, iup_ref, kk_ref, ka_ref,
         rk_ref, vd_ref, vu_ref, v0_ref, vf_ref,
         ub_o, ug_o, r_o, v_o, an_o, lw_o, kd_o, bb_o, bonus_o, gz_o, hb_ref, ua_ref) = refs
    else:
        (x_ref, xp_ref, xn_ref, g_ref, w_ref, mu_ref, w0_ref, dup_ref, a0_ref, iup_ref, kk_ref, ka_ref,
         rk_ref,
         ub_o, ug_o, r_o, v_o, an_o, lw_o, kd_o, bb_o, bonus_o, gz_o, hb_ref, ua_ref) = refs
    tm = x_ref.shape[0]
    i = pl.program_id(1)
    last = pl.num_programs(1) - 1
    W = RWKV_WIDTH

    def normed(ref):
        x = ref[...]
        return (x * lax.rsqrt(jnp.mean(x * x, axis=-1, keepdims=True) + RMS_EPS) * g_ref[...]).astype(BF16)

    hb_ref[0:HALO] = jnp.where(i > 0, normed(xp_ref), jnp.zeros((), BF16))
    hb_ref[HALO:HALO + tm] = normed(x_ref)
    hb_ref[HALO + tm:] = jnp.where(i < last, normed(xn_ref), jnp.zeros((), BF16))
    ua_ref[...] = _dot(hb_ref[...], w_ref[:, :A_COLS])
    hm = hb_ref[HALO:HALO + tm]

    def project_b(lo, hi):
        ub_o[:, lo:hi] = _dot(hm, w_ref[:, A_COLS + lo:A_COLS + hi])

    def project_g(lo, hi):
        ug_o[:, lo:hi] = _dot(hm, w_ref[:, A_COLS + B_COLS + lo:A_COLS + B_COLS + hi]).astype(ug_o.dtype)

    hsum = _head_sum_matrix(W)

    def shifted(lo, hi):
        u = ua_ref[HALO:HALO + tm, lo:hi]
        prev = ua_ref[HALO - 1:HALO - 1 + tm, lo:hi]
        nxt = ua_ref[HALO + 1:HALO + 1 + tm, lo:hi]
        return u + mu_ref[:, lo:hi] * (0.5 * (prev + nxt) - u)

    r = shifted(0, W)
    project_b(0, ATTN_WIDTH)
    k = shifted(W, 2 * W)
    project_b(ATTN_WIDTH, B_COLS)
    v = shifted(2 * W, 3 * W)
    project_g(0, D_MODEL // 2)
    z = shifted(3 * W, 4 * W)
    project_g(D_MODEL // 2, D_MODEL)
    low = shifted(4 * W, 4 * W + 4 * LOW_RANK)
    w_raw = w0_ref[...] + _dot(jnp.tanh(low[:, :2 * LOW_RANK]).astype(BF16), dup_ref[...])
    project_g(D_MODEL, D_MODEL + D_MODEL // 2)
    lw = -EXP_MINUS_HALF * _sigmoid(w_raw)
    a = _sigmoid(a0_ref[...] + _dot(low[:, 2 * LOW_RANK:].astype(BF16), iup_ref[...]))
    if has_vres:
        hd = _dot(hm, vd_ref[...])
        mix = _sigmoid(v0_ref[...] + _dot(hd.astype(BF16), vu_ref[...]))
        v = v + (vf_ref[...] - v) * mix
    project_g(D_MODEL + D_MODEL // 2, G_COLS)
    kk = k * kk_ref[...]
    ss = _head_sums(kk * kk, hsum)
    kk = kk * lax.rsqrt(jnp.maximum(ss, 1e-24))
    ka = ka_ref[...]
    ksum = jnp.zeros_like(k)
    for d in range(2):
        a_d = a[:, d * W:(d + 1) * W]
        kd = k * (1.0 + (a_d - 1.0) * ka)
        ksum = ksum + kd
        lw_o[d] = lw[:, d * W:(d + 1) * W]
        kd_o[d] = kd
        bb_o[d] = kk * a_d
    r_o[...] = r
    v_o[...] = v
    an_o[...] = -kk
    bonus_o[...] = _head_sums(r * ksum * rk_ref[...], hsum) * v
    gz_o[...] = z * _sigmoid(z)


def _proj_prep(x, g, w, params, vres, tm=PROJ_TILE):
    bsz, T, _ = x.shape
    W = RWKV_WIDTH
    nt = T // tm
    hb = tm // HALO
    nh = T // HALO
    const = lambda a: pl.BlockSpec(a.shape, lambda b, i: (0,) * a.ndim, pipeline_mode=pl.Buffered(1))
    tile = lambda n: pl.BlockSpec((None, tm, n), lambda b, i: (b, i, 0))
    in_specs = [
        tile(D_MODEL),
        pl.BlockSpec((None, HALO, D_MODEL), lambda b, i: (b, jnp.maximum(i * hb - 1, 0), 0)),
        pl.BlockSpec((None, HALO, D_MODEL), lambda b, i: (b, jnp.minimum((i + 1) * hb, nh - 1), 0)),
        const(g), const(w),
    ] + [const(a) for a in params]
    args = [x, x, x, g, w] + list(params)
    if vres is not None:
        vd_pad, vu_split, v0, v_first = vres
        in_specs += [const(vd_pad), const(vu_split), const(v0), tile(W)]
        args += [vd_pad, vu_split, v0, v_first]
    dir_tile = pl.BlockSpec((2, None, tm, W), lambda b, i: (0, b, i, 0))
    out_specs = [tile(B_COLS), tile(G_COLS), tile(W), tile(W), tile(W), dir_tile, dir_tile, dir_tile, tile(W), tile(W)]
    one = jax.ShapeDtypeStruct((bsz, T, W), F32)
    two = jax.ShapeDtypeStruct((2, bsz, T, W), F32)
    out_shape = [jax.ShapeDtypeStruct((bsz, T, B_COLS), F32), jax.ShapeDtypeStruct((bsz, T, G_COLS), BF16),
                 one, one, one, two, two, two, one, one]
    return pl.pallas_call(
        functools.partial(_proj_prep_body, has_vres=vres is not None),
        grid=(bsz, nt),
        in_specs=in_specs,
        out_specs=out_specs,
        out_shape=out_shape,
        scratch_shapes=[pltpu.VMEM((tm + 2 * HALO, D_MODEL), BF16), pltpu.VMEM((tm + 2 * HALO, A_COLS), F32)],
        compiler_params=_cparams(("parallel", "parallel")),
        name="proj_prep",
    )(*args)


def _wkv_step(side_work, rf_ref, vf_ref, af_ref, rb_ref, vb_ref, ab_ref, lwf_ref, kdf_ref, bbf_ref,
              lwb_ref, kdb_ref, bbb_ref, yf_ref, yb_ref, s_ref):
    bsz, C, _ = rf_ref.shape
    G = WKV_GROUP
    GW = G * HEAD
    NG = RWKV_HEADS // G
    assert C == HEAD

    row = lax.broadcasted_iota(jnp.int32, (C, GW), 0)
    lane = lax.broadcasted_iota(jnp.int32, (C, GW), 1)
    col = lane % HEAD
    eye = (row == col).astype(F32)
    same_blk = []
    size = INV_BASE
    while size < C:
        same_blk.append(((row // size) == (col // size)).astype(F32))
        size *= 2
    head_mask = [((lane // HEAD) == g).astype(BF16) for g in range(G)]
    srow = lax.broadcasted_iota(jnp.int32, (GW, GW), 0) // HEAD
    slane = lax.broadcasted_iota(jnp.int32, (GW, GW), 1) // HEAD
    state_mask = (srow == slane).astype(F32)
    trow = lax.broadcasted_iota(jnp.int32, (C, C), 0)
    tcol = lax.broadcasted_iota(jnp.int32, (C, C), 1)
    incl_d = [(col <= row).astype(F32), (col >= row).astype(F32)]
    strict_d = [(col < row).astype(F32), (col > row).astype(F32)]
    tri_d = [(tcol <= trow).astype(BF16), (tcol >= trow).astype(BF16)]

    b16 = lambda x: x.astype(BF16)
    bd = lambda xb: jnp.concatenate([xb * m for m in head_mask], axis=0)
    cat = lambda a, b, axis: jnp.concatenate([a, b], axis=axis)

    dir_refs = [(rf_ref, vf_ref, af_ref, lwf_ref, kdf_ref, bbf_ref, yf_ref),
                (rb_ref, vb_ref, ab_ref, lwb_ref, kdb_ref, bbb_ref, yb_ref)]
    chains = []
    at_g, rt_g, v_g, bt_g, kt_g, bh_g, kh_g, pt_g = [], [], [], [], [], [], [], []
    for d, (r_ref, v_ref, an_ref, lw_ref, kd_ref, bb_ref, _) in enumerate(dir_refs):
        for b in range(bsz):
            lw = lw_ref[b]
            cs = _dot_exact_lhs(tri_d[d], lw)
            tot = jnp.sum(lw, axis=0, keepdims=True)
            e_inv = jnp.exp(-cs)
            p_tot = jnp.exp(tot)
            rt = r_ref[b] * jnp.exp(cs)
            at = an_ref[b] * jnp.exp(cs - lw)
            bt = bb_ref[b] * e_inv
            kt = kd_ref[b] * e_inv
            bh = bt * p_tot
            kh = kt * p_tot
            vv = v_ref[b]
            for g in range(NG):
                sl = slice(g * GW, (g + 1) * GW)
                chains.append((d, b, g))
                for lst, val in ((at_g, at), (rt_g, rt), (v_g, vv), (bt_g, bt), (kt_g, kt), (bh_g, bh), (kh_g, kh),
                                 (pt_g, p_tot)):
                    lst.append(val[:, sl])
    strict = [strict_d[d] for d, _, _ in chains]
    incl = [incl_d[d] for d, _, _ in chains]

    at_b = [b16(a) for a in at_g]
    v_b = [b16(v) for v in v_g]
    ar_b = [cat(a, b16(r), 0) for a, r in zip(at_b, rt_g)]
    sbk = [_dot(ar, cat(bd(b16(xb)), bd(b16(xk)), 0), NT) for ar, xb, xk in zip(ar_b, bt_g, kt_g)]
    sb = [x[:, :GW] for x in sbk]
    sk = [x[:, GW:] for x in sbk]
    m_ab = [x[:C] * m for x, m in zip(sb, strict)]
    n_rb = [x[C:] * m for x, m in zip(sb, incl)]
    m_ak = [x[:C] * m for x, m in zip(sk, strict)]
    n_rk = [x[C:] * m for x, m in zip(sk, incl)]
    v_bd = [bd(v) for v in v_b]
    mv = [_dot(b16(m), vb) for m, vb in zip(m_ak, v_bd)]

    x = [eye + m * same_blk[0] for m in m_ab]
    for lvl in range(len(same_blk)):
        inner = same_blk[lvl]
        outer = same_blk[lvl + 1] if lvl + 1 < len(same_blk) else 1.0
        sel = outer - inner
        if lvl in side_work:
            done = side_work[lvl](_zero_after(x[0][:SUBLANES]))
        x_b = [b16(xi) for xi in x]
        t = [_dot(xb, bd(b16(m * sel))) for xb, m in zip(x_b, m_ab)]
        x = [xi + _dot(b16(ti), bd(xb)) for xi, ti, xb in zip(x, t, x_b)]
        if lvl in side_work:
            hold = jnp.tile(done, (C // SUBLANES, 1))
            x = [xi + hold for xi in x]

    w_ab = [_dot(b16(xi), cat(bd(a), bd(b16(m)), 1)) for xi, a, m in zip(x, at_b, mv)]
    w_a_b = [b16(w[:, :GW]) for w in w_ab]
    w_b_b = [b16(w[:, GW:]) for w in w_ab]
    zero_bd = jnp.zeros((GW, GW), BF16)
    yqc = [_dot(cat(b16(n), b16(nk), 1), cat(cat(bd(wa), bd(wb), 1), cat(zero_bd, vb, 1), 0))
           for n, nk, wa, wb, vb in zip(n_rb, n_rk, w_a_b, w_b_b, v_bd)]
    y_q = [r + x[:, :GW] for r, x in zip(rt_g, yqc)]
    y_c = [x[:, GW:] for x in yqc]
    zero_c = jnp.zeros((C, GW), BF16)
    php = [_dot(cat(cat(wa, wb, 1), cat(zero_c, v, 1), 0), cat(b16(b), b16(k), 0), TN)
           for wa, wb, v, b, k in zip(w_a_b, w_b_b, v_b, bh_g, kh_g)]
    phi = [x[:GW] * state_mask for x in php]
    psi = [x[GW:] * state_mask for x in php]
    for i, (d, b, g) in enumerate(chains):
        s0 = s_ref[d, b, g]
        s0_b = b16(s0)
        dir_refs[d][-1][b, :, g * GW:(g + 1) * GW] = _dot(b16(y_q[i]), s0_b, NT) + y_c[i]
        s_ref[d, b, g] = s0 * pt_g[i] + _dot(s0_b, b16(phi[i])) + psi[i]


def _attn_bias_init(slope_ref, bias_ref):
    rows2 = 2 * BLOCK
    qi = lax.broadcasted_iota(jnp.int32, (rows2, 3 * BLOCK), 0) % BLOCK
    kpos = lax.broadcasted_iota(jnp.int32, (rows2, 3 * BLOCK), 1) - BLOCK
    upper = lax.broadcasted_iota(jnp.int32, (rows2, 3 * BLOCK), 0) >= BLOCK
    dist = jnp.abs(qi - kpos)
    for p in range(ATTN_Q_HEADS // 2):
        slope = jnp.where(upper, slope_ref[2 * p + 1], slope_ref[2 * p])
        bias_ref[p] = jnp.where(dist <= WINDOW, -slope * dist.astype(F32), NEG_INF)


def _attn_step(n, nb, q_ref, kp_ref, kc_ref, kn_ref, qg_ref, kg_ref, slope_ref, sink_ref, o_ref, bias_ref):
    pairs = ATTN_Q_HEADS // 2
    rows2 = 2 * BLOCK
    hq = _head_sum_matrix(ATTN_WIDTH)
    hk = _head_sum_matrix(KV_WIDTH)
    lane = lax.broadcasted_iota(jnp.int32, (BLOCK, LANES), 1)
    low = lane < HEAD
    srow = lax.broadcasted_iota(jnp.int32, (rows2, 1), 0)

    q = q_ref[:, :ATTN_WIDTH]
    q = q * lax.rsqrt(_head_sums(q * q, hq) * (1.0 / HEAD) + RMS_EPS) * (qg_ref[...] * (HEAD ** -0.5))

    def dup(x, g):
        rolled = pltpu.roll(x, HEAD, axis=1)
        return (jnp.where(low, x, rolled) if g == 0 else jnp.where(low, rolled, x)).astype(BF16)

    k_dup, v_dup = [[], []], [[], []]
    for ref in (kp_ref, kc_ref, kn_ref):
        kx = ref[:, :KV_WIDTH]
        kx = kx * lax.rsqrt(_head_sums(kx * kx, hk) * (1.0 / HEAD) + RMS_EPS) * kg_ref[...]
        vx = ref[:, KV_WIDTH:]
        for g in range(ATTN_KV_HEADS):
            k_dup[g].append(dup(kx, g))
            v_dup[g].append(dup(vx, g))
    k_cat = [jnp.concatenate(k_dup[g], axis=0) for g in range(ATTN_KV_HEADS)]
    v_cat = [jnp.concatenate(v_dup[g], axis=0) for g in range(ATTN_KV_HEADS)]
    edge_prev = jnp.where(n > 0, 0.0, NEG_INF)
    edge_next = jnp.where(n < nb - 1, 0.0, NEG_INF)

    def head_pair(p, start):
        g = (2 * p) // ATTN_GROUP
        qp = q[:, p * LANES:(p + 1) * LANES] + jnp.tile(start, (BLOCK // SUBLANES, 1))
        qs = jnp.concatenate([jnp.where(low, qp, 0.0), jnp.where(low, 0.0, qp)], axis=0).astype(BF16)
        s = _dot(qs, k_cat[g], NT) + bias_ref[p]
        s0 = s[:, :BLOCK] + edge_prev
        s1 = s[:, BLOCK:2 * BLOCK]
        s2 = s[:, 2 * BLOCK:] + edge_next
        sink = jnp.where(srow >= BLOCK, sink_ref[2 * p + 1], sink_ref[2 * p])
        m = jnp.maximum(jnp.max(jnp.maximum(jnp.maximum(s0, s1), s2), axis=-1, keepdims=True), sink)
        e0, e1, e2 = jnp.exp(s0 - m), jnp.exp(s1 - m), jnp.exp(s2 - m)
        den = jnp.sum(e0 + e1 + e2, axis=-1, keepdims=True) + jnp.exp(sink - m)
        e = jnp.concatenate([e0, e1, e2], axis=1).astype(BF16)
        o2 = _dot(e, v_cat[g]) * (1.0 / den)
        o = jnp.where(low, o2[:BLOCK], o2[BLOCK:])
        z_lo = ATTN_WIDTH + 2 * KV_WIDTH + p * LANES
        zp = q_ref[:, z_lo:z_lo + LANES]
        og = o * (zp * _sigmoid(zp))
        o_ref[:, p * LANES:(p + 1) * LANES] = og.astype(o_ref.dtype)
        return _zero_after(og[:SUBLANES])

    return head_pair


N_WKV_IN = 12
N_ATTN_IN = 8
ATTN_JOIN_LEVELS = (1, 2, 3, 4)


def _wkv_attn_body(*refs, nb):
    wkv_in = refs[:N_WKV_IN]
    attn_in = refs[N_WKV_IN:N_WKV_IN + N_ATTN_IN]
    yf_ref, yb_ref, o_ref, s_ref, bias_ref = refs[N_WKV_IN + N_ATTN_IN:]
    c = pl.program_id(0)

    @pl.when(c == 0)
    def _():
        s_ref[...] = jnp.zeros_like(s_ref)
        _attn_bias_init(attn_in[6], bias_ref)

    head_pair = _attn_step(c % nb, nb, *attn_in, o_ref, bias_ref)
    side_work = {lvl: functools.partial(head_pair, p) for p, lvl in enumerate(ATTN_JOIN_LEVELS)}
    _wkv_step(side_work, *wkv_in, yf_ref, yb_ref, s_ref)


def _wkv_attn(r, v, an, lw, kd, bb, ub, q_g, k_g, slopes, sink):
    bsz, T, W = r.shape
    C = CHUNK
    nc = T // C
    nb = T // BLOCK
    assert nc == bsz * nb
    gw = WKV_GROUP * HEAD
    fwd = pl.BlockSpec((bsz, C, W), lambda c: (0, c, 0))
    bwd = pl.BlockSpec((bsz, C, W), lambda c: (0, nc - 1 - c, 0))
    fwd_d = pl.BlockSpec((None, bsz, C, W), lambda c: (0, 0, c, 0))
    bwd_d = pl.BlockSpec((None, bsz, C, W), lambda c: (1, 0, nc - 1 - c, 0))
    kv_blk = ATTN_WIDTH // (2 * KV_WIDTH)
    kv = lambda f: pl.BlockSpec((None, BLOCK, 2 * KV_WIDTH), f)
    smem = pl.BlockSpec(memory_space=pltpu.SMEM)
    attn_specs = [
        pl.BlockSpec((None, BLOCK, B_COLS), lambda c: (c // nb, c % nb, 0)),
        kv(lambda c: (c // nb, jnp.maximum(c % nb - 1, 0), kv_blk)),
        kv(lambda c: (c // nb, c % nb, kv_blk)),
        kv(lambda c: (c // nb, jnp.minimum(c % nb + 1, nb - 1), kv_blk)),
        pl.BlockSpec((1, ATTN_WIDTH), lambda c: (0, 0)),
        pl.BlockSpec((1, KV_WIDTH), lambda c: (0, 0)),
        smem, smem,
    ]
    out = jax.ShapeDtypeStruct((bsz, T, W), F32)
    return pl.pallas_call(
        functools.partial(_wkv_attn_body, nb=nb),
        grid=(nc,),
        in_specs=[fwd, fwd, fwd, bwd, bwd, bwd, fwd_d, fwd_d, fwd_d, bwd_d, bwd_d, bwd_d] + attn_specs,
        out_specs=[fwd, bwd, pl.BlockSpec((None, BLOCK, ATTN_WIDTH), lambda c: (c // nb, c % nb, 0))],
        out_shape=[out, out, jax.ShapeDtypeStruct((bsz, T, ATTN_WIDTH), BF16)],
        scratch_shapes=[pltpu.VMEM((2, bsz, RWKV_HEADS // WKV_GROUP, gw, gw), F32),
                        pltpu.VMEM((ATTN_Q_HEADS // 2, 2 * BLOCK, 3 * BLOCK), F32)],
        compiler_params=_cparams(("arbitrary",)),
        name="wkv_attn",
    )(r, v, an, r, v, an, lw, kd, bb, lw, kd, bb, ub, ub, ub, ub, q_g, k_g, slopes, sink)


def _merge_out_body(x_ref, yf_ref, yb_ref, bonus_ref, gz_ref, ob_ref, ug_ref, p_ref, lnw_ref, lnb_ref,
                    pa_ref, pb_ref, wo_ref, pg_ref, gw_ref, pp_ref, o_ref):
    hsum = _head_sum_matrix(RWKV_WIDTH)
    y = yf_ref[...] + yb_ref[...]
    mean = _dot_split2_rhs(y, hsum) * (1.0 / HEAD)
    yc = y - mean
    var = _head_sums(yc * yc, hsum) * (1.0 / HEAD)
    yn = yc * lax.rsqrt(var + GN_EPS) * lnw_ref[...] + lnb_ref[...]
    o_a = (yn + bonus_ref[...]) * gz_ref[...]
    y_a = _dot(o_a.astype(BF16), pa_ref[...])
    y_b = _dot(ob_ref[...], pb_ref[...])
    merged = (_sigmoid(ug_ref[:, :D_MODEL].astype(F32)) * y_a
              + _sigmoid(ug_ref[:, D_MODEL:].astype(F32)) * y_b)
    x1 = x_ref[...] + _dot(merged.astype(BF16), wo_ref[...])
    ple = _dot(p_ref[...].astype(BF16), pp_ref[...])
    hn = x1 * lax.rsqrt(jnp.mean(x1 * x1, axis=-1, keepdims=True) + RMS_EPS) * pg_ref[...]
    o_ref[...] = x1 + _sigmoid(_dot(hn.astype(BF16), gw_ref[...])) * ple


def _merge_out(x2, yf, yb, bonus, gz, ob, ug, p3, layer, lnw, lnb, pa, pb, wo, pg, gw, pp, tm=MERGE_TILE):
    rows = x2.shape[0]
    W = RWKV_WIDTH
    tile = lambda n: pl.BlockSpec((tm, n), lambda i: (i, 0))
    full = lambda a: pl.BlockSpec(a.shape, lambda i: (0,) * a.ndim)
    return pl.pallas_call(
        _merge_out_body,
        grid=(rows // tm,),
        in_specs=[tile(D_MODEL), tile(W), tile(W), tile(W), tile(W), tile(ATTN_WIDTH),
                  tile(G_COLS), pl.BlockSpec((None, tm, PLE_DIM), lambda i: (layer, i, 0)), full(lnw), full(lnb), full(pa), full(pb), full(wo), full(pg),
                  full(gw), full(pp)],
        out_specs=tile(D_MODEL),
        out_shape=jax.ShapeDtypeStruct((rows, D_MODEL), F32),
        compiler_params=_cparams(("parallel",)),
        name="merge_out",
    )(x2, yf, yb, bonus, gz, ob, ug, p3, lnw, lnb, pa, pb, wo, pg, gw, pp)


def _block_diag2(m):
    z = jnp.zeros_like(m[0])
    return jnp.concatenate([jnp.concatenate([m[0], z], axis=1), jnp.concatenate([z, m[1]], axis=1)], axis=0)


def kernel(x, p, norm_g, w_in, shift_mu, decay_w0, decay_up, iclr_a0, iclr_up, vres_down, vres_up, vres_v0, k_k, k_a, r_k, ln_x_w, ln_x_b, q_norm_g, k_norm_g, sink, proj_a, proj_b, w_out, ple_norm_g, ple_gate_w, ple_proj):
    bsz, T, _ = x.shape
    depth = w_in.shape[0]
    rows = bsz * T
    W = RWKV_WIDTH
    slopes = jnp.asarray(2.0 ** (-8.0 * jnp.arange(1, ATTN_Q_HEADS + 1, dtype=F32) / ATTN_Q_HEADS), F32)
    v_first = None
    for i in range(depth):
        params = [shift_mu[i].reshape(1, A_COLS), decay_w0[i].reshape(1, 2 * W), _block_diag2(decay_up[i]).astype(BF16),
                  iclr_a0[i].reshape(1, 2 * W), _block_diag2(iclr_up[i]).astype(BF16), k_k[i].reshape(1, W),
                  k_a[i].reshape(1, W), r_k[i].reshape(1, W)]
        vres = None
        if i > 0:
            vu_pad = jnp.concatenate([vres_up[i - 1], jnp.zeros((LANES - VRES_RANK, W), F32)], axis=0)
            vd_pad = jnp.concatenate([vres_down[i - 1], jnp.zeros((D_MODEL, LANES - VRES_RANK), F32)], axis=1)
            vres = (vd_pad.astype(BF16), vu_pad.astype(BF16), vres_v0[i - 1].reshape(1, W), v_first)
        ub, ug, r, v, an, lw, kd, bb, bonus, gz = _proj_prep(x, norm_g[i].reshape(1, D_MODEL), w_in[i].astype(BF16),
                                                             params, vres)
        if i == 0:
            v_first = v
        yf, yb, ob = _wkv_attn(r, v, an, lw, kd, bb, ub, jnp.tile(q_norm_g[i], ATTN_Q_HEADS).reshape(1, ATTN_WIDTH),
                               jnp.tile(k_norm_g[i], ATTN_KV_HEADS).reshape(1, KV_WIDTH), slopes, sink[i])
        x2 = _merge_out(x.reshape(rows, D_MODEL), yf.reshape(rows, W), yb.reshape(rows, W), bonus.reshape(rows, W),
                        gz.reshape(rows, W), ob.reshape(rows, ATTN_WIDTH), ug.reshape(rows, G_COLS),
                        p.reshape(depth, rows, PLE_DIM), i, ln_x_w[i].reshape(1, W), ln_x_b[i].reshape(1, W),
                        proj_a[i].astype(BF16), proj_b[i].astype(BF16), w_out[i].astype(BF16),
                        ple_norm_g[i].reshape(1, D_MODEL), ple_gate_w[i].astype(BF16), ple_proj[i].astype(BF16))
        x = x2.reshape(bsz, T, D_MODEL)
    return x
```

```python
import functools

import jax
import jax.numpy as jnp
from jax import lax
from jax.experimental import pallas as pl
from jax.experimental.pallas import tpu as pltpu

F32 = jnp.float32
BF16 = jnp.bfloat16

D_MODEL = 1024
PLE_DIM = 256
RWKV_WIDTH = 512
HEAD = 64
RWKV_HEADS = RWKV_WIDTH // HEAD
LOW_RANK = 64
VRES_RANK = 32
ATTN_WIDTH = 512
ATTN_Q_HEADS = ATTN_WIDTH // HEAD
ATTN_KV_HEADS = 2
ATTN_GROUP = ATTN_Q_HEADS // ATTN_KV_HEADS
KV_WIDTH = ATTN_KV_HEADS * HEAD
WINDOW = 128
BLOCK = 128
RMS_EPS = 1e-6
GN_EPS = 64e-5
NEG_INF = -1e30
EXP_MINUS_HALF = 0.6065306597126334
LOG2_E = 1.4426950408889634
A_COLS = 4 * RWKV_WIDTH + 4 * LOW_RANK
B_COLS = 2 * ATTN_WIDTH + 2 * KV_WIDTH
G_COLS = 2 * D_MODEL
LANES = 128
SUBLANES = 8
HALO = 16
CHUNK = 64
INV_BASE = 2
WKV_GROUP = 2
PROJ_TILE = 256
MERGE_TILE = 512
VMEM_LIMIT = 48 * 1024 * 1024

NN = ((1,), (0,))
NT = ((1,), (1,))
TN = ((0,), (0,))


def _dot(a, b, dims=NN):
    return lax.dot_general(a, b, (dims, ((), ())), preferred_element_type=F32)


def _split2(x):
    hi = x.astype(BF16)
    lo = (x - hi.astype(F32)).astype(BF16)
    return hi, lo


def _dot_exact_lhs(a_bf16, b, dims=NN):
    b1 = b.astype(BF16)
    r1 = b - b1.astype(F32)
    b2 = r1.astype(BF16)
    b3 = (r1 - b2.astype(F32)).astype(BF16)
    return _dot(a_bf16, b1, dims) + (_dot(a_bf16, b2, dims) + _dot(a_bf16, b3, dims))


def _dot_split2_rhs(a, b_bf16):
    hi, lo = _split2(a)
    return _dot(hi, b_bf16) + _dot(lo, b_bf16)


def _head_sums(a, hsum_bf16):
    return _dot(a.astype(BF16), hsum_bf16)


def _zero_after(x):
    bits = lax.bitcast_convert_type(x, jnp.int32)
    half = jnp.int32(16)
    return lax.shift_right_logical(lax.shift_right_logical(bits, half), half).astype(F32)


def _sigmoid(x):
    return 1.0 / (1.0 + jnp.exp(-x))


def _head_sum_matrix(width):
    r = lax.broadcasted_iota(jnp.int32, (width, width), 0) // HEAD
    c = lax.broadcasted_iota(jnp.int32, (width, width), 1) // HEAD
    return (r == c).astype(BF16)


def _cparams(sem):
    return pltpu.CompilerParams(dimension_semantics=sem, vmem_limit_bytes=VMEM_LIMIT)


def _proj_prep_body(*refs, has_vres):
    if has_vres:
        (x_ref, xp_ref, xn_ref, g_ref, w_ref, mu_ref, w0_ref, dup_ref, a0_ref, iup_ref, kk_ref, ka_ref,
         rk_ref, vd_ref, vu_ref, v0_ref, vf_ref,
         ub_o, ug_o, r_o, v_o, an_o, lw_o, kd_o, bb_o, bonus_o, gz_o, hb_ref, ua_ref) = refs
    else:
        (x_ref, xp_ref, xn_ref, g_ref, w_ref, mu_ref, w0_ref, dup_ref, a0_ref, iup_ref, kk_ref, ka_ref,
         rk_ref,
         ub_o, ug_o, r_o, v_o, an_o, lw_o, kd_o, bb_o, bonus_o, gz_o, hb_ref, ua_ref) = refs
    tm = x_ref.shape[0]
    i = pl.program_id(1)
    last = pl.num_programs(1) - 1
    W = RWKV_WIDTH

    def normed(ref):
        x = ref[...]
        return (x * lax.rsqrt(jnp.mean(x * x, axis=-1, keepdims=True) + RMS_EPS) * g_ref[...]).astype(BF16)

    hb_ref[0:HALO] = jnp.where(i > 0, normed(xp_ref), jnp.zeros((), BF16))
    hb_ref[HALO:HALO + tm] = normed(x_ref)
    hb_ref[HALO + tm:] = jnp.where(i < last, normed(xn_ref), jnp.zeros((), BF16))
    ua_ref[...] = _dot(hb_ref[...], w_ref[:, :A_COLS])
    hm = hb_ref[HALO:HALO + tm]

    def project_b(lo, hi):
        ub_o[:, lo:hi] = _dot(hm, w_ref[:, A_COLS + lo:A_COLS + hi])

    def project_g(lo, hi):
        ug_o[:, lo:hi] = _dot(hm, w_ref[:, A_COLS + B_COLS + lo:A_COLS + B_COLS + hi]).astype(ug_o.dtype)

    hsum = _head_sum_matrix(W)

    def shifted(lo, hi):
        u = ua_ref[HALO:HALO + tm, lo:hi]
        prev = ua_ref[HALO - 1:HALO - 1 + tm, lo:hi]
        nxt = ua_ref[HALO + 1:HALO + 1 + tm, lo:hi]
        return u + mu_ref[:, lo:hi] * (0.5 * (prev + nxt) - u)

    r = shifted(0, W)
    project_b(0, ATTN_WIDTH)
    k = shifted(W, 2 * W)
    project_b(ATTN_WIDTH, B_COLS)
    v = shifted(2 * W, 3 * W)
    project_g(0, D_MODEL // 2)
    z = shifted(3 * W, 4 * W)
    project_g(D_MODEL // 2, D_MODEL)
    low = shifted(4 * W, 4 * W + 4 * LOW_RANK)
    w_raw = w0_ref[...] + _dot(jnp.tanh(low[:, :2 * LOW_RANK]).astype(BF16), dup_ref[...])
    project_g(D_MODEL, D_MODEL + D_MODEL // 2)
    lw = -(EXP_MINUS_HALF * LOG2_E) * _sigmoid(w_raw)
    a = _sigmoid(a0_ref[...] + _dot(low[:, 2 * LOW_RANK:].astype(BF16), iup_ref[...]))
    if has_vres:
        hd = _dot(hm, vd_ref[...])
        mix = _sigmoid(v0_ref[...] + _dot(hd.astype(BF16), vu_ref[...]))
        v = v + (vf_ref[...] - v) * mix
    project_g(D_MODEL + D_MODEL // 2, G_COLS)
    kk = k * kk_ref[...]
    ss = _head_sums(kk * kk, hsum)
    kk = kk * lax.rsqrt(jnp.maximum(ss, 1e-24))
    ka = ka_ref[...]
    ksum = jnp.zeros_like(k)
    for d in range(2):
        a_d = a[:, d * W:(d + 1) * W]
        kd = k * (1.0 + (a_d - 1.0) * ka)
        ksum = ksum + kd
        lw_o[d] = lw[:, d * W:(d + 1) * W]
        kd_o[d] = kd
        bb_o[d] = kk * a_d
    r_o[...] = r
    v_o[...] = v
    an_o[...] = -kk
    bonus_o[...] = _head_sums(r * ksum * rk_ref[...], hsum) * v
    gz_o[...] = z * _sigmoid(z)


def _proj_prep(x, g, w, params, vres, tm=PROJ_TILE):
    bsz, T, _ = x.shape
    W = RWKV_WIDTH
    nt = T // tm
    hb = tm // HALO
    nh = T // HALO
    const = lambda a: pl.BlockSpec(a.shape, lambda b, i: (0,) * a.ndim, pipeline_mode=pl.Buffered(1))
    tile = lambda n: pl.BlockSpec((None, tm, n), lambda b, i: (b, i, 0))
    in_specs = [
        tile(D_MODEL),
        pl.BlockSpec((None, HALO, D_MODEL), lambda b, i: (b, jnp.maximum(i * hb - 1, 0), 0)),
        pl.BlockSpec((None, HALO, D_MODEL), lambda b, i: (b, jnp.minimum((i + 1) * hb, nh - 1), 0)),
        const(g), const(w),
    ] + [const(a) for a in params]
    args = [x, x, x, g, w] + list(params)
    if vres is not None:
        vd_pad, vu_split, v0, v_first = vres
        in_specs += [const(vd_pad), const(vu_split), const(v0), tile(W)]
        args += [vd_pad, vu_split, v0, v_first]
    dir_tile = pl.BlockSpec((2, None, tm, W), lambda b, i: (0, b, i, 0))
    out_specs = [tile(B_COLS), tile(G_COLS), tile(W), tile(W), tile(W), dir_tile, dir_tile, dir_tile, tile(W), tile(W)]
    one = jax.ShapeDtypeStruct((bsz, T, W), F32)
    two = jax.ShapeDtypeStruct((2, bsz, T, W), F32)
    out_shape = [jax.ShapeDtypeStruct((bsz, T, B_COLS), F32), jax.ShapeDtypeStruct((bsz, T, G_COLS), BF16),
                 one, one, one, two, two, two, one, one]
    return pl.pallas_call(
        functools.partial(_proj_prep_body, has_vres=vres is not None),
        grid=(bsz, nt),
        in_specs=in_specs,
        out_specs=out_specs,
        out_shape=out_shape,
        scratch_shapes=[pltpu.VMEM((tm + 2 * HALO, D_MODEL), BF16), pltpu.VMEM((tm + 2 * HALO, A_COLS), F32)],
        compiler_params=_cparams(("parallel", "parallel")),
        name="proj_prep",
    )(*args)


def _wkv_step(side_work, rf_ref, vf_ref, af_ref, rb_ref, vb_ref, ab_ref, lwf_ref, kdf_ref, bbf_ref,
              lwb_ref, kdb_ref, bbb_ref, yf_ref, yb_ref, s_ref):
    bsz, C, _ = rf_ref.shape
    G = WKV_GROUP
    GW = G * HEAD
    NG = RWKV_HEADS // G
    assert C == HEAD

    row = lax.broadcasted_iota(jnp.int32, (C, GW), 0)
    lane = lax.broadcasted_iota(jnp.int32, (C, GW), 1)
    col = lane % HEAD
    eye = (row == col).astype(F32)
    same_blk = []
    size = INV_BASE
    while size < C:
        same_blk.append(((row // size) == (col // size)).astype(F32))
        size *= 2
    head_mask = [((lane // HEAD) == g).astype(BF16) for g in range(G)]
    srow = lax.broadcasted_iota(jnp.int32, (GW, GW), 0) // HEAD
    slane = lax.broadcasted_iota(jnp.int32, (GW, GW), 1) // HEAD
    state_mask = (srow == slane).astype(F32)
    trow = lax.broadcasted_iota(jnp.int32, (C, C), 0)
    tcol = lax.broadcasted_iota(jnp.int32, (C, C), 1)
    incl_d = [(col <= row).astype(F32), (col >= row).astype(F32)]
    strict_d = [(col < row).astype(F32), (col > row).astype(F32)]
    tri_d = [(tcol <= trow).astype(BF16), (tcol >= trow).astype(BF16)]

    b16 = lambda x: x.astype(BF16)
    bd = lambda xb: jnp.concatenate([xb * m for m in head_mask], axis=0)
    cat = lambda a, b, axis: jnp.concatenate([a, b], axis=axis)

    dir_refs = [(rf_ref, vf_ref, af_ref, lwf_ref, kdf_ref, bbf_ref, yf_ref),
                (rb_ref, vb_ref, ab_ref, lwb_ref, kdb_ref, bbb_ref, yb_ref)]
    chains = []
    at_g, rt_g, v_g, bt_g, kt_g, bh_g, kh_g, pt_g = [], [], [], [], [], [], [], []
    for d, (r_ref, v_ref, an_ref, lw_ref, kd_ref, bb_ref, _) in enumerate(dir_refs):
        for b in range(bsz):
            lw = lw_ref[b]
            cs = _dot_exact_lhs(tri_d[d], lw)
            tot = jnp.sum(lw, axis=0, keepdims=True)
            e_inv = jnp.exp2(-cs)
            p_tot = jnp.exp2(tot)
            rt = r_ref[b] * jnp.exp2(cs)
            at = an_ref[b] * jnp.exp2(cs - lw)
            bt = bb_ref[b] * e_inv
            kt = kd_ref[b] * e_inv
            bh = bt * p_tot
            kh = kt * p_tot
            vv = v_ref[b]
            for g in range(NG):
                sl = slice(g * GW, (g + 1) * GW)
                chains.append((d, b, g))
                for lst, val in ((at_g, at), (rt_g, rt), (v_g, vv), (bt_g, bt), (kt_g, kt), (bh_g, bh), (kh_g, kh),
                                 (pt_g, p_tot)):
                    lst.append(val[:, sl])
    strict = [strict_d[d] for d, _, _ in chains]
    incl = [incl_d[d] for d, _, _ in chains]

    at_b = [b16(a) for a in at_g]
    v_b = [b16(v) for v in v_g]
    ar_b = [cat(a, b16(r), 0) for a, r in zip(at_b, rt_g)]
    sbk = [_dot(ar, cat(bd(b16(xb)), bd(b16(xk)), 0), NT) for ar, xb, xk in zip(ar_b, bt_g, kt_g)]
    sb = [x[:, :GW] for x in sbk]
    sk = [x[:, GW:] for x in sbk]
    m_ab = [x[:C] * m for x, m in zip(sb, strict)]
    n_rb = [x[C:] * m for x, m in zip(sb, incl)]
    m_ak = [x[:C] * m for x, m in zip(sk, strict)]
    n_rk = [x[C:] * m for x, m in zip(sk, incl)]
    v_bd = [bd(v) for v in v_b]
    mv = [_dot(b16(m), vb) for m, vb in zip(m_ak, v_bd)]

    x = [eye + m * same_blk[0] for m in m_ab]
    m_ab_b = [b16(m) for m in m_ab]
    for lvl in range(len(same_blk)):
        inner = same_blk[lvl]
        outer = same_blk[lvl + 1] if lvl + 1 < len(same_blk) else 1.0
        lvl_mask = [hm * b16(outer - inner) for hm in head_mask]
        if lvl in side_work:
            done = side_work[lvl](_zero_after(x[0][:SUBLANES]))
        x_b = [b16(xi) for xi in x]
        t = [_dot(xb, jnp.concatenate([mb * lm for lm in lvl_mask], axis=0)) for xb, mb in zip(x_b, m_ab_b)]
        x = [xi + _dot(b16(ti), bd(xb)) for xi, ti, xb in zip(x, t, x_b)]
        if lvl in side_work:
            hold = jnp.tile(done, (C // SUBLANES, 1))
            x = [xi + hold for xi in x]

    w_ab = [_dot(b16(xi), cat(bd(a), bd(b16(m)), 1)) for xi, a, m in zip(x, at_b, mv)]
    w_a_b = [b16(w[:, :GW]) for w in w_ab]
    w_b_b = [b16(w[:, GW:]) for w in w_ab]
    zero_bd = jnp.zeros((GW, GW), BF16)
    yqc = [_dot(cat(b16(n), b16(nk), 1), cat(cat(bd(wa), bd(wb), 1), cat(zero_bd, vb, 1), 0))
           for n, nk, wa, wb, vb in zip(n_rb, n_rk, w_a_b, w_b_b, v_bd)]
    y_q = [r + x[:, :GW] for r, x in zip(rt_g, yqc)]
    y_c = [x[:, GW:] for x in yqc]
    zero_c = jnp.zeros((C, GW), BF16)
    php = [_dot(cat(cat(wa, wb, 1), cat(zero_c, v, 1), 0), cat(b16(b), b16(k), 0), TN)
           for wa, wb, v, b, k in zip(w_a_b, w_b_b, v_b, bh_g, kh_g)]
    phi = [x[:GW] for x in php]
    psi = [x[GW:] for x in php]
    for i, (d, b, g) in enumerate(chains):
        s0 = s_ref[d, b, g]
        s0_b = b16(s0)
        dir_refs[d][-1][b, :, g * GW:(g + 1) * GW] = _dot(b16(y_q[i]), s0_b, NT) + y_c[i]
        s_ref[d, b, g] = (s0 * pt_g[i] + _dot(s0_b, b16(phi[i])) + psi[i]) * state_mask


def _attn_bias_init(slope_ref, bias_ref):
    rows2 = 2 * BLOCK
    qi = lax.broadcasted_iota(jnp.int32, (rows2, 3 * BLOCK), 0) % BLOCK
    kpos = lax.broadcasted_iota(jnp.int32, (rows2, 3 * BLOCK), 1) - BLOCK
    upper = lax.broadcasted_iota(jnp.int32, (rows2, 3 * BLOCK), 0) >= BLOCK
    dist = jnp.abs(qi - kpos)
    for p in range(ATTN_Q_HEADS // 2):
        slope = jnp.where(upper, slope_ref[2 * p + 1], slope_ref[2 * p])
        bias_ref[p] = jnp.where(dist <= WINDOW, (-LOG2_E) * slope * dist.astype(F32), NEG_INF)


def _attn_step(n, nb, q_ref, kp_ref, kc_ref, kn_ref, qg_ref, kg_ref, slope_ref, sink_ref, o_ref, bias_ref):
    pairs = ATTN_Q_HEADS // 2
    rows2 = 2 * BLOCK
    hq = _head_sum_matrix(ATTN_WIDTH)
    hk = _head_sum_matrix(KV_WIDTH)
    lane = lax.broadcasted_iota(jnp.int32, (BLOCK, LANES), 1)
    low = lane < HEAD
    srow = lax.broadcasted_iota(jnp.int32, (rows2, 1), 0)

    q = q_ref[:, :ATTN_WIDTH]
    q = q * lax.rsqrt(_head_sums(q * q, hq) * (1.0 / HEAD) + RMS_EPS) * (qg_ref[...] * (HEAD ** -0.5 * LOG2_E))

    def dup(x, g):
        rolled = pltpu.roll(x, HEAD, axis=1)
        return (jnp.where(low, x, rolled) if g == 0 else jnp.where(low, rolled, x)).astype(BF16)

    k_dup, v_dup = [[], []], [[], []]
    for ref in (kp_ref, kc_ref, kn_ref):
        kx = ref[:, :KV_WIDTH]
        kx = kx * lax.rsqrt(_head_sums(kx * kx, hk) * (1.0 / HEAD) + RMS_EPS) * kg_ref[...]
        vx = ref[:, KV_WIDTH:]
        for g in range(ATTN_KV_HEADS):
            k_dup[g].append(dup(kx, g))
            v_dup[g].append(dup(vx, g))
    k_cat = [jnp.concatenate(k_dup[g], axis=0) for g in range(ATTN_KV_HEADS)]
    v_cat = [jnp.concatenate(v_dup[g], axis=0) for g in range(ATTN_KV_HEADS)]
    edge_prev = jnp.where(n > 0, 0.0, NEG_INF)
    edge_next = jnp.where(n < nb - 1, 0.0, NEG_INF)

    def head_pair(p, start):
        g = (2 * p) // ATTN_GROUP
        qp = q[:, p * LANES:(p + 1) * LANES] + jnp.tile(start, (BLOCK // SUBLANES, 1))
        qs = jnp.concatenate([jnp.where(low, qp, 0.0), jnp.where(low, 0.0, qp)], axis=0).astype(BF16)
        s = _dot(qs, k_cat[g], NT) + bias_ref[p]
        s0 = s[:, :BLOCK] + edge_prev
        s1 = s[:, BLOCK:2 * BLOCK]
        s2 = s[:, 2 * BLOCK:] + edge_next
        sink = jnp.where(srow >= BLOCK, sink_ref[2 * p + 1], sink_ref[2 * p]) * LOG2_E
        m = jnp.maximum(jnp.max(jnp.maximum(jnp.maximum(s0, s1), s2), axis=-1, keepdims=True), sink)
        e0, e1, e2 = jnp.exp2(s0 - m), jnp.exp2(s1 - m), jnp.exp2(s2 - m)
        den = jnp.sum(e0 + e1 + e2, axis=-1, keepdims=True) + jnp.exp2(sink - m)
        e = jnp.concatenate([e0, e1, e2], axis=1).astype(BF16)
        o2 = _dot(e, v_cat[g]) * (1.0 / den)
        o = jnp.where(low, o2[:BLOCK], o2[BLOCK:])
        z_lo = ATTN_WIDTH + 2 * KV_WIDTH + p * LANES
        zp = q_ref[:, z_lo:z_lo + LANES]
        og = o * (zp * _sigmoid(zp))
        o_ref[:, p * LANES:(p + 1) * LANES] = og.astype(o_ref.dtype)
        return _zero_after(og[:SUBLANES])

    return head_pair


N_WKV_IN = 12
N_ATTN_IN = 8
ATTN_JOIN_LEVELS = (1, 2, 3, 4)


def _wkv_attn_body(*refs, nb):
    wkv_in = refs[:N_WKV_IN]
    attn_in = refs[N_WKV_IN:N_WKV_IN + N_ATTN_IN]
    yf_ref, yb_ref, o_ref, s_ref, bias_ref = refs[N_WKV_IN + N_ATTN_IN:]
    c = pl.program_id(0)

    @pl.when(c == 0)
    def _():
        s_ref[...] = jnp.zeros_like(s_ref)
        _attn_bias_init(attn_in[6], bias_ref)

    head_pair = _attn_step(c % nb, nb, *attn_in, o_ref, bias_ref)
    side_work = {lvl: functools.partial(head_pair, p) for p, lvl in enumerate(ATTN_JOIN_LEVELS)}
    _wkv_step(side_work, *wkv_in, yf_ref, yb_ref, s_ref)


def _wkv_attn(r, v, an, lw, kd, bb, ub, q_g, k_g, slopes, sink):
    bsz, T, W = r.shape
    C = CHUNK
    nc = T // C
    nb = T // BLOCK
    assert nc == bsz * nb
    gw = WKV_GROUP * HEAD
    fwd = pl.BlockSpec((bsz, C, W), lambda c: (0, c, 0))
    bwd = pl.BlockSpec((bsz, C, W), lambda c: (0, nc - 1 - c, 0))
    fwd_d = pl.BlockSpec((None, bsz, C, W), lambda c: (0, 0, c, 0))
    bwd_d = pl.BlockSpec((None, bsz, C, W), lambda c: (1, 0, nc - 1 - c, 0))
    kv_blk = ATTN_WIDTH // (2 * KV_WIDTH)
    kv = lambda f: pl.BlockSpec((None, BLOCK, 2 * KV_WIDTH), f)
    smem = pl.BlockSpec(memory_space=pltpu.SMEM)
    attn_specs = [
        pl.BlockSpec((None, BLOCK, B_COLS), lambda c: (c // nb, c % nb, 0)),
        kv(lambda c: (c // nb, jnp.maximum(c % nb - 1, 0), kv_blk)),
        kv(lambda c: (c // nb, c % nb, kv_blk)),
        kv(lambda c: (c // nb, jnp.minimum(c % nb + 1, nb - 1), kv_blk)),
        pl.BlockSpec((1, ATTN_WIDTH), lambda c: (0, 0)),
        pl.BlockSpec((1, KV_WIDTH), lambda c: (0, 0)),
        smem, smem,
    ]
    out = jax.ShapeDtypeStruct((bsz, T, W), F32)
    return pl.pallas_call(
        functools.partial(_wkv_attn_body, nb=nb),
        grid=(nc,),
        in_specs=[fwd, fwd, fwd, bwd, bwd, bwd, fwd_d, fwd_d, fwd_d, bwd_d, bwd_d, bwd_d] + attn_specs,
        out_specs=[fwd, bwd, pl.BlockSpec((None, BLOCK, ATTN_WIDTH), lambda c: (c // nb, c % nb, 0))],
        out_shape=[out, out, jax.ShapeDtypeStruct((bsz, T, ATTN_WIDTH), BF16)],
        scratch_shapes=[pltpu.VMEM((2, bsz, RWKV_HEADS // WKV_GROUP, gw, gw), F32),
                        pltpu.VMEM((ATTN_Q_HEADS // 2, 2 * BLOCK, 3 * BLOCK), F32)],
        compiler_params=_cparams(("arbitrary",)),
        name="wkv_attn",
    )(r, v, an, r, v, an, lw, kd, bb, lw, kd, bb, ub, ub, ub, ub, q_g, k_g, slopes, sink)


def _merge_out_body(x_ref, yf_ref, yb_ref, bonus_ref, gz_ref, ob_ref, ug_ref, p_ref, lnw_ref, lnb_ref,
                    pa_ref, pb_ref, wo_ref, pg_ref, gw_ref, pp_ref, o_ref):
    hsum = _head_sum_matrix(RWKV_WIDTH)
    y = yf_ref[...] + yb_ref[...]
    mean = _dot_split2_rhs(y, hsum) * (1.0 / HEAD)
    yc = y - mean
    var = _head_sums(yc * yc, hsum) * (1.0 / HEAD)
    yn = yc * lax.rsqrt(var + GN_EPS) * lnw_ref[...] + lnb_ref[...]
    o_a = (yn + bonus_ref[...]) * gz_ref[...]
    y_a = _dot(o_a.astype(BF16), pa_ref[...])
    y_b = _dot(ob_ref[...], pb_ref[...])
    merged = (_sigmoid(ug_ref[:, :D_MODEL].astype(F32)) * y_a
              + _sigmoid(ug_ref[:, D_MODEL:].astype(F32)) * y_b)
    x1 = x_ref[...] + _dot(merged.astype(BF16), wo_ref[...])
    ple = _dot(p_ref[...].astype(BF16), pp_ref[...])
    hn = x1 * lax.rsqrt(jnp.mean(x1 * x1, axis=-1, keepdims=True) + RMS_EPS) * pg_ref[...]
    o_ref[...] = x1 + _sigmoid(_dot(hn.astype(BF16), gw_ref[...])) * ple


def _merge_out(x2, yf, yb, bonus, gz, ob, ug, p3, layer, lnw, lnb, pa, pb, wo, pg, gw, pp, tm=MERGE_TILE):
    rows = x2.shape[0]
    W = RWKV_WIDTH
    tile = lambda n: pl.BlockSpec((tm, n), lambda i: (i, 0))
    full = lambda a: pl.BlockSpec(a.shape, lambda i: (0,) * a.ndim)
    return pl.pallas_call(
        _merge_out_body,
        grid=(rows // tm,),
        in_specs=[tile(D_MODEL), tile(W), tile(W), tile(W), tile(W), tile(ATTN_WIDTH),
                  tile(G_COLS), pl.BlockSpec((None, tm, PLE_DIM), lambda i: (layer, i, 0)), full(lnw), full(lnb), full(pa), full(pb), full(wo), full(pg),
                  full(gw), full(pp)],
        out_specs=tile(D_MODEL),
        out_shape=jax.ShapeDtypeStruct((rows, D_MODEL), F32),
        compiler_params=_cparams(("parallel",)),
        name="merge_out",
    )(x2, yf, yb, bonus, gz, ob, ug, p3, lnw, lnb, pa, pb, wo, pg, gw, pp)


def _block_diag2(m):
    z = jnp.zeros_like(m[0])
    return jnp.concatenate([jnp.concatenate([m[0], z], axis=1), jnp.concatenate([z, m[1]], axis=1)], axis=0)


def kernel(x, p, norm_g, w_in, shift_mu, decay_w0, decay_up, iclr_a0, iclr_up, vres_down, vres_up, vres_v0, k_k, k_a, r_k, ln_x_w, ln_x_b, q_norm_g, k_norm_g, sink, proj_a, proj_b, w_out, ple_norm_g, ple_gate_w, ple_proj):
    bsz, T, _ = x.shape
    depth = w_in.shape[0]
    rows = bsz * T
    W = RWKV_WIDTH
    slopes = jnp.asarray(2.0 ** (-8.0 * jnp.arange(1, ATTN_Q_HEADS + 1, dtype=F32) / ATTN_Q_HEADS), F32)
    v_first = None
    for i in range(depth):
        params = [shift_mu[i].reshape(1, A_COLS), decay_w0[i].reshape(1, 2 * W), _block_diag2(decay_up[i]).astype(BF16),
                  iclr_a0[i].reshape(1, 2 * W), _block_diag2(iclr_up[i]).astype(BF16), k_k[i].reshape(1, W),
                  k_a[i].reshape(1, W), r_k[i].reshape(1, W)]
        vres = None
        if i > 0:
            vu_pad = jnp.concatenate([vres_up[i - 1], jnp.zeros((LANES - VRES_RANK, W), F32)], axis=0)
            vd_pad = jnp.concatenate([vres_down[i - 1], jnp.zeros((D_MODEL, LANES - VRES_RANK), F32)], axis=1)
            vres = (vd_pad.astype(BF16), vu_pad.astype(BF16), vres_v0[i - 1].reshape(1, W), v_first)
        ub, ug, r, v, an, lw, kd, bb, bonus, gz = _proj_prep(x, norm_g[i].reshape(1, D_MODEL), w_in[i].astype(BF16),
                                                             params, vres)
        if i == 0:
            v_first = v
        yf, yb, ob = _wkv_attn(r, v, an, lw, kd, bb, ub, jnp.tile(q_norm_g[i], ATTN_Q_HEADS).reshape(1, ATTN_WIDTH),
                               jnp.tile(k_norm_g[i], ATTN_KV_HEADS).reshape(1, KV_WIDTH), slopes, sink[i])
        x2 = _merge_out(x.reshape(rows, D_MODEL), yf.reshape(rows, W), yb.reshape(rows, W), bonus.reshape(rows, W),
                        gz.reshape(rows, W), ob.reshape(rows, ATTN_WIDTH), ug.reshape(rows, G_COLS),
                        p.reshape(depth, rows, PLE_DIM), i, ln_x_w[i].reshape(1, W), ln_x_b[i].reshape(1, W),
                        proj_a[i].astype(BF16), proj_b[i].astype(BF16), w_out[i].astype(BF16),
                        ple_norm_g[i].reshape(1, D_MODEL), ple_gate_w[i].astype(BF16), ple_proj[i].astype(BF16))
        x = x2.reshape(bsz, T, D_MODEL)
    return x
```

```python
import functools

import jax
import jax.numpy as jnp
from jax import lax
from jax.experimental import pallas as pl
from jax.experimental.pallas import tpu as pltpu

F32 = jnp.float32
BF16 = jnp.bfloat16

D_MODEL = 1024
PLE_DIM = 256
RWKV_WIDTH = 512
HEAD = 64
RWKV_HEADS = RWKV_WIDTH // HEAD
LOW_RANK = 64
VRES_RANK = 32
ATTN_WIDTH = 512
ATTN_Q_HEADS = ATTN_WIDTH // HEAD
ATTN_KV_HEADS = 2
ATTN_GROUP = ATTN_Q_HEADS // ATTN_KV_HEADS
KV_WIDTH = ATTN_KV_HEADS * HEAD
WINDOW = 128
BLOCK = 128
RMS_EPS = 1e-6
GN_EPS = 64e-5
NEG_INF = -1e30
EXP_MINUS_HALF = 0.6065306597126334
A_COLS = 4 * RWKV_WIDTH + 4 * LOW_RANK
B_COLS = 2 * ATTN_WIDTH + 2 * KV_WIDTH
G_COLS = 2 * D_MODEL
LANES = 128
SUBLANES = 8
HALO = 16
CHUNK = 64
INV_BASE = 2
WKV_GROUP = 2
PROJ_TILE = 256
MERGE_TILE = 512
VMEM_LIMIT = 48 * 1024 * 1024

NN = ((1,), (0,))
NT = ((1,), (1,))
TN = ((0,), (0,))


def _dot(a, b, dims=NN):
    return lax.dot_general(a, b, (dims, ((), ())), preferred_element_type=F32)


def _split2(x):
    hi = x.astype(BF16)
    lo = (x - hi.astype(F32)).astype(BF16)
    return hi, lo


def _dot_exact_lhs(a_bf16, b, dims=NN):
    b1 = b.astype(BF16)
    r1 = b - b1.astype(F32)
    b2 = r1.astype(BF16)
    b3 = (r1 - b2.astype(F32)).astype(BF16)
    return _dot(a_bf16, b1, dims) + (_dot(a_bf16, b2, dims) + _dot(a_bf16, b3, dims))


def _dot_split2_rhs(a, b_bf16):
    hi, lo = _split2(a)
    return _dot(hi, b_bf16) + _dot(lo, b_bf16)


def _head_sums(a, hsum_bf16):
    return _dot(a.astype(BF16), hsum_bf16)


def _zero_after(x):
    bits = lax.bitcast_convert_type(x, jnp.int32)
    half = jnp.int32(16)
    return lax.shift_right_logical(lax.shift_right_logical(bits, half), half).astype(F32)


def _sigmoid(x):
    return 1.0 / (1.0 + jnp.exp(-x))


def _head_sum_matrix(width):
    r = lax.broadcasted_iota(jnp.int32, (width, width), 0) // HEAD
    c = lax.broadcasted_iota(jnp.int32, (width, width), 1) // HEAD
    return (r == c).astype(BF16)


def _cparams(sem):
    return pltpu.CompilerParams(dimension_semantics=sem, vmem_limit_bytes=VMEM_LIMIT)


def _proj_prep_body(*refs, has_vres):
    if has_vres:
        (x_ref, xp_ref, xn_ref, g_ref, w_ref, mu_ref, w0_ref, dup_ref, a0_ref, iup_ref, kk_ref, ka_ref,
         rk_ref, vd_ref, vu_ref, v0_ref, vf_ref,
         ub_o, ug_o, r_o, v_o, an_o, lw_o, kd_o, bb_o, bonus_o, gz_o, hb_ref, ua_ref) = refs
    else:
        (x_ref, xp_ref, xn_ref, g_ref, w_ref, mu_ref, w0_ref, dup_ref, a0_ref, iup_ref, kk_ref, ka_ref,
         rk_ref,
         ub_o, ug_o, r_o, v_o, an_o, lw_o, kd_o, bb_o, bonus_o, gz_o, hb_ref, ua_ref) = refs
    tm = x_ref.shape[0]
    i = pl.program_id(1)
    last = pl.num_programs(1) - 1
    W = RWKV_WIDTH

    def normed(ref):
        x = ref[...]
        return (x * lax.rsqrt(jnp.mean(x * x, axis=-1, keepdims=True) + RMS_EPS) * g_ref[...]).astype(BF16)

    hb_ref[0:HALO] = jnp.where(i > 0, normed(xp_ref), jnp.zeros((), BF16))
    hb_ref[HALO:HALO + tm] = normed(x_ref)
    hb_ref[HALO + tm:] = jnp.where(i < last, normed(xn_ref), jnp.zeros((), BF16))
    ua_ref[...] = _dot(hb_ref[...], w_ref[:, :A_COLS])
    hm = hb_ref[HALO:HALO + tm]

    def project_b(lo, hi):
        ub_o[:, lo:hi] = _dot(hm, w_ref[:, A_COLS + lo:A_COLS + hi])

    def project_g(lo, hi):
        ug_o[:, lo:hi] = _dot(hm, w_ref[:, A_COLS + B_COLS + lo:A_COLS + B_COLS + hi]).astype(ug_o.dtype)

    hsum = _head_sum_matrix(W)

    def shifted(lo, hi):
        u = ua_ref[HALO:HALO + tm, lo:hi]
        ext = ua_ref[:, lo:hi]
        prev = pltpu.roll(ext, 1, axis=0)[HALO:HALO + tm]
        nxt = pltpu.roll(ext, tm + 2 * HALO - 1, axis=0)[HALO:HALO + tm]
        return u + mu_ref[:, lo:hi] * (0.5 * (prev + nxt) - u)

    r = shifted(0, W)
    project_b(0, ATTN_WIDTH)
    k = shifted(W, 2 * W)
    project_b(ATTN_WIDTH, B_COLS)
    v = shifted(2 * W, 3 * W)
    project_g(0, D_MODEL // 2)
    z = shifted(3 * W, 4 * W)
    project_g(D_MODEL // 2, D_MODEL)
    low = shifted(4 * W, 4 * W + 4 * LOW_RANK)
    w_raw = w0_ref[...] + _dot(jnp.tanh(low[:, :2 * LOW_RANK]).astype(BF16), dup_ref[...])
    project_g(D_MODEL, D_MODEL + D_MODEL // 2)
    lw = -EXP_MINUS_HALF * _sigmoid(w_raw)
    a = _sigmoid(a0_ref[...] + _dot(low[:, 2 * LOW_RANK:].astype(BF16), iup_ref[...]))
    if has_vres:
        hd = _dot(hm, vd_ref[...])
        mix = _sigmoid(v0_ref[...] + _dot(hd.astype(BF16), vu_ref[...]))
        v = v + (vf_ref[...] - v) * mix
    project_g(D_MODEL + D_MODEL // 2, G_COLS)
    kk = k * kk_ref[...]
    ss = _head_sums(kk * kk, hsum)
    kk = kk * lax.rsqrt(jnp.maximum(ss, 1e-24))
    ka = ka_ref[...]
    ksum = jnp.zeros_like(k)
    for d in range(2):
        a_d = a[:, d * W:(d + 1) * W]
        kd = k * (1.0 + (a_d - 1.0) * ka)
        ksum = ksum + kd
        lw_o[d] = lw[:, d * W:(d + 1) * W]
        kd_o[d] = kd
        bb_o[d] = kk * a_d
    r_o[...] = r
    v_o[...] = v
    an_o[...] = -kk
    bonus_o[...] = _head_sums(r * ksum * rk_ref[...], hsum) * v
    gz_o[...] = z * _sigmoid(z)


def _proj_prep(x, g, w, params, vres, tm=PROJ_TILE):
    bsz, T, _ = x.shape
    W = RWKV_WIDTH
    nt = T // tm
    hb = tm // HALO
    nh = T // HALO
    const = lambda a: pl.BlockSpec(a.shape, lambda b, i: (0,) * a.ndim, pipeline_mode=pl.Buffered(1))
    tile = lambda n: pl.BlockSpec((None, tm, n), lambda b, i: (b, i, 0))
    in_specs = [
        tile(D_MODEL),
        pl.BlockSpec((None, HALO, D_MODEL), lambda b, i: (b, jnp.maximum(i * hb - 1, 0), 0)),
        pl.BlockSpec((None, HALO, D_MODEL), lambda b, i: (b, jnp.minimum((i + 1) * hb, nh - 1), 0)),
        const(g), const(w),
    ] + [const(a) for a in params]
    args = [x, x, x, g, w] + list(params)
    if vres is not None:
        vd_pad, vu_split, v0, v_first = vres
        in_specs += [const(vd_pad), const(vu_split), const(v0), tile(W)]
        args += [vd_pad, vu_split, v0, v_first]
    dir_tile = pl.BlockSpec((2, None, tm, W), lambda b, i: (0, b, i, 0))
    out_specs = [tile(B_COLS), tile(G_COLS), tile(W), tile(W), tile(W), dir_tile, dir_tile, dir_tile, tile(W), tile(W)]
    one = jax.ShapeDtypeStruct((bsz, T, W), F32)
    two = jax.ShapeDtypeStruct((2, bsz, T, W), F32)
    out_shape = [jax.ShapeDtypeStruct((bsz, T, B_COLS), F32), jax.ShapeDtypeStruct((bsz, T, G_COLS), BF16),
                 one, one, one, two, two, two, one, one]
    return pl.pallas_call(
        functools.partial(_proj_prep_body, has_vres=vres is not None),
        grid=(bsz, nt),
        in_specs=in_specs,
        out_specs=out_specs,
        out_shape=out_shape,
        scratch_shapes=[pltpu.VMEM((tm + 2 * HALO, D_MODEL), BF16), pltpu.VMEM((tm + 2 * HALO, A_COLS), F32)],
        compiler_params=_cparams(("parallel", "parallel")),
        name="proj_prep",
    )(*args)


def _wkv_step(side_work, rf_ref, vf_ref, af_ref, rb_ref, vb_ref, ab_ref, lwf_ref, kdf_ref, bbf_ref,
              lwb_ref, kdb_ref, bbb_ref, yf_ref, yb_ref, s_ref):
    bsz, C, _ = rf_ref.shape
    G = WKV_GROUP
    GW = G * HEAD
    NG = RWKV_HEADS // G
    assert C == HEAD

    row = lax.broadcasted_iota(jnp.int32, (C, GW), 0)
    lane = lax.broadcasted_iota(jnp.int32, (C, GW), 1)
    col = lane % HEAD
    eye = (row == col).astype(F32)
    same_blk = []
    size = INV_BASE
    while size < C:
        same_blk.append(((row // size) == (col // size)).astype(F32))
        size *= 2
    head_mask = [((lane // HEAD) == g).astype(BF16) for g in range(G)]
    srow = lax.broadcasted_iota(jnp.int32, (GW, GW), 0) // HEAD
    slane = lax.broadcasted_iota(jnp.int32, (GW, GW), 1) // HEAD
    state_mask = (srow == slane).astype(F32)
    trow = lax.broadcasted_iota(jnp.int32, (C, C), 0)
    tcol = lax.broadcasted_iota(jnp.int32, (C, C), 1)
    incl_d = [(col <= row).astype(F32), (col >= row).astype(F32)]
    strict_d = [(col < row).astype(F32), (col > row).astype(F32)]
    tri_d = [(tcol <= trow).astype(BF16), (tcol >= trow).astype(BF16)]

    b16 = lambda x: x.astype(BF16)
    bd = lambda xb: jnp.concatenate([xb * m for m in head_mask], axis=0)
    cat = lambda a, b, axis: jnp.concatenate([a, b], axis=axis)

    dir_refs = [(rf_ref, vf_ref, af_ref, lwf_ref, kdf_ref, bbf_ref, yf_ref),
                (rb_ref, vb_ref, ab_ref, lwb_ref, kdb_ref, bbb_ref, yb_ref)]
    chains = []
    at_g, rt_g, v_g, bt_g, kt_g, bh_g, kh_g, pt_g = [], [], [], [], [], [], [], []
    for d, (r_ref, v_ref, an_ref, lw_ref, kd_ref, bb_ref, _) in enumerate(dir_refs):
        for b in range(bsz):
            lw = lw_ref[b]
            cs = _dot_exact_lhs(tri_d[d], lw)
            tot = jnp.sum(lw, axis=0, keepdims=True)
            e_inv = jnp.exp(-cs)
            p_tot = jnp.exp(tot)
            rt = r_ref[b] * jnp.exp(cs)
            at = an_ref[b] * jnp.exp(cs - lw)
            bt = bb_ref[b] * e_inv
            kt = kd_ref[b] * e_inv
            bh = bt * p_tot
            kh = kt * p_tot
            vv = v_ref[b]
            for g in range(NG):
                sl = slice(g * GW, (g + 1) * GW)
                chains.append((d, b, g))
                for lst, val in ((at_g, at), (rt_g, rt), (v_g, vv), (bt_g, bt), (kt_g, kt), (bh_g, bh), (kh_g, kh),
                                 (pt_g, p_tot)):
                    lst.append(val[:, sl])
    strict = [strict_d[d] for d, _, _ in chains]
    incl = [incl_d[d] for d, _, _ in chains]

    at_b = [b16(a) for a in at_g]
    v_b = [b16(v) for v in v_g]
    ar_b = [cat(a, b16(r), 0) for a, r in zip(at_b, rt_g)]
    sbk = [_dot(ar, cat(bd(b16(xb)), bd(b16(xk)), 0), NT) for ar, xb, xk in zip(ar_b, bt_g, kt_g)]
    sb = [x[:, :GW] for x in sbk]
    sk = [x[:, GW:] for x in sbk]
    m_ab = [x[:C] * m for x, m in zip(sb, strict)]
    n_rb = [x[C:] * m for x, m in zip(sb, incl)]
    m_ak = [x[:C] * m for x, m in zip(sk, strict)]
    n_rk = [x[C:] * m for x, m in zip(sk, incl)]
    v_bd = [bd(v) for v in v_b]
    mv = [_dot(b16(m), vb) for m, vb in zip(m_ak, v_bd)]

    x = [eye + m * same_blk[0] for m in m_ab]
    for lvl in range(len(same_blk)):
        inner = same_blk[lvl]
        outer = same_blk[lvl + 1] if lvl + 1 < len(same_blk) else 1.0
        sel = outer - inner
        if lvl in side_work:
            done = side_work[lvl](_zero_after(x[0][:SUBLANES]))
        x_b = [b16(xi) for xi in x]
        t = [_dot(xb, bd(b16(m * sel))) for xb, m in zip(x_b, m_ab)]
        x = [xi + _dot(b16(ti), bd(xb)) for xi, ti, xb in zip(x, t, x_b)]
        if lvl in side_work:
            hold = jnp.tile(done, (C // SUBLANES, 1))
            x = [xi + hold for xi in x]

    w_ab = [_dot(b16(xi), cat(bd(a), bd(b16(m)), 1)) for xi, a, m in zip(x, at_b, mv)]
    w_a_b = [b16(w[:, :GW]) for w in w_ab]
    w_b_b = [b16(w[:, GW:]) for w in w_ab]
    zero_bd = jnp.zeros((GW, GW), BF16)
    yqc = [_dot(cat(b16(n), b16(nk), 1), cat(cat(bd(wa), bd(wb), 1), cat(zero_bd, vb, 1), 0))
           for n, nk, wa, wb, vb in zip(n_rb, n_rk, w_a_b, w_b_b, v_bd)]
    y_q = [r + x[:, :GW] for r, x in zip(rt_g, yqc)]
    y_c = [x[:, GW:] for x in yqc]
    zero_c = jnp.zeros((C, GW), BF16)
    php = [_dot(cat(cat(wa, wb, 1), cat(zero_c, v, 1), 0), cat(b16(b), b16(k), 0), TN)
           for wa, wb, v, b, k in zip(w_a_b, w_b_b, v_b, bh_g, kh_g)]
    phi = [x[:GW] * state_mask for x in php]
    psi = [x[GW:] * state_mask for x in php]
    for i, (d, b, g) in enumerate(chains):
        s0 = s_ref[d, b, g]
        s0_b = b16(s0)
        dir_refs[d][-1][b, :, g * GW:(g + 1) * GW] = _dot(b16(y_q[i]), s0_b, NT) + y_c[i]
        s_ref[d, b, g] = s0 * pt_g[i] + _dot(s0_b, b16(phi[i])) + psi[i]


def _attn_bias_init(slope_ref, bias_ref):
    rows2 = 2 * BLOCK
    qi = lax.broadcasted_iota(jnp.int32, (rows2, 3 * BLOCK), 0) % BLOCK
    kpos = lax.broadcasted_iota(jnp.int32, (rows2, 3 * BLOCK), 1) - BLOCK
    upper = lax.broadcasted_iota(jnp.int32, (rows2, 3 * BLOCK), 0) >= BLOCK
    dist = jnp.abs(qi - kpos)
    for p in range(ATTN_Q_HEADS // 2):
        slope = jnp.where(upper, slope_ref[2 * p + 1], slope_ref[2 * p])
        bias_ref[p] = jnp.where(dist <= WINDOW, -slope * dist.astype(F32), NEG_INF)


def _attn_step(n, nb, q_ref, kp_ref, kc_ref, kn_ref, qg_ref, kg_ref, slope_ref, sink_ref, o_ref, bias_ref):
    pairs = ATTN_Q_HEADS // 2
    rows2 = 2 * BLOCK
    hq = _head_sum_matrix(ATTN_WIDTH)
    hk = _head_sum_matrix(KV_WIDTH)
    lane = lax.broadcasted_iota(jnp.int32, (BLOCK, LANES), 1)
    low = lane < HEAD
    srow = lax.broadcasted_iota(jnp.int32, (rows2, 1), 0)

    q = q_ref[:, :ATTN_WIDTH]
    q = q * lax.rsqrt(_head_sums(q * q, hq) * (1.0 / HEAD) + RMS_EPS) * (qg_ref[...] * (HEAD ** -0.5))

    def dup(x, g):
        rolled = pltpu.roll(x, HEAD, axis=1)
        return (jnp.where(low, x, rolled) if g == 0 else jnp.where(low, rolled, x)).astype(BF16)

    k_dup, v_dup = [[], []], [[], []]
    for ref in (kp_ref, kc_ref, kn_ref):
        kx = ref[:, :KV_WIDTH]
        kx = kx * lax.rsqrt(_head_sums(kx * kx, hk) * (1.0 / HEAD) + RMS_EPS) * kg_ref[...]
        vx = ref[:, KV_WIDTH:]
        for g in range(ATTN_KV_HEADS):
            k_dup[g].append(dup(kx, g))
            v_dup[g].append(dup(vx, g))
    k_cat = [jnp.concatenate(k_dup[g], axis=0) for g in range(ATTN_KV_HEADS)]
    v_cat = [jnp.concatenate(v_dup[g], axis=0) for g in range(ATTN_KV_HEADS)]
    edge_prev = jnp.where(n > 0, 0.0, NEG_INF)
    edge_next = jnp.where(n < nb - 1, 0.0, NEG_INF)

    def head_pair(p, start):
        g = (2 * p) // ATTN_GROUP
        qp = q[:, p * LANES:(p + 1) * LANES] + jnp.tile(start, (BLOCK // SUBLANES, 1))
        qs = jnp.concatenate([jnp.where(low, qp, 0.0), jnp.where(low, 0.0, qp)], axis=0).astype(BF16)
        s = _dot(qs, k_cat[g], NT) + bias_ref[p]
        s0 = s[:, :BLOCK] + edge_prev
        s1 = s[:, BLOCK:2 * BLOCK]
        s2 = s[:, 2 * BLOCK:] + edge_next
        sink = jnp.where(srow >= BLOCK, sink_ref[2 * p + 1], sink_ref[2 * p])
        m = jnp.maximum(jnp.max(jnp.maximum(jnp.maximum(s0, s1), s2), axis=-1, keepdims=True), sink)
        e0, e1, e2 = jnp.exp(s0 - m), jnp.exp(s1 - m), jnp.exp(s2 - m)
        den = jnp.sum(e0 + e1 + e2, axis=-1, keepdims=True) + jnp.exp(sink - m)
        e = jnp.concatenate([e0, e1, e2], axis=1).astype(BF16)
        o2 = _dot(e, v_cat[g]) * (1.0 / den)
        o = jnp.where(low, o2[:BLOCK], o2[BLOCK:])
        z_lo = ATTN_WIDTH + 2 * KV_WIDTH + p * LANES
        zp = q_ref[:, z_lo:z_lo + LANES]
        og = o * (zp * _sigmoid(zp))
        o_ref[:, p * LANES:(p + 1) * LANES] = og.astype(o_ref.dtype)
        return _zero_after(og[:SUBLANES])

    return head_pair


N_WKV_IN = 12
N_ATTN_IN = 8
ATTN_JOIN_LEVELS = (1, 2, 3, 4)


def _wkv_attn_body(*refs, nb):
    wkv_in = refs[:N_WKV_IN]
    attn_in = refs[N_WKV_IN:N_WKV_IN + N_ATTN_IN]
    yf_ref, yb_ref, o_ref, s_ref, bias_ref = refs[N_WKV_IN + N_ATTN_IN:]
    c = pl.program_id(0)

    @pl.when(c == 0)
    def _():
        s_ref[...] = jnp.zeros_like(s_ref)
        _attn_bias_init(attn_in[6], bias_ref)

    head_pair = _attn_step(c % nb, nb, *attn_in, o_ref, bias_ref)
    side_work = {lvl: functools.partial(head_pair, p) for p, lvl in enumerate(ATTN_JOIN_LEVELS)}
    _wkv_step(side_work, *wkv_in, yf_ref, yb_ref, s_ref)


def _wkv_attn(r, v, an, lw, kd, bb, ub, q_g, k_g, slopes, sink):
    bsz, T, W = r.shape
    C = CHUNK
    nc = T // C
    nb = T // BLOCK
    assert nc == bsz * nb
    gw = WKV_GROUP * HEAD
    fwd = pl.BlockSpec((bsz, C, W), lambda c: (0, c, 0))
    bwd = pl.BlockSpec((bsz, C, W), lambda c: (0, nc - 1 - c, 0))
    fwd_d = pl.BlockSpec((None, bsz, C, W), lambda c: (0, 0, c, 0))
    bwd_d = pl.BlockSpec((None, bsz, C, W), lambda c: (1, 0, nc - 1 - c, 0))
    kv_blk = ATTN_WIDTH // (2 * KV_WIDTH)
    kv = lambda f: pl.BlockSpec((None, BLOCK, 2 * KV_WIDTH), f)
    smem = pl.BlockSpec(memory_space=pltpu.SMEM)
    attn_specs = [
        pl.BlockSpec((None, BLOCK, B_COLS), lambda c: (c // nb, c % nb, 0)),
        kv(lambda c: (c // nb, jnp.maximum(c % nb - 1, 0), kv_blk)),
        kv(lambda c: (c // nb, c % nb, kv_blk)),
        kv(lambda c: (c // nb, jnp.minimum(c % nb + 1, nb - 1), kv_blk)),
        pl.BlockSpec((1, ATTN_WIDTH), lambda c: (0, 0)),
        pl.BlockSpec((1, KV_WIDTH), lambda c: (0, 0)),
        smem, smem,
    ]
    out = jax.ShapeDtypeStruct((bsz, T, W), F32)
    return pl.pallas_call(
        functools.partial(_wkv_attn_body, nb=nb),
        grid=(nc,),
        in_specs=[fwd, fwd, fwd, bwd, bwd, bwd, fwd_d, fwd_d, fwd_d, bwd_d, bwd_d, bwd_d] + attn_specs,
        out_specs=[fwd, bwd, pl.BlockSpec((None, BLOCK, ATTN_WIDTH), lambda c: (c // nb, c % nb, 0))],
        out_shape=[out, out, jax.ShapeDtypeStruct((bsz, T, ATTN_WIDTH), BF16)],
        scratch_shapes=[pltpu.VMEM((2, bsz, RWKV_HEADS // WKV_GROUP, gw, gw), F32),
                        pltpu.VMEM((ATTN_Q_HEADS // 2, 2 * BLOCK, 3 * BLOCK), F32)],
        compiler_params=_cparams(("arbitrary",)),
        name="wkv_attn",
    )(r, v, an, r, v, an, lw, kd, bb, lw, kd, bb, ub, ub, ub, ub, q_g, k_g, slopes, sink)


def _merge_out_body(x_ref, yf_ref, yb_ref, bonus_ref, gz_ref, ob_ref, ug_ref, p_ref, lnw_ref, lnb_ref,
                    pa_ref, pb_ref, wo_ref, pg_ref, gw_ref, pp_ref, o_ref):
    hsum = _head_sum_matrix(RWKV_WIDTH)
    y = yf_ref[...] + yb_ref[...]
    mean = _dot_split2_rhs(y, hsum) * (1.0 / HEAD)
    yc = y - mean
    var = _head_sums(yc * yc, hsum) * (1.0 / HEAD)
    yn = yc * lax.rsqrt(var + GN_EPS) * lnw_ref[...] + lnb_ref[...]
    o_a = (yn + bonus_ref[...]) * gz_ref[...]
    y_a = _dot(o_a.astype(BF16), pa_ref[...])
    y_b = _dot(ob_ref[...], pb_ref[...])
    merged = (_sigmoid(ug_ref[:, :D_MODEL].astype(F32)) * y_a
              + _sigmoid(ug_ref[:, D_MODEL:].astype(F32)) * y_b)
    x1 = x_ref[...] + _dot(merged.astype(BF16), wo_ref[...])
    ple = _dot(p_ref[...].astype(BF16), pp_ref[...])
    hn = x1 * lax.rsqrt(jnp.mean(x1 * x1, axis=-1, keepdims=True) + RMS_EPS) * pg_ref[...]
    o_ref[...] = x1 + _sigmoid(_dot(hn.astype(BF16), gw_ref[...])) * ple


def _merge_out(x2, yf, yb, bonus, gz, ob, ug, p3, layer, lnw, lnb, pa, pb, wo, pg, gw, pp, tm=MERGE_TILE):
    rows = x2.shape[0]
    W = RWKV_WIDTH
    tile = lambda n: pl.BlockSpec((tm, n), lambda i: (i, 0))
    full = lambda a: pl.BlockSpec(a.shape, lambda i: (0,) * a.ndim)
    return pl.pallas_call(
        _merge_out_body,
        grid=(rows // tm,),
        in_specs=[tile(D_MODEL), tile(W), tile(W), tile(W), tile(W), tile(ATTN_WIDTH),
                  tile(G_COLS), pl.BlockSpec((None, tm, PLE_DIM), lambda i: (layer, i, 0)), full(lnw), full(lnb), full(pa), full(pb), full(wo), full(pg),
                  full(gw), full(pp)],
        out_specs=tile(D_MODEL),
        out_shape=jax.ShapeDtypeStruct((rows, D_MODEL), F32),
        compiler_params=_cparams(("parallel",)),
        name="merge_out",
    )(x2, yf, yb, bonus, gz, ob, ug, p3, lnw, lnb, pa, pb, wo, pg, gw, pp)


def _block_diag2(m):
    z = jnp.zeros_like(m[0])
    return jnp.concatenate([jnp.concatenate([m[0], z], axis=1), jnp.concatenate([z, m[1]], axis=1)], axis=0)


def kernel(x, p, norm_g, w_in, shift_mu, decay_w0, decay_up, iclr_a0, iclr_up, vres_down, vres_up, vres_v0, k_k, k_a, r_k, ln_x_w, ln_x_b, q_norm_g, k_norm_g, sink, proj_a, proj_b, w_out, ple_norm_g, ple_gate_w, ple_proj):
    bsz, T, _ = x.shape
    depth = w_in.shape[0]
    rows = bsz * T
    W = RWKV_WIDTH
    slopes = jnp.asarray(2.0 ** (-8.0 * jnp.arange(1, ATTN_Q_HEADS + 1, dtype=F32) / ATTN_Q_HEADS), F32)
    v_first = None
    for i in range(depth):
        params = [shift_mu[i].reshape(1, A_COLS), decay_w0[i].reshape(1, 2 * W), _block_diag2(decay_up[i]).astype(BF16),
                  iclr_a0[i].reshape(1, 2 * W), _block_diag2(iclr_up[i]).astype(BF16), k_k[i].reshape(1, W),
                  k_a[i].reshape(1, W), r_k[i].reshape(1, W)]
        vres = None
        if i > 0:
            vu_pad = jnp.concatenate([vres_up[i - 1], jnp.zeros((LANES - VRES_RANK, W), F32)], axis=0)
            vd_pad = jnp.concatenate([vres_down[i - 1], jnp.zeros((D_MODEL, LANES - VRES_RANK), F32)], axis=1)
            vres = (vd_pad.astype(BF16), vu_pad.astype(BF16), vres_v0[i - 1].reshape(1, W), v_first)
        ub, ug, r, v, an, lw, kd, bb, bonus, gz = _proj_prep(x, norm_g[i].reshape(1, D_MODEL), w_in[i].astype(BF16),
                                                             params, vres)
        if i == 0:
            v_first = v
        yf, yb, ob = _wkv_attn(r, v, an, lw, kd, bb, ub, jnp.tile(q_norm_g[i], ATTN_Q_HEADS).reshape(1, ATTN_WIDTH),
                               jnp.tile(k_norm_g[i], ATTN_KV_HEADS).reshape(1, KV_WIDTH), slopes, sink[i])
        x2 = _merge_out(x.reshape(rows, D_MODEL), yf.reshape(rows, W), yb.reshape(rows, W), bonus.reshape(rows, W),
                        gz.reshape(rows, W), ob.reshape(rows, ATTN_WIDTH), ug.reshape(rows, G_COLS),
                        p.reshape(depth, rows, PLE_DIM), i, ln_x_w[i].reshape(1, W), ln_x_b[i].reshape(1, W),
                        proj_a[i].astype(BF16), proj_b[i].astype(BF16), w_out[i].astype(BF16),
                        ple_norm_g[i].reshape(1, D_MODEL), ple_gate_w[i].astype(BF16), ple_proj[i].astype(BF16))
        x = x2.reshape(bsz, T, D_MODEL)
    return x
```

```python
import functools

import jax
import jax.numpy as jnp
from jax import lax
from jax.experimental import pallas as pl
from jax.experimental.pallas import tpu as pltpu

F32 = jnp.float32
BF16 = jnp.bfloat16

D_MODEL = 1024
PLE_DIM = 256
RWKV_WIDTH = 512
HEAD = 64
RWKV_HEADS = RWKV_WIDTH // HEAD
LOW_RANK = 64
VRES_RANK = 32
ATTN_WIDTH = 512
ATTN_Q_HEADS = ATTN_WIDTH // HEAD
ATTN_KV_HEADS = 2
ATTN_GROUP = ATTN_Q_HEADS // ATTN_KV_HEADS
KV_WIDTH = ATTN_KV_HEADS * HEAD
WINDOW = 128
BLOCK = 128
RMS_EPS = 1e-6
GN_EPS = 64e-5
NEG_INF = -1e30
EXP_MINUS_HALF = 0.6065306597126334
LOG2_E = 1.4426950408889634
A_COLS = 4 * RWKV_WIDTH + 4 * LOW_RANK
B_COLS = 2 * ATTN_WIDTH + 2 * KV_WIDTH
G_COLS = 2 * D_MODEL
LANES = 128
SUBLANES = 8
HALO = 16
CHUNK = 64
INV_BASE = 2
WKV_GROUP = 2
PROJ_TILE = 256
MERGE_TILE = 512
VMEM_LIMIT = 48 * 1024 * 1024

NN = ((1,), (0,))
NT = ((1,), (1,))
TN = ((0,), (0,))


def _dot(a, b, dims=NN):
    return lax.dot_general(a, b, (dims, ((), ())), preferred_element_type=F32)


def _split2(x):
    hi = x.astype(BF16)
    lo = (x - hi.astype(F32)).astype(BF16)
    return hi, lo


def _dot_exact_lhs(a_bf16, b, dims=NN):
    b1 = b.astype(BF16)
    r1 = b - b1.astype(F32)
    b2 = r1.astype(BF16)
    b3 = (r1 - b2.astype(F32)).astype(BF16)
    return _dot(a_bf16, b1, dims) + (_dot(a_bf16, b2, dims) + _dot(a_bf16, b3, dims))


def _dot_split2_rhs(a, b_bf16):
    hi, lo = _split2(a)
    return _dot(hi, b_bf16) + _dot(lo, b_bf16)


def _head_sums(a, hsum_bf16):
    return _dot(a.astype(BF16), hsum_bf16)


def _zero_after(x):
    bits = lax.bitcast_convert_type(x, jnp.int32)
    half = jnp.int32(16)
    return lax.shift_right_logical(lax.shift_right_logical(bits, half), half).astype(F32)


def _sigmoid(x):
    return 1.0 / (1.0 + jnp.exp(-x))


def _head_sum_matrix(width):
    r = lax.broadcasted_iota(jnp.int32, (width, width), 0) // HEAD
    c = lax.broadcasted_iota(jnp.int32, (width, width), 1) // HEAD
    return (r == c).astype(BF16)


def _cparams(sem):
    return pltpu.CompilerParams(dimension_semantics=sem, vmem_limit_bytes=VMEM_LIMIT)


def _proj_prep_body(*refs, has_vres):
    if has_vres:
        (x_ref, xp_ref, xn_ref, g_ref, w_ref, mu_ref, w0_ref, dup_ref, a0_ref, iup_ref, kk_ref, ka_ref,
         rk_ref, vd_ref, vu_ref, v0_ref, vf_ref,
         ub_o, ug_o, r_o, v_o, an_o, lw_o, kd_o, bb_o, bonus_o, gz_o, hb_ref, ua_ref) = refs
    else:
        (x_ref, xp_ref, xn_ref, g_ref, w_ref, mu_ref, w0_ref, dup_ref, a0_ref, iup_ref, kk_ref, ka_ref,
         rk_ref,
         ub_o, ug_o, r_o, v_o, an_o, lw_o, kd_o, bb_o, bonus_o, gz_o, hb_ref, ua_ref) = refs
    tm = x_ref.shape[0]
    i = pl.program_id(1)
    last = pl.num_programs(1) - 1
    W = RWKV_WIDTH

    def normed(ref):
        x = ref[...]
        return (x * lax.rsqrt(jnp.mean(x * x, axis=-1, keepdims=True) + RMS_EPS) * g_ref[...]).astype(BF16)

    hb_ref[0:HALO] = jnp.where(i > 0, normed(xp_ref), jnp.zeros((), BF16))
    hb_ref[HALO:HALO + tm] = normed(x_ref)
    hb_ref[HALO + tm:] = jnp.where(i < last, normed(xn_ref), jnp.zeros((), BF16))
    ua_ref[...] = _dot(hb_ref[...], w_ref[:, :A_COLS])
    hm = hb_ref[HALO:HALO + tm]

    def project_b(lo, hi):
        ub_o[:, lo:hi] = _dot(hm, w_ref[:, A_COLS + lo:A_COLS + hi])

    def project_g(lo, hi):
        ug_o[:, lo:hi] = _dot(hm, w_ref[:, A_COLS + B_COLS + lo:A_COLS + B_COLS + hi]).astype(ug_o.dtype)

    hsum = _head_sum_matrix(W)

    def shifted(lo, hi):
        u = ua_ref[HALO:HALO + tm, lo:hi]
        ext = ua_ref[:, lo:hi]
        prev = pltpu.roll(ext, 1, axis=0)[HALO:HALO + tm]
        nxt = pltpu.roll(ext, tm + 2 * HALO - 1, axis=0)[HALO:HALO + tm]
        return u + mu_ref[:, lo:hi] * (0.5 * (prev + nxt) - u)

    r = shifted(0, W)
    project_b(0, ATTN_WIDTH)
    k = shifted(W, 2 * W)
    project_b(ATTN_WIDTH, B_COLS)
    v = shifted(2 * W, 3 * W)
    project_g(0, D_MODEL // 2)
    z = shifted(3 * W, 4 * W)
    project_g(D_MODEL // 2, D_MODEL)
    low = shifted(4 * W, 4 * W + 4 * LOW_RANK)
    w_raw = w0_ref[...] + _dot(jnp.tanh(low[:, :2 * LOW_RANK]).astype(BF16), dup_ref[...])
    project_g(D_MODEL, D_MODEL + D_MODEL // 2)
    lw = -EXP_MINUS_HALF * _sigmoid(w_raw)
    a = _sigmoid(a0_ref[...] + _dot(low[:, 2 * LOW_RANK:].astype(BF16), iup_ref[...]))
    if has_vres:
        hd = _dot(hm, vd_ref[...])
        mix = _sigmoid(v0_ref[...] + _dot(hd.astype(BF16), vu_ref[...]))
        v = v + (vf_ref[...] - v) * mix
    project_g(D_MODEL + D_MODEL // 2, G_COLS)
    kk = k * kk_ref[...]
    ss = _head_sums(kk * kk, hsum)
    kk = kk * lax.rsqrt(jnp.maximum(ss, 1e-24))
    ka = ka_ref[...]
    ksum = jnp.zeros_like(k)
    for d in range(2):
        a_d = a[:, d * W:(d + 1) * W]
        kd = k * (1.0 + (a_d - 1.0) * ka)
        ksum = ksum + kd
        lw_o[d] = lw[:, d * W:(d + 1) * W]
        kd_o[d] = kd
        bb_o[d] = kk * a_d
    r_o[...] = r
    v_o[...] = v
    an_o[...] = -kk
    bonus_o[...] = _head_sums(r * ksum * rk_ref[...], hsum) * v
    gz_o[...] = z * _sigmoid(z)


def _proj_prep(x, g, w, params, vres, tm=PROJ_TILE):
    bsz, T, _ = x.shape
    W = RWKV_WIDTH
    nt = T // tm
    hb = tm // HALO
    nh = T // HALO
    const = lambda a: pl.BlockSpec(a.shape, lambda b, i: (0,) * a.ndim, pipeline_mode=pl.Buffered(1))
    tile = lambda n: pl.BlockSpec((None, tm, n), lambda b, i: (b, i, 0))
    in_specs = [
        tile(D_MODEL),
        pl.BlockSpec((None, HALO, D_MODEL), lambda b, i: (b, jnp.maximum(i * hb - 1, 0), 0)),
        pl.BlockSpec((None, HALO, D_MODEL), lambda b, i: (b, jnp.minimum((i + 1) * hb, nh - 1), 0)),
        const(g), const(w),
    ] + [const(a) for a in params]
    args = [x, x, x, g, w] + list(params)
    if vres is not None:
        vd_pad, vu_split, v0, v_first = vres
        in_specs += [const(vd_pad), const(vu_split), const(v0), tile(W)]
        args += [vd_pad, vu_split, v0, v_first]
    dir_tile = pl.BlockSpec((2, None, tm, W), lambda b, i: (0, b, i, 0))
    out_specs = [tile(B_COLS), tile(G_COLS), tile(W), tile(W), tile(W), dir_tile, dir_tile, dir_tile, tile(W), tile(W)]
    one = jax.ShapeDtypeStruct((bsz, T, W), F32)
    two = jax.ShapeDtypeStruct((2, bsz, T, W), F32)
    out_shape = [jax.ShapeDtypeStruct((bsz, T, B_COLS), F32), jax.ShapeDtypeStruct((bsz, T, G_COLS), BF16),
                 one, one, one, two, two, two, one, one]
    return pl.pallas_call(
        functools.partial(_proj_prep_body, has_vres=vres is not None),
        grid=(bsz, nt),
        in_specs=in_specs,
        out_specs=out_specs,
        out_shape=out_shape,
        scratch_shapes=[pltpu.VMEM((tm + 2 * HALO, D_MODEL), BF16), pltpu.VMEM((tm + 2 * HALO, A_COLS), F32)],
        compiler_params=_cparams(("parallel", "parallel")),
        name="proj_prep",
    )(*args)


def _wkv_step(side_work, rf_ref, vf_ref, af_ref, rb_ref, vb_ref, ab_ref, lwf_ref, kdf_ref, bbf_ref,
              lwb_ref, kdb_ref, bbb_ref, yf_ref, yb_ref, s_ref):
    bsz, C, _ = rf_ref.shape
    G = WKV_GROUP
    GW = G * HEAD
    NG = RWKV_HEADS // G
    assert C == HEAD

    row = lax.broadcasted_iota(jnp.int32, (C, GW), 0)
    lane = lax.broadcasted_iota(jnp.int32, (C, GW), 1)
    col = lane % HEAD
    eye = (row == col).astype(F32)
    same_blk = []
    size = INV_BASE
    while size < C:
        same_blk.append(((row // size) == (col // size)).astype(F32))
        size *= 2
    head_mask = [((lane // HEAD) == g).astype(BF16) for g in range(G)]
    srow = lax.broadcasted_iota(jnp.int32, (GW, GW), 0) // HEAD
    slane = lax.broadcasted_iota(jnp.int32, (GW, GW), 1) // HEAD
    state_mask = (srow == slane).astype(F32)
    trow = lax.broadcasted_iota(jnp.int32, (C, C), 0)
    tcol = lax.broadcasted_iota(jnp.int32, (C, C), 1)
    incl_d = [(col <= row).astype(F32), (col >= row).astype(F32)]
    strict_d = [(col < row).astype(F32), (col > row).astype(F32)]
    tri_d = [(tcol <= trow).astype(BF16), (tcol >= trow).astype(BF16)]

    b16 = lambda x: x.astype(BF16)
    bd = lambda xb: jnp.concatenate([xb * m for m in head_mask], axis=0)
    cat = lambda a, b, axis: jnp.concatenate([a, b], axis=axis)

    dir_refs = [(rf_ref, vf_ref, af_ref, lwf_ref, kdf_ref, bbf_ref, yf_ref),
                (rb_ref, vb_ref, ab_ref, lwb_ref, kdb_ref, bbb_ref, yb_ref)]
    chains = []
    at_g, rt_g, v_g, bt_g, kt_g, bh_g, kh_g, pt_g = [], [], [], [], [], [], [], []
    for d, (r_ref, v_ref, an_ref, lw_ref, kd_ref, bb_ref, _) in enumerate(dir_refs):
        for b in range(bsz):
            lw = lw_ref[b]
            cs = _dot_exact_lhs(tri_d[d], lw)
            tot = jnp.sum(lw, axis=0, keepdims=True)
            e_inv = jnp.exp(-cs)
            p_tot = jnp.exp(tot)
            rt = r_ref[b] * jnp.exp(cs)
            at = an_ref[b] * jnp.exp(cs - lw)
            bt = bb_ref[b] * e_inv
            kt = kd_ref[b] * e_inv
            bh = bt * p_tot
            kh = kt * p_tot
            vv = v_ref[b]
            for g in range(NG):
                sl = slice(g * GW, (g + 1) * GW)
                chains.append((d, b, g))
                for lst, val in ((at_g, at), (rt_g, rt), (v_g, vv), (bt_g, bt), (kt_g, kt), (bh_g, bh), (kh_g, kh),
                                 (pt_g, p_tot)):
                    lst.append(val[:, sl])
    strict = [strict_d[d] for d, _, _ in chains]
    incl = [incl_d[d] for d, _, _ in chains]

    at_b = [b16(a) for a in at_g]
    v_b = [b16(v) for v in v_g]
    ar_b = [cat(a, b16(r), 0) for a, r in zip(at_b, rt_g)]
    sbk = [_dot(ar, cat(bd(b16(xb)), bd(b16(xk)), 0), NT) for ar, xb, xk in zip(ar_b, bt_g, kt_g)]
    sb = [x[:, :GW] for x in sbk]
    sk = [x[:, GW:] for x in sbk]
    m_ab = [x[:C] * m for x, m in zip(sb, strict)]
    n_rb = [x[C:] * m for x, m in zip(sb, incl)]
    m_ak = [x[:C] * m for x, m in zip(sk, strict)]
    n_rk = [x[C:] * m for x, m in zip(sk, incl)]
    v_bd = [bd(v) for v in v_b]
    mv = [_dot(b16(m), vb) for m, vb in zip(m_ak, v_bd)]

    x = [eye + m * same_blk[0] for m in m_ab]
    for lvl in range(len(same_blk)):
        inner = same_blk[lvl]
        outer = same_blk[lvl + 1] if lvl + 1 < len(same_blk) else 1.0
        sel = outer - inner
        if lvl in side_work:
            done = side_work[lvl](_zero_after(x[0][:SUBLANES]))
        x_b = [b16(xi) for xi in x]
        t = [_dot(xb, bd(b16(m * sel))) for xb, m in zip(x_b, m_ab)]
        x = [xi + _dot(b16(ti), bd(xb)) for xi, ti, xb in zip(x, t, x_b)]
        if lvl in side_work:
            hold = jnp.tile(done, (C // SUBLANES, 1))
            x = [xi + hold for xi in x]

    w_ab = [_dot(b16(xi), cat(bd(a), bd(b16(m)), 1)) for xi, a, m in zip(x, at_b, mv)]
    w_a_b = [b16(w[:, :GW]) for w in w_ab]
    w_b_b = [b16(w[:, GW:]) for w in w_ab]
    zero_bd = jnp.zeros((GW, GW), BF16)
    yqc = [_dot(cat(b16(n), b16(nk), 1), cat(cat(bd(wa), bd(wb), 1), cat(zero_bd, vb, 1), 0))
           for n, nk, wa, wb, vb in zip(n_rb, n_rk, w_a_b, w_b_b, v_bd)]
    y_q = [r + x[:, :GW] for r, x in zip(rt_g, yqc)]
    y_c = [x[:, GW:] for x in yqc]
    zero_c = jnp.zeros((C, GW), BF16)
    php = [_dot(cat(cat(wa, wb, 1), cat(zero_c, v, 1), 0), cat(b16(b), b16(k), 0), TN)
           for wa, wb, v, b, k in zip(w_a_b, w_b_b, v_b, bh_g, kh_g)]
    phi = [x[:GW] * state_mask for x in php]
    psi = [x[GW:] * state_mask for x in php]
    for i, (d, b, g) in enumerate(chains):
        s0 = s_ref[d, b, g]
        s0_b = b16(s0)
        dir_refs[d][-1][b, :, g * GW:(g + 1) * GW] = _dot(b16(y_q[i]), s0_b, NT) + y_c[i]
        s_ref[d, b, g] = s0 * pt_g[i] + _dot(s0_b, b16(phi[i])) + psi[i]


def _attn_bias_init(slope_ref, bias_ref):
    rows2 = 2 * BLOCK
    qi = lax.broadcasted_iota(jnp.int32, (rows2, 3 * BLOCK), 0) % BLOCK
    kpos = lax.broadcasted_iota(jnp.int32, (rows2, 3 * BLOCK), 1) - BLOCK
    upper = lax.broadcasted_iota(jnp.int32, (rows2, 3 * BLOCK), 0) >= BLOCK
    dist = jnp.abs(qi - kpos)
    for p in range(ATTN_Q_HEADS // 2):
        slope = jnp.where(upper, slope_ref[2 * p + 1], slope_ref[2 * p])
        bias_ref[p] = jnp.where(dist <= WINDOW, (-LOG2_E) * slope * dist.astype(F32), NEG_INF)


def _attn_step(n, nb, q_ref, kp_ref, kc_ref, kn_ref, qg_ref, kg_ref, slope_ref, sink_ref, o_ref, bias_ref):
    pairs = ATTN_Q_HEADS // 2
    rows2 = 2 * BLOCK
    hq = _head_sum_matrix(ATTN_WIDTH)
    hk = _head_sum_matrix(KV_WIDTH)
    lane = lax.broadcasted_iota(jnp.int32, (BLOCK, LANES), 1)
    low = lane < HEAD
    srow = lax.broadcasted_iota(jnp.int32, (rows2, 1), 0)

    q = q_ref[:, :ATTN_WIDTH]
    q = q * lax.rsqrt(_head_sums(q * q, hq) * (1.0 / HEAD) + RMS_EPS) * (qg_ref[...] * (HEAD ** -0.5 * LOG2_E))

    def dup(x, g):
        rolled = pltpu.roll(x, HEAD, axis=1)
        return (jnp.where(low, x, rolled) if g == 0 else jnp.where(low, rolled, x)).astype(BF16)

    k_dup, v_dup = [[], []], [[], []]
    for ref in (kp_ref, kc_ref, kn_ref):
        kx = ref[:, :KV_WIDTH]
        kx = kx * lax.rsqrt(_head_sums(kx * kx, hk) * (1.0 / HEAD) + RMS_EPS) * kg_ref[...]
        vx = ref[:, KV_WIDTH:]
        for g in range(ATTN_KV_HEADS):
            k_dup[g].append(dup(kx, g))
            v_dup[g].append(dup(vx, g))
    k_cat = [jnp.concatenate(k_dup[g], axis=0) for g in range(ATTN_KV_HEADS)]
    v_cat = [jnp.concatenate(v_dup[g], axis=0) for g in range(ATTN_KV_HEADS)]
    edge_prev = jnp.where(n > 0, 0.0, NEG_INF)
    edge_next = jnp.where(n < nb - 1, 0.0, NEG_INF)

    def head_pair(p, start):
        g = (2 * p) // ATTN_GROUP
        qp = q[:, p * LANES:(p + 1) * LANES] + jnp.tile(start, (BLOCK // SUBLANES, 1))
        qs = jnp.concatenate([jnp.where(low, qp, 0.0), jnp.where(low, 0.0, qp)], axis=0).astype(BF16)
        s = _dot(qs, k_cat[g], NT) + bias_ref[p]
        s0 = s[:, :BLOCK] + edge_prev
        s1 = s[:, BLOCK:2 * BLOCK]
        s2 = s[:, 2 * BLOCK:] + edge_next
        sink = jnp.where(srow >= BLOCK, sink_ref[2 * p + 1], sink_ref[2 * p]) * LOG2_E
        m = jnp.maximum(jnp.max(jnp.maximum(jnp.maximum(s0, s1), s2), axis=-1, keepdims=True), sink)
        e0, e1, e2 = jnp.exp2(s0 - m), jnp.exp2(s1 - m), jnp.exp2(s2 - m)
        den = jnp.sum(e0 + e1 + e2, axis=-1, keepdims=True) + jnp.exp2(sink - m)
        e = jnp.concatenate([e0, e1, e2], axis=1).astype(BF16)
        o2 = _dot(e, v_cat[g]) * (1.0 / den)
        o = jnp.where(low, o2[:BLOCK], o2[BLOCK:])
        z_lo = ATTN_WIDTH + 2 * KV_WIDTH + p * LANES
        zp = q_ref[:, z_lo:z_lo + LANES]
        og = o * (zp * _sigmoid(zp))
        o_ref[:, p * LANES:(p + 1) * LANES] = og.astype(o_ref.dtype)
        return _zero_after(og[:SUBLANES])

    return head_pair


N_WKV_IN = 12
N_ATTN_IN = 8
ATTN_JOIN_LEVELS = (1, 2, 3, 4)


def _wkv_attn_body(*refs, nb):
    wkv_in = refs[:N_WKV_IN]
    attn_in = refs[N_WKV_IN:N_WKV_IN + N_ATTN_IN]
    yf_ref, yb_ref, o_ref, s_ref, bias_ref = refs[N_WKV_IN + N_ATTN_IN:]
    c = pl.program_id(0)

    @pl.when(c == 0)
    def _():
        s_ref[...] = jnp.zeros_like(s_ref)
        _attn_bias_init(attn_in[6], bias_ref)

    head_pair = _attn_step(c % nb, nb, *attn_in, o_ref, bias_ref)
    side_work = {lvl: functools.partial(head_pair, p) for p, lvl in enumerate(ATTN_JOIN_LEVELS)}
    _wkv_step(side_work, *wkv_in, yf_ref, yb_ref, s_ref)


def _wkv_attn(r, v, an, lw, kd, bb, ub, q_g, k_g, slopes, sink):
    bsz, T, W = r.shape
    C = CHUNK
    nc = T // C
    nb = T // BLOCK
    assert nc == bsz * nb
    gw = WKV_GROUP * HEAD
    fwd = pl.BlockSpec((bsz, C, W), lambda c: (0, c, 0))
    bwd = pl.BlockSpec((bsz, C, W), lambda c: (0, nc - 1 - c, 0))
    fwd_d = pl.BlockSpec((None, bsz, C, W), lambda c: (0, 0, c, 0))
    bwd_d = pl.BlockSpec((None, bsz, C, W), lambda c: (1, 0, nc - 1 - c, 0))
    kv_blk = ATTN_WIDTH // (2 * KV_WIDTH)
    kv = lambda f: pl.BlockSpec((None, BLOCK, 2 * KV_WIDTH), f)
    smem = pl.BlockSpec(memory_space=pltpu.SMEM)
    attn_specs = [
        pl.BlockSpec((None, BLOCK, B_COLS), lambda c: (c // nb, c % nb, 0)),
        kv(lambda c: (c // nb, jnp.maximum(c % nb - 1, 0), kv_blk)),
        kv(lambda c: (c // nb, c % nb, kv_blk)),
        kv(lambda c: (c // nb, jnp.minimum(c % nb + 1, nb - 1), kv_blk)),
        pl.BlockSpec((1, ATTN_WIDTH), lambda c: (0, 0)),
        pl.BlockSpec((1, KV_WIDTH), lambda c: (0, 0)),
        smem, smem,
    ]
    out = jax.ShapeDtypeStruct((bsz, T, W), F32)
    return pl.pallas_call(
        functools.partial(_wkv_attn_body, nb=nb),
        grid=(nc,),
        in_specs=[fwd, fwd, fwd, bwd, bwd, bwd, fwd_d, fwd_d, fwd_d, bwd_d, bwd_d, bwd_d] + attn_specs,
        out_specs=[fwd, bwd, pl.BlockSpec((None, BLOCK, ATTN_WIDTH), lambda c: (c // nb, c % nb, 0))],
        out_shape=[out, out, jax.ShapeDtypeStruct((bsz, T, ATTN_WIDTH), BF16)],
        scratch_shapes=[pltpu.VMEM((2, bsz, RWKV_HEADS // WKV_GROUP, gw, gw), F32),
                        pltpu.VMEM((ATTN_Q_HEADS // 2, 2 * BLOCK, 3 * BLOCK), F32)],
        compiler_params=_cparams(("arbitrary",)),
        name="wkv_attn",
    )(r, v, an, r, v, an, lw, kd, bb, lw, kd, bb, ub, ub, ub, ub, q_g, k_g, slopes, sink)


def _merge_out_body(x_ref, yf_ref, yb_ref, bonus_ref, gz_ref, ob_ref, ug_ref, p_ref, lnw_ref, lnb_ref,
                    pa_ref, pb_ref, wo_ref, pg_ref, gw_ref, pp_ref, o_ref):
    hsum = _head_sum_matrix(RWKV_WIDTH)
    y = yf_ref[...] + yb_ref[...]
    mean = _dot_split2_rhs(y, hsum) * (1.0 / HEAD)
    yc = y - mean
    var = _head_sums(yc * yc, hsum) * (1.0 / HEAD)
    yn = yc * lax.rsqrt(var + GN_EPS) * lnw_ref[...] + lnb_ref[...]
    o_a = (yn + bonus_ref[...]) * gz_ref[...]
    y_a = _dot(o_a.astype(BF16), pa_ref[...])
    y_b = _dot(ob_ref[...], pb_ref[...])
    merged = (_sigmoid(ug_ref[:, :D_MODEL].astype(F32)) * y_a
              + _sigmoid(ug_ref[:, D_MODEL:].astype(F32)) * y_b)
    x1 = x_ref[...] + _dot(merged.astype(BF16), wo_ref[...])
    ple = _dot(p_ref[...].astype(BF16), pp_ref[...])
    hn = x1 * lax.rsqrt(jnp.mean(x1 * x1, axis=-1, keepdims=True) + RMS_EPS) * pg_ref[...]
    o_ref[...] = x1 + _sigmoid(_dot(hn.astype(BF16), gw_ref[...])) * ple


def _merge_out(x2, yf, yb, bonus, gz, ob, ug, p3, layer, lnw, lnb, pa, pb, wo, pg, gw, pp, tm=MERGE_TILE):
    rows = x2.shape[0]
    W = RWKV_WIDTH
    tile = lambda n: pl.BlockSpec((tm, n), lambda i: (i, 0))
    full = lambda a: pl.BlockSpec(a.shape, lambda i: (0,) * a.ndim)
    return pl.pallas_call(
        _merge_out_body,
        grid=(rows // tm,),
        in_specs=[tile(D_MODEL), tile(W), tile(W), tile(W), tile(W), tile(ATTN_WIDTH),
                  tile(G_COLS), pl.BlockSpec((None, tm, PLE_DIM), lambda i: (layer, i, 0)), full(lnw), full(lnb), full(pa), full(pb), full(wo), full(pg),
                  full(gw), full(pp)],
        out_specs=tile(D_MODEL),
        out_shape=jax.ShapeDtypeStruct((rows, D_MODEL), F32),
        compiler_params=_cparams(("parallel",)),
        name="merge_out",
    )(x2, yf, yb, bonus, gz, ob, ug, p3, lnw, lnb, pa, pb, wo, pg, gw, pp)


def _block_diag2(m):
    z = jnp.zeros_like(m[0])
    return jnp.concatenate([jnp.concatenate([m[0], z], axis=1), jnp.concatenate([z, m[1]], axis=1)], axis=0)


def kernel(x, p, norm_g, w_in, shift_mu, decay_w0, decay_up, iclr_a0, iclr_up, vres_down, vres_up, vres_v0, k_k, k_a, r_k, ln_x_w, ln_x_b, q_norm_g, k_norm_g, sink, proj_a, proj_b, w_out, ple_norm_g, ple_gate_w, ple_proj):
    bsz, T, _ = x.shape
    depth = w_in.shape[0]
    rows = bsz * T
    W = RWKV_WIDTH
    slopes = jnp.asarray(2.0 ** (-8.0 * jnp.arange(1, ATTN_Q_HEADS + 1, dtype=F32) / ATTN_Q_HEADS), F32)
    v_first = None
    for i in range(depth):
        params = [shift_mu[i].reshape(1, A_COLS), decay_w0[i].reshape(1, 2 * W), _block_diag2(decay_up[i]).astype(BF16),
                  iclr_a0[i].reshape(1, 2 * W), _block_diag2(iclr_up[i]).astype(BF16), k_k[i].reshape(1, W),
                  k_a[i].reshape(1, W), r_k[i].reshape(1, W)]
        vres = None
        if i > 0:
            vu_pad = jnp.concatenate([vres_up[i - 1], jnp.zeros((LANES - VRES_RANK, W), F32)], axis=0)
            vd_pad = jnp.concatenate([vres_down[i - 1], jnp.zeros((D_MODEL, LANES - VRES_RANK), F32)], axis=1)
            vres = (vd_pad.astype(BF16), vu_pad.astype(BF16), vres_v0[i - 1].reshape(1, W), v_first)
        ub, ug, r, v, an, lw, kd, bb, bonus, gz = _proj_prep(x, norm_g[i].reshape(1, D_MODEL), w_in[i].astype(BF16),
                                                             params, vres)
        if i == 0:
            v_first = v
        yf, yb, ob = _wkv_attn(r, v, an, lw, kd, bb, ub, jnp.tile(q_norm_g[i], ATTN_Q_HEADS).reshape(1, ATTN_WIDTH),
                               jnp.tile(k_norm_g[i], ATTN_KV_HEADS).reshape(1, KV_WIDTH), slopes, sink[i])
        x2 = _merge_out(x.reshape(rows, D_MODEL), yf.reshape(rows, W), yb.reshape(rows, W), bonus.reshape(rows, W),
                        gz.reshape(rows, W), ob.reshape(rows, ATTN_WIDTH), ug.reshape(rows, G_COLS),
                        p.reshape(depth, rows, PLE_DIM), i, ln_x_w[i].reshape(1, W), ln_x_b[i].reshape(1, W),
                        proj_a[i].astype(BF16), proj_b[i].astype(BF16), w_out[i].astype(BF16),
                        ple_norm_g[i].reshape(1, D_MODEL), ple_gate_w[i].astype(BF16), ple_proj[i].astype(BF16))
        x = x2.reshape(bsz, T, D_MODEL)
    return x
```

```python
import functools

import jax
import jax.numpy as jnp
from jax import lax
from jax.experimental import pallas as pl
from jax.experimental.pallas import tpu as pltpu

F32 = jnp.float32
BF16 = jnp.bfloat16

D_MODEL = 1024
PLE_DIM = 256
RWKV_WIDTH = 512
HEAD = 64
RWKV_HEADS = RWKV_WIDTH // HEAD
LOW_RANK = 64
VRES_RANK = 32
ATTN_WIDTH = 512
ATTN_Q_HEADS = ATTN_WIDTH // HEAD
ATTN_KV_HEADS = 2
ATTN_GROUP = ATTN_Q_HEADS // ATTN_KV_HEADS
KV_WIDTH = ATTN_KV_HEADS * HEAD
WINDOW = 128
BLOCK = 128
RMS_EPS = 1e-6
GN_EPS = 64e-5
NEG_INF = -1e30
EXP_MINUS_HALF = 0.6065306597126334
A_COLS = 4 * RWKV_WIDTH + 4 * LOW_RANK
B_COLS = 2 * ATTN_WIDTH + 2 * KV_WIDTH
G_COLS = 2 * D_MODEL
LANES = 128
SUBLANES = 8
HALO = 16
CHUNK = 64
INV_BASE = 2
WKV_GROUP = 2
PROJ_TILE = 256
MERGE_TILE = 512
VMEM_LIMIT = 48 * 1024 * 1024

NN = ((1,), (0,))
NT = ((1,), (1,))
TN = ((0,), (0,))


def _dot(a, b, dims=NN):
    return lax.dot_general(a, b, (dims, ((), ())), preferred_element_type=F32)


def _split2(x):
    hi = x.astype(BF16)
    lo = (x - hi.astype(F32)).astype(BF16)
    return hi, lo


def _dot_exact_lhs(a_bf16, b, dims=NN):
    b1 = b.astype(BF16)
    r1 = b - b1.astype(F32)
    b2 = r1.astype(BF16)
    b3 = (r1 - b2.astype(F32)).astype(BF16)
    return _dot(a_bf16, b1, dims) + (_dot(a_bf16, b2, dims) + _dot(a_bf16, b3, dims))


def _dot_split2_rhs(a, b_bf16):
    hi, lo = _split2(a)
    return _dot(hi, b_bf16) + _dot(lo, b_bf16)


def _head_sums(a, hsum_bf16):
    return _dot(a.astype(BF16), hsum_bf16)


def _zero_after(x):
    bits = lax.bitcast_convert_type(x, jnp.int32)
    half = jnp.int32(16)
    return lax.shift_right_logical(lax.shift_right_logical(bits, half), half).astype(F32)


def _sigmoid(x):
    return 1.0 / (1.0 + jnp.exp(-x))


def _head_sum_matrix(width):
    r = lax.broadcasted_iota(jnp.int32, (width, width), 0) // HEAD
    c = lax.broadcasted_iota(jnp.int32, (width, width), 1) // HEAD
    return (r == c).astype(BF16)


def _cparams(sem):
    return pltpu.CompilerParams(dimension_semantics=sem, vmem_limit_bytes=VMEM_LIMIT)


def _proj_prep_body(*refs, has_vres):
    if has_vres:
        (x_ref, xp_ref, xn_ref, g_ref, w_ref, mu_ref, w0_ref, dup_ref, a0_ref, iup_ref, kk_ref, ka_ref,
         rk_ref, vd_ref, vu_ref, v0_ref, vf_ref,
         ub_o, ug_o, r_o, v_o, an_o, lw_o, kd_o, bb_o, bonus_o, gz_o, hb_ref, ua_ref) = refs
    else:
        (x_ref, xp_ref, xn_ref, g_ref, w_ref, mu_ref, w0_ref, dup_ref, a0_ref, iup_ref, kk_ref, ka_ref,
         rk_ref,
         ub_o, ug_o, r_o, v_o, an_o, lw_o, kd_o, bb_o, bonus_o, gz_o, hb_ref, ua_ref) = refs
    tm = x_ref.shape[0]
    i = pl.program_id(1)
    last = pl.num_programs(1) - 1
    W = RWKV_WIDTH

    def normed(ref):
        x = ref[...]
        return (x * lax.rsqrt(jnp.mean(x * x, axis=-1, keepdims=True) + RMS_EPS) * g_ref[...]).astype(BF16)

    hb_ref[0:HALO] = jnp.where(i > 0, normed(xp_ref), jnp.zeros((), BF16))
    hb_ref[HALO:HALO + tm] = normed(x_ref)
    hb_ref[HALO + tm:] = jnp.where(i < last, normed(xn_ref), jnp.zeros((), BF16))
    ua_ref[...] = _dot(hb_ref[...], w_ref[:, :A_COLS])
    hm = hb_ref[HALO:HALO + tm]

    def project_b(lo, hi):
        ub_o[:, lo:hi] = _dot(hm, w_ref[:, A_COLS + lo:A_COLS + hi])

    def project_g(lo, hi):
        ug_o[:, lo:hi] = _dot(hm, w_ref[:, A_COLS + B_COLS + lo:A_COLS + B_COLS + hi]).astype(ug_o.dtype)

    hsum = _head_sum_matrix(W)

    def shifted(lo, hi):
        u = ua_ref[HALO:HALO + tm, lo:hi]
        ext = ua_ref[:, lo:hi]
        prev = pltpu.roll(ext, 1, axis=0)[HALO:HALO + tm]
        nxt = pltpu.roll(ext, tm + 2 * HALO - 1, axis=0)[HALO:HALO + tm]
        return u + mu_ref[:, lo:hi] * (0.5 * (prev + nxt) - u)

    r = shifted(0, W)
    project_b(0, ATTN_WIDTH)
    k = shifted(W, 2 * W)
    project_b(ATTN_WIDTH, B_COLS)
    v = shifted(2 * W, 3 * W)
    project_g(0, D_MODEL // 2)
    z = shifted(3 * W, 4 * W)
    project_g(D_MODEL // 2, D_MODEL)
    low = shifted(4 * W, 4 * W + 4 * LOW_RANK)
    w_raw = w0_ref[...] + _dot(jnp.tanh(low[:, :2 * LOW_RANK]).astype(BF16), dup_ref[...])
    project_g(D_MODEL, D_MODEL + D_MODEL // 2)
    lw = -EXP_MINUS_HALF * _sigmoid(w_raw)
    a = _sigmoid(a0_ref[...] + _dot(low[:, 2 * LOW_RANK:].astype(BF16), iup_ref[...]))
    if has_vres:
        hd = _dot(hm, vd_ref[...])
        mix = _sigmoid(v0_ref[...] + _dot(hd.astype(BF16), vu_ref[...]))
        v = v + (vf_ref[...] - v) * mix
    project_g(D_MODEL + D_MODEL // 2, G_COLS)
    kk = k * kk_ref[...]
    ss = _head_sums(kk * kk, hsum)
    kk = kk * lax.rsqrt(jnp.maximum(ss, 1e-24))
    ka = ka_ref[...]
    ksum = jnp.zeros_like(k)
    for d in range(2):
        a_d = a[:, d * W:(d + 1) * W]
        kd = k * (1.0 + (a_d - 1.0) * ka)
        ksum = ksum + kd
        lw_o[d] = lw[:, d * W:(d + 1) * W]
        kd_o[d] = kd
        bb_o[d] = kk * a_d
    r_o[...] = r
    v_o[...] = v
    an_o[...] = -kk
    bonus_o[...] = _head_sums(r * ksum * rk_ref[...], hsum) * v
    gz_o[...] = z * _sigmoid(z)


def _proj_prep(x, g, w, params, vres, tm=PROJ_TILE):
    bsz, T, _ = x.shape
    W = RWKV_WIDTH
    nt = T // tm
    hb = tm // HALO
    nh = T // HALO
    const = lambda a: pl.BlockSpec(a.shape, lambda b, i: (0,) * a.ndim, pipeline_mode=pl.Buffered(1))
    tile = lambda n: pl.BlockSpec((None, tm, n), lambda b, i: (b, i, 0))
    in_specs = [
        tile(D_MODEL),
        pl.BlockSpec((None, HALO, D_MODEL), lambda b, i: (b, jnp.maximum(i * hb - 1, 0), 0)),
        pl.BlockSpec((None, HALO, D_MODEL), lambda b, i: (b, jnp.minimum((i + 1) * hb, nh - 1), 0)),
        const(g), const(w),
    ] + [const(a) for a in params]
    args = [x, x, x, g, w] + list(params)
    if vres is not None:
        vd_pad, vu_split, v0, v_first = vres
        in_specs += [const(vd_pad), const(vu_split), const(v0), tile(W)]
        args += [vd_pad, vu_split, v0, v_first]
    dir_tile = pl.BlockSpec((2, None, tm, W), lambda b, i: (0, b, i, 0))
    out_specs = [tile(B_COLS), tile(G_COLS), tile(W), tile(W), tile(W), dir_tile, dir_tile, dir_tile, tile(W), tile(W)]
    one = jax.ShapeDtypeStruct((bsz, T, W), F32)
    two = jax.ShapeDtypeStruct((2, bsz, T, W), F32)
    out_shape = [jax.ShapeDtypeStruct((bsz, T, B_COLS), F32), jax.ShapeDtypeStruct((bsz, T, G_COLS), BF16),
                 one, one, one, two, two, two, one, one]
    return pl.pallas_call(
        functools.partial(_proj_prep_body, has_vres=vres is not None),
        grid=(bsz, nt),
        in_specs=in_specs,
        out_specs=out_specs,
        out_shape=out_shape,
        scratch_shapes=[pltpu.VMEM((tm + 2 * HALO, D_MODEL), BF16), pltpu.VMEM((tm + 2 * HALO, A_COLS), F32)],
        compiler_params=_cparams(("parallel", "parallel")),
        name="proj_prep",
    )(*args)


def _wkv_step(side_work, rf_ref, vf_ref, af_ref, rb_ref, vb_ref, ab_ref, lwf_ref, kdf_ref, bbf_ref,
              lwb_ref, kdb_ref, bbb_ref, yf_ref, yb_ref, s_ref):
    bsz, C, _ = rf_ref.shape
    G = WKV_GROUP
    GW = G * HEAD
    NG = RWKV_HEADS // G
    assert C == HEAD

    row = lax.broadcasted_iota(jnp.int32, (C, GW), 0)
    lane = lax.broadcasted_iota(jnp.int32, (C, GW), 1)
    col = lane % HEAD
    eye = (row == col).astype(F32)
    same_blk = []
    size = INV_BASE
    while size < C:
        same_blk.append(((row // size) == (col // size)).astype(F32))
        size *= 2
    head_mask = [((lane // HEAD) == g).astype(BF16) for g in range(G)]
    srow = lax.broadcasted_iota(jnp.int32, (GW, GW), 0) // HEAD
    slane = lax.broadcasted_iota(jnp.int32, (GW, GW), 1) // HEAD
    state_mask = (srow == slane).astype(F32)
    trow = lax.broadcasted_iota(jnp.int32, (C, C), 0)
    tcol = lax.broadcasted_iota(jnp.int32, (C, C), 1)
    incl_d = [(col <= row).astype(F32), (col >= row).astype(F32)]
    strict_d = [(col < row).astype(F32), (col > row).astype(F32)]
    tri_d = [(tcol <= trow).astype(BF16), (tcol >= trow).astype(BF16)]

    b16 = lambda x: x.astype(BF16)
    bd = lambda xb: jnp.concatenate([xb * m for m in head_mask], axis=0)
    cat = lambda a, b, axis: jnp.concatenate([a, b], axis=axis)

    dir_refs = [(rf_ref, vf_ref, af_ref, lwf_ref, kdf_ref, bbf_ref, yf_ref),
                (rb_ref, vb_ref, ab_ref, lwb_ref, kdb_ref, bbb_ref, yb_ref)]
    chains = []
    at_g, rt_g, v_g, bt_g, kt_g, bh_g, kh_g, pt_g = [], [], [], [], [], [], [], []
    for d, (r_ref, v_ref, an_ref, lw_ref, kd_ref, bb_ref, _) in enumerate(dir_refs):
        for b in range(bsz):
            lw = lw_ref[b]
            cs = _dot_exact_lhs(tri_d[d], lw)
            tot = jnp.sum(lw, axis=0, keepdims=True)
            e_inv = jnp.exp(-cs)
            p_tot = jnp.exp(tot)
            rt = r_ref[b] * jnp.exp(cs)
            at = an_ref[b] * jnp.exp(cs - lw)
            bt = bb_ref[b] * e_inv
            kt = kd_ref[b] * e_inv
            bh = bt * p_tot
            kh = kt * p_tot
            vv = v_ref[b]
            for g in range(NG):
                sl = slice(g * GW, (g + 1) * GW)
                chains.append((d, b, g))
                for lst, val in ((at_g, at), (rt_g, rt), (v_g, vv), (bt_g, bt), (kt_g, kt), (bh_g, bh), (kh_g, kh),
                                 (pt_g, p_tot)):
                    lst.append(val[:, sl])
    strict = [strict_d[d] for d, _, _ in chains]
    incl = [incl_d[d] for d, _, _ in chains]

    at_b = [b16(a) for a in at_g]
    v_b = [b16(v) for v in v_g]
    ar_b = [cat(a, b16(r), 0) for a, r in zip(at_b, rt_g)]
    sbk = [_dot(ar, cat(bd(b16(xb)), bd(b16(xk)), 0), NT) for ar, xb, xk in zip(ar_b, bt_g, kt_g)]
    sb = [x[:, :GW] for x in sbk]
    sk = [x[:, GW:] for x in sbk]
    m_ab = [x[:C] * m for x, m in zip(sb, strict)]
    n_rb = [x[C:] * m for x, m in zip(sb, incl)]
    m_ak = [x[:C] * m for x, m in zip(sk, strict)]
    n_rk = [x[C:] * m for x, m in zip(sk, incl)]
    v_bd = [bd(v) for v in v_b]
    mv = [_dot(b16(m), vb) for m, vb in zip(m_ak, v_bd)]

    x = [eye + m * same_blk[0] for m in m_ab]
    for lvl in range(len(same_blk)):
        inner = same_blk[lvl]
        outer = same_blk[lvl + 1] if lvl + 1 < len(same_blk) else 1.0
        sel = outer - inner
        if lvl in side_work:
            done = side_work[lvl](_zero_after(x[0][:SUBLANES]))
        x_b = [b16(xi) for xi in x]
        t = [_dot(xb, bd(b16(m * sel))) for xb, m in zip(x_b, m_ab)]
        x = [xi + _dot(b16(ti), bd(xb)) for xi, ti, xb in zip(x, t, x_b)]
        if lvl in side_work:
            hold = jnp.tile(done, (C // SUBLANES, 1))
            x = [xi + hold for xi in x]

    w_ab = [_dot(b16(xi), cat(bd(a), bd(b16(m)), 1)) for xi, a, m in zip(x, at_b, mv)]
    w_a_b = [b16(w[:, :GW]) for w in w_ab]
    w_b_b = [b16(w[:, GW:]) for w in w_ab]
    zero_bd = jnp.zeros((GW, GW), BF16)
    yqc = [_dot(cat(b16(n), b16(nk), 1), cat(cat(bd(wa), bd(wb), 1), cat(zero_bd, vb, 1), 0))
           for n, nk, wa, wb, vb in zip(n_rb, n_rk, w_a_b, w_b_b, v_bd)]
    y_q = [r + x[:, :GW] for r, x in zip(rt_g, yqc)]
    y_c = [x[:, GW:] for x in yqc]
    zero_c = jnp.zeros((C, GW), BF16)
    php = [_dot(cat(cat(wa, wb, 1), cat(zero_c, v, 1), 0), cat(b16(b), b16(k), 0), TN)
           for wa, wb, v, b, k in zip(w_a_b, w_b_b, v_b, bh_g, kh_g)]
    phi = [x[:GW] * state_mask for x in php]
    psi = [x[GW:] * state_mask for x in php]
    for i, (d, b, g) in enumerate(chains):
        s0 = s_ref[d, b, g]
        s0_b = b16(s0)
        dir_refs[d][-1][b, :, g * GW:(g + 1) * GW] = _dot(b16(y_q[i]), s0_b, NT) + y_c[i]
        s_ref[d, b, g] = s0 * pt_g[i] + _dot(s0_b, b16(phi[i])) + psi[i]


def _attn_bias_init(slope_ref, bias_ref):
    rows2 = 2 * BLOCK
    qi = lax.broadcasted_iota(jnp.int32, (rows2, 3 * BLOCK), 0) % BLOCK
    kpos = lax.broadcasted_iota(jnp.int32, (rows2, 3 * BLOCK), 1) - BLOCK
    upper = lax.broadcasted_iota(jnp.int32, (rows2, 3 * BLOCK), 0) >= BLOCK
    dist = jnp.abs(qi - kpos)
    for p in range(ATTN_Q_HEADS // 2):
        slope = jnp.where(upper, slope_ref[2 * p + 1], slope_ref[2 * p])
        bias_ref[p] = jnp.where(dist <= WINDOW, -slope * dist.astype(F32), NEG_INF)


def _attn_step(n, nb, q_ref, kp_ref, kc_ref, kn_ref, qg_ref, kg_ref, slope_ref, sink_ref, o_ref, bias_ref):
    pairs = ATTN_Q_HEADS // 2
    rows2 = 2 * BLOCK
    hq = _head_sum_matrix(ATTN_WIDTH)
    hk = _head_sum_matrix(KV_WIDTH)
    lane = lax.broadcasted_iota(jnp.int32, (BLOCK, LANES), 1)
    low = lane < HEAD
    srow = lax.broadcasted_iota(jnp.int32, (rows2, 1), 0)

    q = q_ref[:, :ATTN_WIDTH]
    q = q * lax.rsqrt(_head_sums(q * q, hq) * (1.0 / HEAD) + RMS_EPS) * (qg_ref[...] * (HEAD ** -0.5))

    def dup(x, g):
        rolled = pltpu.roll(x, HEAD, axis=1)
        return (jnp.where(low, x, rolled) if g == 0 else jnp.where(low, rolled, x)).astype(BF16)

    k_dup, v_dup = [[], []], [[], []]
    for ref in (kp_ref, kc_ref, kn_ref):
        kx = ref[:, :KV_WIDTH]
        kx = kx * lax.rsqrt(_head_sums(kx * kx, hk) * (1.0 / HEAD) + RMS_EPS) * kg_ref[...]
        vx = ref[:, KV_WIDTH:]
        for g in range(ATTN_KV_HEADS):
            k_dup[g].append(dup(kx, g))
            v_dup[g].append(dup(vx, g))
    k_cat = [jnp.concatenate(k_dup[g], axis=0) for g in range(ATTN_KV_HEADS)]
    v_cat = [jnp.concatenate(v_dup[g], axis=0) for g in range(ATTN_KV_HEADS)]
    edge_prev = jnp.where(n > 0, 0.0, NEG_INF)
    edge_next = jnp.where(n < nb - 1, 0.0, NEG_INF)

    def head_pair(p, start):
        g = (2 * p) // ATTN_GROUP
        qp = q[:, p * LANES:(p + 1) * LANES] + jnp.tile(start, (BLOCK // SUBLANES, 1))
        qs = jnp.concatenate([jnp.where(low, qp, 0.0), jnp.where(low, 0.0, qp)], axis=0).astype(BF16)
        s = _dot(qs, k_cat[g], NT) + bias_ref[p]
        s0 = s[:, :BLOCK] + edge_prev
        s1 = s[:, BLOCK:2 * BLOCK]
        s2 = s[:, 2 * BLOCK:] + edge_next
        sink = jnp.where(srow >= BLOCK, sink_ref[2 * p + 1], sink_ref[2 * p])
        m = jnp.maximum(jnp.max(jnp.maximum(jnp.maximum(s0, s1), s2), axis=-1, keepdims=True), sink)
        e0, e1, e2 = jnp.exp(s0 - m), jnp.exp(s1 - m), jnp.exp(s2 - m)
        den = jnp.sum(e0 + e1 + e2, axis=-1, keepdims=True) + jnp.exp(sink - m)
        e = jnp.concatenate([e0, e1, e2], axis=1).astype(BF16)
        o2 = _dot(e, v_cat[g]) * (1.0 / den)
        o = jnp.where(low, o2[:BLOCK], o2[BLOCK:])
        z_lo = ATTN_WIDTH + 2 * KV_WIDTH + p * LANES
        zp = q_ref[:, z_lo:z_lo + LANES]
        og = o * (zp * _sigmoid(zp))
        o_ref[:, p * LANES:(p + 1) * LANES] = og.astype(o_ref.dtype)
        return _zero_after(og[:SUBLANES])

    return head_pair


N_WKV_IN = 12
N_ATTN_IN = 8
ATTN_JOIN_LEVELS = (1, 2, 3, 4)


def _wkv_attn_body(*refs, nb):
    wkv_in = refs[:N_WKV_IN]
    attn_in = refs[N_WKV_IN:N_WKV_IN + N_ATTN_IN]
    yf_ref, yb_ref, o_ref, s_ref, bias_ref = refs[N_WKV_IN + N_ATTN_IN:]
    c = pl.program_id(0)

    @pl.when(c == 0)
    def _():
        s_ref[...] = jnp.zeros_like(s_ref)
        _attn_bias_init(attn_in[6], bias_ref)

    head_pair = _attn_step(c % nb, nb, *attn_in, o_ref, bias_ref)
    for p in range(len(ATTN_JOIN_LEVELS)):
        head_pair(p, jnp.zeros((SUBLANES, LANES), F32))
    _wkv_step({}, *wkv_in, yf_ref, yb_ref, s_ref)


def _wkv_attn(r, v, an, lw, kd, bb, ub, q_g, k_g, slopes, sink):
    bsz, T, W = r.shape
    C = CHUNK
    nc = T // C
    nb = T // BLOCK
    assert nc == bsz * nb
    gw = WKV_GROUP * HEAD
    fwd = pl.BlockSpec((bsz, C, W), lambda c: (0, c, 0))
    bwd = pl.BlockSpec((bsz, C, W), lambda c: (0, nc - 1 - c, 0))
    fwd_d = pl.BlockSpec((None, bsz, C, W), lambda c: (0, 0, c, 0))
    bwd_d = pl.BlockSpec((None, bsz, C, W), lambda c: (1, 0, nc - 1 - c, 0))
    kv_blk = ATTN_WIDTH // (2 * KV_WIDTH)
    kv = lambda f: pl.BlockSpec((None, BLOCK, 2 * KV_WIDTH), f)
    smem = pl.BlockSpec(memory_space=pltpu.SMEM)
    attn_specs = [
        pl.BlockSpec((None, BLOCK, B_COLS), lambda c: (c // nb, c % nb, 0)),
        kv(lambda c: (c // nb, jnp.maximum(c % nb - 1, 0), kv_blk)),
        kv(lambda c: (c // nb, c % nb, kv_blk)),
        kv(lambda c: (c // nb, jnp.minimum(c % nb + 1, nb - 1), kv_blk)),
        pl.BlockSpec((1, ATTN_WIDTH), lambda c: (0, 0)),
        pl.BlockSpec((1, KV_WIDTH), lambda c: (0, 0)),
        smem, smem,
    ]
    out = jax.ShapeDtypeStruct((bsz, T, W), F32)
    return pl.pallas_call(
        functools.partial(_wkv_attn_body, nb=nb),
        grid=(nc,),
        in_specs=[fwd, fwd, fwd, bwd, bwd, bwd, fwd_d, fwd_d, fwd_d, bwd_d, bwd_d, bwd_d] + attn_specs,
        out_specs=[fwd, bwd, pl.BlockSpec((None, BLOCK, ATTN_WIDTH), lambda c: (c // nb, c % nb, 0))],
        out_shape=[out, out, jax.ShapeDtypeStruct((bsz, T, ATTN_WIDTH), BF16)],
        scratch_shapes=[pltpu.VMEM((2, bsz, RWKV_HEADS // WKV_GROUP, gw, gw), F32),
                        pltpu.VMEM((ATTN_Q_HEADS // 2, 2 * BLOCK, 3 * BLOCK), F32)],
        compiler_params=_cparams(("arbitrary",)),
        name="wkv_attn",
    )(r, v, an, r, v, an, lw, kd, bb, lw, kd, bb, ub, ub, ub, ub, q_g, k_g, slopes, sink)


def _merge_out_body(x_ref, yf_ref, yb_ref, bonus_ref, gz_ref, ob_ref, ug_ref, p_ref, lnw_ref, lnb_ref,
                    pa_ref, pb_ref, wo_ref, pg_ref, gw_ref, pp_ref, o_ref):
    hsum = _head_sum_matrix(RWKV_WIDTH)
    y = yf_ref[...] + yb_ref[...]
    mean = _dot_split2_rhs(y, hsum) * (1.0 / HEAD)
    yc = y - mean
    var = _head_sums(yc * yc, hsum) * (1.0 / HEAD)
    yn = yc * lax.rsqrt(var + GN_EPS) * lnw_ref[...] + lnb_ref[...]
    o_a = (yn + bonus_ref[...]) * gz_ref[...]
    y_a = _dot(o_a.astype(BF16), pa_ref[...])
    y_b = _dot(ob_ref[...], pb_ref[...])
    merged = (_sigmoid(ug_ref[:, :D_MODEL].astype(F32)) * y_a
              + _sigmoid(ug_ref[:, D_MODEL:].astype(F32)) * y_b)
    x1 = x_ref[...] + _dot(merged.astype(BF16), wo_ref[...])
    ple = _dot(p_ref[...].astype(BF16), pp_ref[...])
    hn = x1 * lax.rsqrt(jnp.mean(x1 * x1, axis=-1, keepdims=True) + RMS_EPS) * pg_ref[...]
    o_ref[...] = x1 + _sigmoid(_dot(hn.astype(BF16), gw_ref[...])) * ple


def _merge_out(x2, yf, yb, bonus, gz, ob, ug, p3, layer, lnw, lnb, pa, pb, wo, pg, gw, pp, tm=MERGE_TILE):
    rows = x2.shape[0]
    W = RWKV_WIDTH
    tile = lambda n: pl.BlockSpec((tm, n), lambda i: (i, 0))
    full = lambda a: pl.BlockSpec(a.shape, lambda i: (0,) * a.ndim)
    return pl.pallas_call(
        _merge_out_body,
        grid=(rows // tm,),
        in_specs=[tile(D_MODEL), tile(W), tile(W), tile(W), tile(W), tile(ATTN_WIDTH),
                  tile(G_COLS), pl.BlockSpec((None, tm, PLE_DIM), lambda i: (layer, i, 0)), full(lnw), full(lnb), full(pa), full(pb), full(wo), full(pg),
                  full(gw), full(pp)],
        out_specs=tile(D_MODEL),
        out_shape=jax.ShapeDtypeStruct((rows, D_MODEL), F32),
        compiler_params=_cparams(("parallel",)),
        name="merge_out",
    )(x2, yf, yb, bonus, gz, ob, ug, p3, lnw, lnb, pa, pb, wo, pg, gw, pp)


def _block_diag2(m):
    z = jnp.zeros_like(m[0])
    return jnp.concatenate([jnp.concatenate([m[0], z], axis=1), jnp.concatenate([z, m[1]], axis=1)], axis=0)


def kernel(x, p, norm_g, w_in, shift_mu, decay_w0, decay_up, iclr_a0, iclr_up, vres_down, vres_up, vres_v0, k_k, k_a, r_k, ln_x_w, ln_x_b, q_norm_g, k_norm_g, sink, proj_a, proj_b, w_out, ple_norm_g, ple_gate_w, ple_proj):
    bsz, T, _ = x.shape
    depth = w_in.shape[0]
    rows = bsz * T
    W = RWKV_WIDTH
    slopes = jnp.asarray(2.0 ** (-8.0 * jnp.arange(1, ATTN_Q_HEADS + 1, dtype=F32) / ATTN_Q_HEADS), F32)
    v_first = None
    for i in range(depth):
        params = [shift_mu[i].reshape(1, A_COLS), decay_w0[i].reshape(1, 2 * W), _block_diag2(decay_up[i]).astype(BF16),
                  iclr_a0[i].reshape(1, 2 * W), _block_diag2(iclr_up[i]).astype(BF16), k_k[i].reshape(1, W),
                  k_a[i].reshape(1, W), r_k[i].reshape(1, W)]
        vres = None
        if i > 0:
            vu_pad = jnp.concatenate([vres_up[i - 1], jnp.zeros((LANES - VRES_RANK, W), F32)], axis=0)
            vd_pad = jnp.concatenate([vres_down[i - 1], jnp.zeros((D_MODEL, LANES - VRES_RANK), F32)], axis=1)
            vres = (vd_pad.astype(BF16), vu_pad.astype(BF16), vres_v0[i - 1].reshape(1, W), v_first)
        ub, ug, r, v, an, lw, kd, bb, bonus, gz = _proj_prep(x, norm_g[i].reshape(1, D_MODEL), w_in[i].astype(BF16),
                                                             params, vres)
        if i == 0:
            v_first = v
        yf, yb, ob = _wkv_attn(r, v, an, lw, kd, bb, ub, jnp.tile(q_norm_g[i], ATTN_Q_HEADS).reshape(1, ATTN_WIDTH),
                               jnp.tile(k_norm_g[i], ATTN_KV_HEADS).reshape(1, KV_WIDTH), slopes, sink[i])
        x2 = _merge_out(x.reshape(rows, D_MODEL), yf.reshape(rows, W), yb.reshape(rows, W), bonus.reshape(rows, W),
                        gz.reshape(rows, W), ob.reshape(rows, ATTN_WIDTH), ug.reshape(rows, G_COLS),
                        p.reshape(depth, rows, PLE_DIM), i, ln_x_w[i].reshape(1, W), ln_x_b[i].reshape(1, W),
                        proj_a[i].astype(BF16), proj_b[i].astype(BF16), w_out[i].astype(BF16),
                        ple_norm_g[i].reshape(1, D_MODEL), ple_gate_w[i].astype(BF16), ple_proj[i].astype(BF16))
        x = x2.reshape(bsz, T, D_MODEL)
    return x
```

```python
import functools

import jax
import jax.numpy as jnp
from jax import lax
from jax.experimental import pallas as pl
from jax.experimental.pallas import tpu as pltpu

F32 = jnp.float32
BF16 = jnp.bfloat16

D_MODEL = 1024
PLE_DIM = 256
RWKV_WIDTH = 512
HEAD = 64
RWKV_HEADS = RWKV_WIDTH // HEAD
LOW_RANK = 64
VRES_RANK = 32
ATTN_WIDTH = 512
ATTN_Q_HEADS = ATTN_WIDTH // HEAD
ATTN_KV_HEADS = 2
ATTN_GROUP = ATTN_Q_HEADS // ATTN_KV_HEADS
KV_WIDTH = ATTN_KV_HEADS * HEAD
WINDOW = 128
BLOCK = 128
RMS_EPS = 1e-6
GN_EPS = 64e-5
NEG_INF = -1e30
EXP_MINUS_HALF = 0.6065306597126334
A_COLS = 4 * RWKV_WIDTH + 4 * LOW_RANK
B_COLS = 2 * ATTN_WIDTH + 2 * KV_WIDTH
G_COLS = 2 * D_MODEL
LANES = 128
SUBLANES = 8
HALO = 16
CHUNK = 64
INV_BASE = 2
WKV_GROUP = 2
PROJ_TILE = 256
MERGE_TILE = 512
VMEM_LIMIT = 48 * 1024 * 1024

NN = ((1,), (0,))
NT = ((1,), (1,))
TN = ((0,), (0,))


def _dot(a, b, dims=NN):
    return lax.dot_general(a, b, (dims, ((), ())), preferred_element_type=F32)


def _split2(x):
    hi = x.astype(BF16)
    lo = (x - hi.astype(F32)).astype(BF16)
    return hi, lo


def _dot_exact_lhs(a_bf16, b, dims=NN):
    b1 = b.astype(BF16)
    r1 = b - b1.astype(F32)
    b2 = r1.astype(BF16)
    b3 = (r1 - b2.astype(F32)).astype(BF16)
    return _dot(a_bf16, b1, dims) + (_dot(a_bf16, b2, dims) + _dot(a_bf16, b3, dims))


def _dot_split2_rhs(a, b_bf16):
    hi, lo = _split2(a)
    return _dot(hi, b_bf16) + _dot(lo, b_bf16)


def _head_sums(a, hsum_bf16):
    return _dot(a.astype(BF16), hsum_bf16)


def _sigmoid(x):
    return 1.0 / (1.0 + jnp.exp(-x))


def _head_sum_matrix(width):
    r = lax.broadcasted_iota(jnp.int32, (width, width), 0) // HEAD
    c = lax.broadcasted_iota(jnp.int32, (width, width), 1) // HEAD
    return (r == c).astype(BF16)


def _cparams(sem):
    return pltpu.CompilerParams(dimension_semantics=sem, vmem_limit_bytes=VMEM_LIMIT)


def _proj_prep_body(*refs, has_vres):
    if has_vres:
        (x_ref, xp_ref, xn_ref, g_ref, w_ref, mu_ref, w0_ref, dup_ref, a0_ref, iup_ref, kk_ref, ka_ref,
         rk_ref, vd_ref, vu_ref, v0_ref, vf_ref,
         ub_o, ug_o, r_o, v_o, an_o, lw_o, kd_o, bb_o, bonus_o, gz_o, hb_ref, ua_ref) = refs
    else:
        (x_ref, xp_ref, xn_ref, g_ref, w_ref, mu_ref, w0_ref, dup_ref, a0_ref, iup_ref, kk_ref, ka_ref,
         rk_ref,
         ub_o, ug_o, r_o, v_o, an_o, lw_o, kd_o, bb_o, bonus_o, gz_o, hb_ref, ua_ref) = refs
    tm = x_ref.shape[0]
    i = pl.program_id(1)
    last = pl.num_programs(1) - 1
    W = RWKV_WIDTH

    def normed(ref):
        x = ref[...]
        return (x * lax.rsqrt(jnp.mean(x * x, axis=-1, keepdims=True) + RMS_EPS) * g_ref[...]).astype(BF16)

    hb_ref[0:HALO] = jnp.where(i > 0, normed(xp_ref), jnp.zeros((), BF16))
    hb_ref[HALO:HALO + tm] = normed(x_ref)
    hb_ref[HALO + tm:] = jnp.where(i < last, normed(xn_ref), jnp.zeros((), BF16))
    ua_ref[...] = _dot(hb_ref[...], w_ref[:, :A_COLS])
    hm = hb_ref[HALO:HALO + tm]

    def project_b(lo, hi):
        ub_o[:, lo:hi] = _dot(hm, w_ref[:, A_COLS + lo:A_COLS + hi])

    def project_g(lo, hi):
        ug_o[:, lo:hi] = _dot(hm, w_ref[:, A_COLS + B_COLS + lo:A_COLS + B_COLS + hi]).astype(ug_o.dtype)

    hsum = _head_sum_matrix(W)

    def shifted(lo, hi):
        u = ua_ref[HALO:HALO + tm, lo:hi]
        ext = ua_ref[:, lo:hi]
        prev = pltpu.roll(ext, 1, axis=0)[HALO:HALO + tm]
        nxt = pltpu.roll(ext, tm + 2 * HALO - 1, axis=0)[HALO:HALO + tm]
        return u + mu_ref[:, lo:hi] * (0.5 * (prev + nxt) - u)

    r = shifted(0, W)
    project_b(0, ATTN_WIDTH)
    k = shifted(W, 2 * W)
    project_b(ATTN_WIDTH, B_COLS)
    v = shifted(2 * W, 3 * W)
    project_g(0, D_MODEL // 2)
    z = shifted(3 * W, 4 * W)
    project_g(D_MODEL // 2, D_MODEL)
    low = shifted(4 * W, 4 * W + 4 * LOW_RANK)
    w_raw = w0_ref[...] + _dot(jnp.tanh(low[:, :2 * LOW_RANK]).astype(BF16), dup_ref[...])
    project_g(D_MODEL, D_MODEL + D_MODEL // 2)
    lw = -EXP_MINUS_HALF * _sigmoid(w_raw)
    a = _sigmoid(a0_ref[...] + _dot(low[:, 2 * LOW_RANK:].astype(BF16), iup_ref[...]))
    if has_vres:
        hd = _dot(hm, vd_ref[...])
        mix = _sigmoid(v0_ref[...] + _dot(hd.astype(BF16), vu_ref[...]))
        v = v + (vf_ref[...] - v) * mix
    project_g(D_MODEL + D_MODEL // 2, G_COLS)
    kk = k * kk_ref[...]
    ss = _head_sums(kk * kk, hsum)
    kk = kk * lax.rsqrt(jnp.maximum(ss, 1e-24))
    ka = ka_ref[...]
    ksum = jnp.zeros_like(k)
    for d in range(2):
        a_d = a[:, d * W:(d + 1) * W]
        kd = k * (1.0 + (a_d - 1.0) * ka)
        ksum = ksum + kd
        lw_o[d] = lw[:, d * W:(d + 1) * W]
        kd_o[d] = kd
        bb_o[d] = kk * a_d
    r_o[...] = r
    v_o[...] = v
    an_o[...] = -kk
    bonus_o[...] = _head_sums(r * ksum * rk_ref[...], hsum) * v
    gz_o[...] = z * _sigmoid(z)


def _proj_prep(x, g, w, params, vres, tm=PROJ_TILE):
    bsz, T, _ = x.shape
    W = RWKV_WIDTH
    nt = T // tm
    hb = tm // HALO
    nh = T // HALO
    const = lambda a: pl.BlockSpec(a.shape, lambda b, i: (0,) * a.ndim, pipeline_mode=pl.Buffered(1))
    tile = lambda n: pl.BlockSpec((None, tm, n), lambda b, i: (b, i, 0))
    in_specs = [
        tile(D_MODEL),
        pl.BlockSpec((None, HALO, D_MODEL), lambda b, i: (b, jnp.maximum(i * hb - 1, 0), 0)),
        pl.BlockSpec((None, HALO, D_MODEL), lambda b, i: (b, jnp.minimum((i + 1) * hb, nh - 1), 0)),
        const(g), const(w),
    ] + [const(a) for a in params]
    args = [x, x, x, g, w] + list(params)
    if vres is not None:
        vd_pad, vu_split, v0, v_first = vres
        in_specs += [const(vd_pad), const(vu_split), const(v0), tile(W)]
        args += [vd_pad, vu_split, v0, v_first]
    dir_tile = pl.BlockSpec((2, None, tm, W), lambda b, i: (0, b, i, 0))
    out_specs = [tile(B_COLS), tile(G_COLS), tile(W), tile(W), tile(W), dir_tile, dir_tile, dir_tile, tile(W), tile(W)]
    one = jax.ShapeDtypeStruct((bsz, T, W), F32)
    two = jax.ShapeDtypeStruct((2, bsz, T, W), F32)
    out_shape = [jax.ShapeDtypeStruct((bsz, T, B_COLS), F32), jax.ShapeDtypeStruct((bsz, T, G_COLS), BF16),
                 one, one, one, two, two, two, one, one]
    return pl.pallas_call(
        functools.partial(_proj_prep_body, has_vres=vres is not None),
        grid=(bsz, nt),
        in_specs=in_specs,
        out_specs=out_specs,
        out_shape=out_shape,
        scratch_shapes=[pltpu.VMEM((tm + 2 * HALO, D_MODEL), BF16), pltpu.VMEM((tm + 2 * HALO, A_COLS), F32)],
        compiler_params=_cparams(("parallel", "parallel")),
        name="proj_prep",
    )(*args)


def _wkv_step(rf_ref, vf_ref, af_ref, rb_ref, vb_ref, ab_ref, lwf_ref, kdf_ref, bbf_ref,
              lwb_ref, kdb_ref, bbb_ref, yf_ref, yb_ref, s_ref):
    bsz, C, _ = rf_ref.shape
    G = WKV_GROUP
    GW = G * HEAD
    NG = RWKV_HEADS // G
    assert C == HEAD

    row = lax.broadcasted_iota(jnp.int32, (C, GW), 0)
    lane = lax.broadcasted_iota(jnp.int32, (C, GW), 1)
    col = lane % HEAD
    eye = (row == col).astype(F32)
    same_blk = []
    size = INV_BASE
    while size < C:
        same_blk.append(((row // size) == (col // size)).astype(F32))
        size *= 2
    head_mask = [((lane // HEAD) == g).astype(BF16) for g in range(G)]
    srow = lax.broadcasted_iota(jnp.int32, (GW, GW), 0) // HEAD
    slane = lax.broadcasted_iota(jnp.int32, (GW, GW), 1) // HEAD
    state_mask = (srow == slane).astype(F32)
    trow = lax.broadcasted_iota(jnp.int32, (C, C), 0)
    tcol = lax.broadcasted_iota(jnp.int32, (C, C), 1)
    incl_d = [(col <= row).astype(F32), (col >= row).astype(F32)]
    strict_d = [(col < row).astype(F32), (col > row).astype(F32)]
    tri_d = [(tcol <= trow).astype(BF16), (tcol >= trow).astype(BF16)]

    b16 = lambda x: x.astype(BF16)
    bd = lambda xb: jnp.concatenate([xb * m for m in head_mask], axis=0)
    cat = lambda a, b, axis: jnp.concatenate([a, b], axis=axis)

    dir_refs = [(rf_ref, vf_ref, af_ref, lwf_ref, kdf_ref, bbf_ref, yf_ref),
                (rb_ref, vb_ref, ab_ref, lwb_ref, kdb_ref, bbb_ref, yb_ref)]
    chains = []
    at_g, rt_g, v_g, bt_g, kt_g, bh_g, kh_g, pt_g = [], [], [], [], [], [], [], []
    for d, (r_ref, v_ref, an_ref, lw_ref, kd_ref, bb_ref, _) in enumerate(dir_refs):
        for b in range(bsz):
            lw = lw_ref[b]
            cs = _dot_exact_lhs(tri_d[d], lw)
            tot = jnp.sum(lw, axis=0, keepdims=True)
            e_inv = jnp.exp(-cs)
            p_tot = jnp.exp(tot)
            rt = r_ref[b] * jnp.exp(cs)
            at = an_ref[b] * jnp.exp(cs - lw)
            bt = bb_ref[b] * e_inv
            kt = kd_ref[b] * e_inv
            bh = bt * p_tot
            kh = kt * p_tot
            vv = v_ref[b]
            for g in range(NG):
                sl = slice(g * GW, (g + 1) * GW)
                chains.append((d, b, g))
                for lst, val in ((at_g, at), (rt_g, rt), (v_g, vv), (bt_g, bt), (kt_g, kt), (bh_g, bh), (kh_g, kh),
                                 (pt_g, p_tot)):
                    lst.append(val[:, sl])
    strict = [strict_d[d] for d, _, _ in chains]
    incl = [incl_d[d] for d, _, _ in chains]

    at_b = [b16(a) for a in at_g]
    v_b = [b16(v) for v in v_g]
    ar_b = [cat(a, b16(r), 0) for a, r in zip(at_b, rt_g)]
    sbk = [_dot(ar, cat(bd(b16(xb)), bd(b16(xk)), 0), NT) for ar, xb, xk in zip(ar_b, bt_g, kt_g)]
    sb = [x[:, :GW] for x in sbk]
    sk = [x[:, GW:] for x in sbk]
    m_ab = [x[:C] * m for x, m in zip(sb, strict)]
    n_rb = [x[C:] * m for x, m in zip(sb, incl)]
    m_ak = [x[:C] * m for x, m in zip(sk, strict)]
    n_rk = [x[C:] * m for x, m in zip(sk, incl)]
    v_bd = [bd(v) for v in v_b]
    mv = [_dot(b16(m), vb) for m, vb in zip(m_ak, v_bd)]

    x = [eye + m * same_blk[0] for m in m_ab]
    for lvl in range(len(same_blk)):
        inner = same_blk[lvl]
        outer = same_blk[lvl + 1] if lvl + 1 < len(same_blk) else 1.0
        sel = outer - inner
        x_b = [b16(xi) for xi in x]
        t = [_dot(xb, bd(b16(m * sel))) for xb, m in zip(x_b, m_ab)]
        x = [xi + _dot(b16(ti), bd(xb)) for xi, ti, xb in zip(x, t, x_b)]

    w_ab = [_dot(b16(xi), cat(bd(a), bd(b16(m)), 1)) for xi, a, m in zip(x, at_b, mv)]
    w_a_b = [b16(w[:, :GW]) for w in w_ab]
    w_b_b = [b16(w[:, GW:]) for w in w_ab]
    zero_bd = jnp.zeros((GW, GW), BF16)
    yqc = [_dot(cat(b16(n), b16(nk), 1), cat(cat(bd(wa), bd(wb), 1), cat(zero_bd, vb, 1), 0))
           for n, nk, wa, wb, vb in zip(n_rb, n_rk, w_a_b, w_b_b, v_bd)]
    y_q = [r + x[:, :GW] for r, x in zip(rt_g, yqc)]
    y_c = [x[:, GW:] for x in yqc]
    zero_c = jnp.zeros((C, GW), BF16)
    php = [_dot(cat(cat(wa, wb, 1), cat(zero_c, v, 1), 0), cat(b16(b), b16(k), 0), TN)
           for wa, wb, v, b, k in zip(w_a_b, w_b_b, v_b, bh_g, kh_g)]
    phi = [x[:GW] * state_mask for x in php]
    psi = [x[GW:] * state_mask for x in php]
    for i, (d, b, g) in enumerate(chains):
        s0 = s_ref[d, b, g]
        s0_b = b16(s0)
        dir_refs[d][-1][b, :, g * GW:(g + 1) * GW] = _dot(b16(y_q[i]), s0_b, NT) + y_c[i]
        s_ref[d, b, g] = s0 * pt_g[i] + _dot(s0_b, b16(phi[i])) + psi[i]


def _attn_bias_init(slope_ref, bias_ref):
    rows2 = 2 * BLOCK
    qi = lax.broadcasted_iota(jnp.int32, (rows2, 3 * BLOCK), 0) % BLOCK
    kpos = lax.broadcasted_iota(jnp.int32, (rows2, 3 * BLOCK), 1) - BLOCK
    upper = lax.broadcasted_iota(jnp.int32, (rows2, 3 * BLOCK), 0) >= BLOCK
    dist = jnp.abs(qi - kpos)
    for p in range(ATTN_Q_HEADS // 2):
        slope = jnp.where(upper, slope_ref[2 * p + 1], slope_ref[2 * p])
        bias_ref[p] = jnp.where(dist <= WINDOW, -slope * dist.astype(F32), NEG_INF)


def _attn_step(n, nb, q_ref, kp_ref, kc_ref, kn_ref, qg_ref, kg_ref, slope_ref, sink_ref, o_ref, bias_ref):
    pairs = ATTN_Q_HEADS // 2
    rows2 = 2 * BLOCK
    hq = _head_sum_matrix(ATTN_WIDTH)
    hk = _head_sum_matrix(KV_WIDTH)
    lane = lax.broadcasted_iota(jnp.int32, (BLOCK, LANES), 1)
    low = lane < HEAD
    srow = lax.broadcasted_iota(jnp.int32, (rows2, 1), 0)

    q = q_ref[:, :ATTN_WIDTH]
    q = q * lax.rsqrt(_head_sums(q * q, hq) * (1.0 / HEAD) + RMS_EPS) * (qg_ref[...] * (HEAD ** -0.5))

    def dup(x, g):
        rolled = pltpu.roll(x, HEAD, axis=1)
        return (jnp.where(low, x, rolled) if g == 0 else jnp.where(low, rolled, x)).astype(BF16)

    k_dup, v_dup = [[], []], [[], []]
    for ref in (kp_ref, kc_ref, kn_ref):
        kx = ref[:, :KV_WIDTH]
        kx = kx * lax.rsqrt(_head_sums(kx * kx, hk) * (1.0 / HEAD) + RMS_EPS) * kg_ref[...]
        vx = ref[:, KV_WIDTH:]
        for g in range(ATTN_KV_HEADS):
            k_dup[g].append(dup(kx, g))
            v_dup[g].append(dup(vx, g))
    k_cat = [jnp.concatenate(k_dup[g], axis=0) for g in range(ATTN_KV_HEADS)]
    v_cat = [jnp.concatenate(v_dup[g], axis=0) for g in range(ATTN_KV_HEADS)]
    edge_prev = jnp.where(n > 0, 0.0, NEG_INF)
    edge_next = jnp.where(n < nb - 1, 0.0, NEG_INF)

    for p in range(pairs):
        g = (2 * p) // ATTN_GROUP
        qp = q[:, p * LANES:(p + 1) * LANES]
        qs = jnp.concatenate([jnp.where(low, qp, 0.0), jnp.where(low, 0.0, qp)], axis=0).astype(BF16)
        s = _dot(qs, k_cat[g], NT) + bias_ref[p]
        s0 = s[:, :BLOCK] + edge_prev
        s1 = s[:, BLOCK:2 * BLOCK]
        s2 = s[:, 2 * BLOCK:] + edge_next
        sink = jnp.where(srow >= BLOCK, sink_ref[2 * p + 1], sink_ref[2 * p])
        m = jnp.maximum(jnp.max(jnp.maximum(jnp.maximum(s0, s1), s2), axis=-1, keepdims=True), sink)
        e0, e1, e2 = jnp.exp(s0 - m), jnp.exp(s1 - m), jnp.exp(s2 - m)
        den = jnp.sum(e0 + e1 + e2, axis=-1, keepdims=True) + jnp.exp(sink - m)
        e = jnp.concatenate([e0, e1, e2], axis=1).astype(BF16)
        o2 = _dot(e, v_cat[g]) * (1.0 / den)
        o = jnp.where(low, o2[:BLOCK], o2[BLOCK:])
        z_lo = ATTN_WIDTH + 2 * KV_WIDTH + p * LANES
        zp = q_ref[:, z_lo:z_lo + LANES]
        og = o * (zp * _sigmoid(zp))
        o_ref[:, p * LANES:(p + 1) * LANES] = og.astype(o_ref.dtype)


N_WKV_IN = 12
N_ATTN_IN = 8


def _wkv_attn_body(*refs, nb):
    wkv_in = refs[:N_WKV_IN]
    attn_in = refs[N_WKV_IN:N_WKV_IN + N_ATTN_IN]
    yf_ref, yb_ref, o_ref, s_ref, bias_ref = refs[N_WKV_IN + N_ATTN_IN:]
    c = pl.program_id(0)

    @pl.when(c == 0)
    def _():
        s_ref[...] = jnp.zeros_like(s_ref)
        _attn_bias_init(attn_in[6], bias_ref)

    _attn_step(c % nb, nb, *attn_in, o_ref, bias_ref)
    _wkv_step(*wkv_in, yf_ref, yb_ref, s_ref)


def _wkv_attn(r, v, an, lw, kd, bb, ub, q_g, k_g, slopes, sink):
    bsz, T, W = r.shape
    C = CHUNK
    nc = T // C
    nb = T // BLOCK
    assert nc == bsz * nb
    gw = WKV_GROUP * HEAD
    fwd = pl.BlockSpec((bsz, C, W), lambda c: (0, c, 0))
    bwd = pl.BlockSpec((bsz, C, W), lambda c: (0, nc - 1 - c, 0))
    fwd_d = pl.BlockSpec((None, bsz, C, W), lambda c: (0, 0, c, 0))
    bwd_d = pl.BlockSpec((None, bsz, C, W), lambda c: (1, 0, nc - 1 - c, 0))
    kv_blk = ATTN_WIDTH // (2 * KV_WIDTH)
    kv = lambda f: pl.BlockSpec((None, BLOCK, 2 * KV_WIDTH), f)
    smem = pl.BlockSpec(memory_space=pltpu.SMEM)
    attn_specs = [
        pl.BlockSpec((None, BLOCK, B_COLS), lambda c: (c // nb, c % nb, 0)),
        kv(lambda c: (c // nb, jnp.maximum(c % nb - 1, 0), kv_blk)),
        kv(lambda c: (c // nb, c % nb, kv_blk)),
        kv(lambda c: (c // nb, jnp.minimum(c % nb + 1, nb - 1), kv_blk)),
        pl.BlockSpec((1, ATTN_WIDTH), lambda c: (0, 0)),
        pl.BlockSpec((1, KV_WIDTH), lambda c: (0, 0)),
        smem, smem,
    ]
    out = jax.ShapeDtypeStruct((bsz, T, W), F32)
    return pl.pallas_call(
        functools.partial(_wkv_attn_body, nb=nb),
        grid=(nc,),
        in_specs=[fwd, fwd, fwd, bwd, bwd, bwd, fwd_d, fwd_d, fwd_d, bwd_d, bwd_d, bwd_d] + attn_specs,
        out_specs=[fwd, bwd, pl.BlockSpec((None, BLOCK, ATTN_WIDTH), lambda c: (c // nb, c % nb, 0))],
        out_shape=[out, out, jax.ShapeDtypeStruct((bsz, T, ATTN_WIDTH), BF16)],
        scratch_shapes=[pltpu.VMEM((2, bsz, RWKV_HEADS // WKV_GROUP, gw, gw), F32),
                        pltpu.VMEM((ATTN_Q_HEADS // 2, 2 * BLOCK, 3 * BLOCK), F32)],
        compiler_params=_cparams(("arbitrary",)),
        name="wkv_attn",
    )(r, v, an, r, v, an, lw, kd, bb, lw, kd, bb, ub, ub, ub, ub, q_g, k_g, slopes, sink)


def _merge_out_body(x_ref, yf_ref, yb_ref, bonus_ref, gz_ref, ob_ref, ug_ref, p_ref, lnw_ref, lnb_ref,
                    pa_ref, pb_ref, wo_ref, pg_ref, gw_ref, pp_ref, o_ref):
    hsum = _head_sum_matrix(RWKV_WIDTH)
    y = yf_ref[...] + yb_ref[...]
    mean = _dot_split2_rhs(y, hsum) * (1.0 / HEAD)
    yc = y - mean
    var = _head_sums(yc * yc, hsum) * (1.0 / HEAD)
    yn = yc * lax.rsqrt(var + GN_EPS) * lnw_ref[...] + lnb_ref[...]
    o_a = (yn + bonus_ref[...]) * gz_ref[...]
    y_a = _dot(o_a.astype(BF16), pa_ref[...])
    y_b = _dot(ob_ref[...], pb_ref[...])
    merged = (_sigmoid(ug_ref[:, :D_MODEL].astype(F32)) * y_a
              + _sigmoid(ug_ref[:, D_MODEL:].astype(F32)) * y_b)
    x1 = x_ref[...] + _dot(merged.astype(BF16), wo_ref[...])
    ple = _dot(p_ref[...].astype(BF16), pp_ref[...])
    hn = x1 * lax.rsqrt(jnp.mean(x1 * x1, axis=-1, keepdims=True) + RMS_EPS) * pg_ref[...]
    o_ref[...] = x1 + _sigmoid(_dot(hn.astype(BF16), gw_ref[...])) * ple


def _merge_out(x2, yf, yb, bonus, gz, ob, ug, p3, layer, lnw, lnb, pa, pb, wo, pg, gw, pp, tm=MERGE_TILE):
    rows = x2.shape[0]
    W = RWKV_WIDTH
    tile = lambda n: pl.BlockSpec((tm, n), lambda i: (i, 0))
    full = lambda a: pl.BlockSpec(a.shape, lambda i: (0,) * a.ndim)
    return pl.pallas_call(
        _merge_out_body,
        grid=(rows // tm,),
        in_specs=[tile(D_MODEL), tile(W), tile(W), tile(W), tile(W), tile(ATTN_WIDTH),
                  tile(G_COLS), pl.BlockSpec((None, tm, PLE_DIM), lambda i: (layer, i, 0)), full(lnw), full(lnb), full(pa), full(pb), full(wo), full(pg),
                  full(gw), full(pp)],
        out_specs=tile(D_MODEL),
        out_shape=jax.ShapeDtypeStruct((rows, D_MODEL), F32),
        compiler_params=_cparams(("parallel",)),
        name="merge_out",
    )(x2, yf, yb, bonus, gz, ob, ug, p3, lnw, lnb, pa, pb, wo, pg, gw, pp)


def _block_diag2(m):
    z = jnp.zeros_like(m[0])
    return jnp.concatenate([jnp.concatenate([m[0], z], axis=1), jnp.concatenate([z, m[1]], axis=1)], axis=0)


def kernel(x, p, norm_g, w_in, shift_mu, decay_w0, decay_up, iclr_a0, iclr_up, vres_down, vres_up, vres_v0, k_k, k_a, r_k, ln_x_w, ln_x_b, q_norm_g, k_norm_g, sink, proj_a, proj_b, w_out, ple_norm_g, ple_gate_w, ple_proj):
    bsz, T, _ = x.shape
    depth = w_in.shape[0]
    rows = bsz * T
    W = RWKV_WIDTH
    slopes = jnp.asarray(2.0 ** (-8.0 * jnp.arange(1, ATTN_Q_HEADS + 1, dtype=F32) / ATTN_Q_HEADS), F32)
    v_first = None
    for i in range(depth):
        params = [shift_mu[i].reshape(1, A_COLS), decay_w0[i].reshape(1, 2 * W), _block_diag2(decay_up[i]).astype(BF16),
                  iclr_a0[i].reshape(1, 2 * W), _block_diag2(iclr_up[i]).astype(BF16), k_k[i].reshape(1, W),
                  k_a[i].reshape(1, W), r_k[i].reshape(1, W)]
        vres = None
        if i > 0:
            vu_pad = jnp.concatenate([vres_up[i - 1], jnp.zeros((LANES - VRES_RANK, W), F32)], axis=0)
            vd_pad = jnp.concatenate([vres_down[i - 1], jnp.zeros((D_MODEL, LANES - VRES_RANK), F32)], axis=1)
            vres = (vd_pad.astype(BF16), vu_pad.astype(BF16), vres_v0[i - 1].reshape(1, W), v_first)
        ub, ug, r, v, an, lw, kd, bb, bonus, gz = _proj_prep(x, norm_g[i].reshape(1, D_MODEL), w_in[i].astype(BF16),
                                                             params, vres)
        if i == 0:
            v_first = v
        yf, yb, ob = _wkv_attn(r, v, an, lw, kd, bb, ub, jnp.tile(q_norm_g[i], ATTN_Q_HEADS).reshape(1, ATTN_WIDTH),
                               jnp.tile(k_norm_g[i], ATTN_KV_HEADS).reshape(1, KV_WIDTH), slopes, sink[i])
        x2 = _merge_out(x.reshape(rows, D_MODEL), yf.reshape(rows, W), yb.reshape(rows, W), bonus.reshape(rows, W),
                        gz.reshape(rows, W), ob.reshape(rows, ATTN_WIDTH), ug.reshape(rows, G_COLS),
                        p.reshape(depth, rows, PLE_DIM), i, ln_x_w[i].reshape(1, W), ln_x_b[i].reshape(1, W),
                        proj_a[i].astype(BF16), proj_b[i].astype(BF16), w_out[i].astype(BF16),
                        ple_norm_g[i].reshape(1, D_MODEL), ple_gate_w[i].astype(BF16), ple_proj[i].astype(BF16))
        x = x2.reshape(bsz, T, D_MODEL)
    return x
```

```python
import functools

import jax
import jax.numpy as jnp
from jax import lax
from jax.experimental import pallas as pl
from jax.experimental.pallas import tpu as pltpu

F32 = jnp.float32
BF16 = jnp.bfloat16

D_MODEL = 1024
PLE_DIM = 256
RWKV_WIDTH = 512
HEAD = 64
RWKV_HEADS = RWKV_WIDTH // HEAD
LOW_RANK = 64
VRES_RANK = 32
ATTN_WIDTH = 512
ATTN_Q_HEADS = ATTN_WIDTH // HEAD
ATTN_KV_HEADS = 2
ATTN_GROUP = ATTN_Q_HEADS // ATTN_KV_HEADS
KV_WIDTH = ATTN_KV_HEADS * HEAD
WINDOW = 128
BLOCK = 128
RMS_EPS = 1e-6
GN_EPS = 64e-5
NEG_INF = -1e30
EXP_MINUS_HALF = 0.6065306597126334
A_COLS = 4 * RWKV_WIDTH + 4 * LOW_RANK
B_COLS = 2 * ATTN_WIDTH + 2 * KV_WIDTH
G_COLS = 2 * D_MODEL
LANES = 128
SUBLANES = 8
HALO = 16
CHUNK = 64
INV_BASE = 2
WKV_GROUP = 2
PROJ_TILE = 256
MERGE_TILE = 512
VMEM_LIMIT = 48 * 1024 * 1024

NN = ((1,), (0,))
NT = ((1,), (1,))
TN = ((0,), (0,))


def _dot(a, b, dims=NN):
    return lax.dot_general(a, b, (dims, ((), ())), preferred_element_type=F32)


def _split2(x):
    hi = x.astype(BF16)
    lo = (x - hi.astype(F32)).astype(BF16)
    return hi, lo


def _dot_exact_lhs(a_bf16, b, dims=NN):
    b1 = b.astype(BF16)
    r1 = b - b1.astype(F32)
    b2 = r1.astype(BF16)
    b3 = (r1 - b2.astype(F32)).astype(BF16)
    return _dot(a_bf16, b1, dims) + (_dot(a_bf16, b2, dims) + _dot(a_bf16, b3, dims))


def _dot_split2_rhs(a, b_bf16):
    hi, lo = _split2(a)
    return _dot(hi, b_bf16) + _dot(lo, b_bf16)


def _head_sums(a, hsum_bf16):
    return _dot(a.astype(BF16), hsum_bf16)


def _sigmoid(x):
    return 1.0 / (1.0 + jnp.exp(-x))


def _head_sum_matrix(width):
    r = lax.broadcasted_iota(jnp.int32, (width, width), 0) // HEAD
    c = lax.broadcasted_iota(jnp.int32, (width, width), 1) // HEAD
    return (r == c).astype(BF16)


def _cparams(sem):
    return pltpu.CompilerParams(dimension_semantics=sem, vmem_limit_bytes=VMEM_LIMIT)


def _proj_prep_body(*refs, has_vres):
    if has_vres:
        (x_ref, xp_ref, xn_ref, g_ref, w_ref, mu_ref, w0_ref, dup_ref, a0_ref, iup_ref, kk_ref, ka_ref,
         rk_ref, vd_ref, vu_ref, v0_ref, vf_ref,
         ub_o, ug_o, r_o, v_o, an_o, lw_o, kd_o, bb_o, bonus_o, gz_o, hb_ref, ua_ref) = refs
    else:
        (x_ref, xp_ref, xn_ref, g_ref, w_ref, mu_ref, w0_ref, dup_ref, a0_ref, iup_ref, kk_ref, ka_ref,
         rk_ref,
         ub_o, ug_o, r_o, v_o, an_o, lw_o, kd_o, bb_o, bonus_o, gz_o, hb_ref, ua_ref) = refs
    tm = x_ref.shape[0]
    i = pl.program_id(1)
    last = pl.num_programs(1) - 1
    W = RWKV_WIDTH

    def normed(ref):
        x = ref[...]
        return (x * lax.rsqrt(jnp.mean(x * x, axis=-1, keepdims=True) + RMS_EPS) * g_ref[...]).astype(BF16)

    hb_ref[0:HALO] = jnp.where(i > 0, normed(xp_ref), jnp.zeros((), BF16))
    hb_ref[HALO:HALO + tm] = normed(x_ref)
    hb_ref[HALO + tm:] = jnp.where(i < last, normed(xn_ref), jnp.zeros((), BF16))
    ua_ref[...] = _dot(hb_ref[...], w_ref[:, :A_COLS])
    hm = hb_ref[HALO:HALO + tm]

    def project_b(lo, hi):
        ub_o[:, lo:hi] = _dot(hm, w_ref[:, A_COLS + lo:A_COLS + hi])

    def project_g(lo, hi):
        ug_o[:, lo:hi] = _dot(hm, w_ref[:, A_COLS + B_COLS + lo:A_COLS + B_COLS + hi]).astype(ug_o.dtype)

    hsum = _head_sum_matrix(W)

    def shifted(lo, hi):
        u = ua_ref[HALO:HALO + tm, lo:hi]
        ext = ua_ref[:, lo:hi]
        prev = pltpu.roll(ext, 1, axis=0)[HALO:HALO + tm]
        nxt = pltpu.roll(ext, tm + 2 * HALO - 1, axis=0)[HALO:HALO + tm]
        return u + mu_ref[:, lo:hi] * (0.5 * (prev + nxt) - u)

    r = shifted(0, W)
    project_b(0, ATTN_WIDTH)
    k = shifted(W, 2 * W)
    project_b(ATTN_WIDTH, B_COLS)
    v = shifted(2 * W, 3 * W)
    project_g(0, D_MODEL // 2)
    z = shifted(3 * W, 4 * W)
    project_g(D_MODEL // 2, D_MODEL)
    low = shifted(4 * W, 4 * W + 4 * LOW_RANK)
    w_raw = w0_ref[...] + _dot(jnp.tanh(low[:, :2 * LOW_RANK]).astype(BF16), dup_ref[...])
    project_g(D_MODEL, D_MODEL + D_MODEL // 2)
    lw = -EXP_MINUS_HALF * _sigmoid(w_raw)
    a = _sigmoid(a0_ref[...] + _dot(low[:, 2 * LOW_RANK:].astype(BF16), iup_ref[...]))
    if has_vres:
        hd = _dot(hm, vd_ref[...])
        mix = _sigmoid(v0_ref[...] + _dot(hd.astype(BF16), vu_ref[...]))
        v = v + (vf_ref[...] - v) * mix
    project_g(D_MODEL + D_MODEL // 2, G_COLS)
    kk = k * kk_ref[...]
    ss = _head_sums(kk * kk, hsum)
    kk = kk * lax.rsqrt(jnp.maximum(ss, 1e-24))
    ka = ka_ref[...]
    ksum = jnp.zeros_like(k)
    for d in range(2):
        a_d = a[:, d * W:(d + 1) * W]
        kd = k * (1.0 + (a_d - 1.0) * ka)
        ksum = ksum + kd
        lw_o[d] = lw[:, d * W:(d + 1) * W]
        kd_o[d] = kd
        bb_o[d] = kk * a_d
    r_o[...] = r
    v_o[...] = v
    an_o[...] = -kk
    bonus_o[...] = _head_sums(r * ksum * rk_ref[...], hsum) * v
    gz_o[...] = z * _sigmoid(z)


def _proj_prep(x, g, w, params, vres, tm=PROJ_TILE):
    bsz, T, _ = x.shape
    W = RWKV_WIDTH
    nt = T // tm
    hb = tm // HALO
    nh = T // HALO
    const = lambda a: pl.BlockSpec(a.shape, lambda b, i: (0,) * a.ndim, pipeline_mode=pl.Buffered(1))
    tile = lambda n: pl.BlockSpec((None, tm, n), lambda b, i: (b, i, 0))
    in_specs = [
        tile(D_MODEL),
        pl.BlockSpec((None, HALO, D_MODEL), lambda b, i: (b, jnp.maximum(i * hb - 1, 0), 0)),
        pl.BlockSpec((None, HALO, D_MODEL), lambda b, i: (b, jnp.minimum((i + 1) * hb, nh - 1), 0)),
        const(g), const(w),
    ] + [const(a) for a in params]
    args = [x, x, x, g, w] + list(params)
    if vres is not None:
        vd_pad, vu_split, v0, v_first = vres
        in_specs += [const(vd_pad), const(vu_split), const(v0), tile(W)]
        args += [vd_pad, vu_split, v0, v_first]
    dir_tile = pl.BlockSpec((2, None, tm, W), lambda b, i: (0, b, i, 0))
    out_specs = [tile(B_COLS), tile(G_COLS), tile(W), tile(W), tile(W), dir_tile, dir_tile, dir_tile, tile(W), tile(W)]
    one = jax.ShapeDtypeStruct((bsz, T, W), F32)
    two = jax.ShapeDtypeStruct((2, bsz, T, W), F32)
    out_shape = [jax.ShapeDtypeStruct((bsz, T, B_COLS), F32), jax.ShapeDtypeStruct((bsz, T, G_COLS), BF16),
                 one, one, one, two, two, two, one, one]
    return pl.pallas_call(
        functools.partial(_proj_prep_body, has_vres=vres is not None),
        grid=(bsz, nt),
        in_specs=in_specs,
        out_specs=out_specs,
        out_shape=out_shape,
        scratch_shapes=[pltpu.VMEM((tm + 2 * HALO, D_MODEL), BF16), pltpu.VMEM((tm + 2 * HALO, A_COLS), F32)],
        compiler_params=_cparams(("parallel", "parallel")),
        name="proj_prep",
    )(*args)


def _wkv_step(rf_ref, vf_ref, af_ref, rb_ref, vb_ref, ab_ref, lwf_ref, kdf_ref, bbf_ref,
              lwb_ref, kdb_ref, bbb_ref, yf_ref, yb_ref, s_ref):
    bsz, C, _ = rf_ref.shape
    G = WKV_GROUP
    GW = G * HEAD
    NG = RWKV_HEADS // G
    assert C == HEAD

    row = lax.broadcasted_iota(jnp.int32, (C, GW), 0)
    lane = lax.broadcasted_iota(jnp.int32, (C, GW), 1)
    col = lane % HEAD
    eye = (row == col).astype(F32)
    same_blk = []
    size = INV_BASE
    while size < C:
        same_blk.append(((row // size) == (col // size)).astype(F32))
        size *= 2
    head_mask = [((lane // HEAD) == g).astype(BF16) for g in range(G)]
    srow = lax.broadcasted_iota(jnp.int32, (GW, GW), 0) // HEAD
    slane = lax.broadcasted_iota(jnp.int32, (GW, GW), 1) // HEAD
    state_mask = (srow == slane).astype(F32)
    trow = lax.broadcasted_iota(jnp.int32, (C, C), 0)
    tcol = lax.broadcasted_iota(jnp.int32, (C, C), 1)
    incl_d = [(col <= row).astype(F32), (col >= row).astype(F32)]
    strict_d = [(col < row).astype(F32), (col > row).astype(F32)]
    tri_d = [(tcol <= trow).astype(BF16), (tcol >= trow).astype(BF16)]

    b16 = lambda x: x.astype(BF16)
    bd = lambda xb: jnp.concatenate([xb * m for m in head_mask], axis=0)
    cat = lambda a, b, axis: jnp.concatenate([a, b], axis=axis)

    dir_refs = [(rf_ref, vf_ref, af_ref, lwf_ref, kdf_ref, bbf_ref, yf_ref),
                (rb_ref, vb_ref, ab_ref, lwb_ref, kdb_ref, bbb_ref, yb_ref)]
    chains = []
    at_g, rt_g, v_g, bt_g, kt_g, bh_g, kh_g, pt_g = [], [], [], [], [], [], [], []
    for d, (r_ref, v_ref, an_ref, lw_ref, kd_ref, bb_ref, _) in enumerate(dir_refs):
        for b in range(bsz):
            lw = lw_ref[b]
            cs = _dot_exact_lhs(tri_d[d], lw)
            tot = jnp.sum(lw, axis=0, keepdims=True)
            e_inv = jnp.exp(-cs)
            p_tot = jnp.exp(tot)
            rt = r_ref[b] * jnp.exp(cs)
            at = an_ref[b] * jnp.exp(cs - lw)
            bt = bb_ref[b] * e_inv
            kt = kd_ref[b] * e_inv
            bh = bt * p_tot
            kh = kt * p_tot
            vv = v_ref[b]
            for g in range(NG):
                sl = slice(g * GW, (g + 1) * GW)
                chains.append((d, b, g))
                for lst, val in ((at_g, at), (rt_g, rt), (v_g, vv), (bt_g, bt), (kt_g, kt), (bh_g, bh), (kh_g, kh),
                                 (pt_g, p_tot)):
                    lst.append(val[:, sl])
    strict = [strict_d[d] for d, _, _ in chains]
    incl = [incl_d[d] for d, _, _ in chains]

    at_b = [b16(a) for a in at_g]
    v_b = [b16(v) for v in v_g]
    ar_b = [cat(a, b16(r), 0) for a, r in zip(at_b, rt_g)]
    sbk = [_dot(ar, cat(bd(b16(xb)), bd(b16(xk)), 0), NT) for ar, xb, xk in zip(ar_b, bt_g, kt_g)]
    sb = [x[:, :GW] for x in sbk]
    sk = [x[:, GW:] for x in sbk]
    m_ab = [x[:C] * m for x, m in zip(sb, strict)]
    n_rb = [x[C:] * m for x, m in zip(sb, incl)]
    m_ak = [x[:C] * m for x, m in zip(sk, strict)]
    n_rk = [x[C:] * m for x, m in zip(sk, incl)]
    v_bd = [bd(v) for v in v_b]
    mv = [_dot(b16(m), vb) for m, vb in zip(m_ak, v_bd)]

    x = [eye + m * same_blk[0] for m in m_ab]
    for lvl in range(len(same_blk)):
        inner = same_blk[lvl]
        outer = same_blk[lvl + 1] if lvl + 1 < len(same_blk) else 1.0
        sel = outer - inner
        x_b = [b16(xi) for xi in x]
        t = [_dot(xb, bd(b16(m * sel))) for xb, m in zip(x_b, m_ab)]
        x = [xi + _dot(b16(ti), bd(xb)) for xi, ti, xb in zip(x, t, x_b)]

    w_ab = [_dot(b16(xi), cat(bd(a), bd(b16(m)), 1)) for xi, a, m in zip(x, at_b, mv)]
    w_a_b = [b16(w[:, :GW]) for w in w_ab]
    w_b_b = [b16(w[:, GW:]) for w in w_ab]
    zero_bd = jnp.zeros((GW, GW), BF16)
    yqc = [_dot(cat(b16(n), b16(nk), 1), cat(cat(bd(wa), bd(wb), 1), cat(zero_bd, vb, 1), 0))
           for n, nk, wa, wb, vb in zip(n_rb, n_rk, w_a_b, w_b_b, v_bd)]
    y_q = [r + x[:, :GW] for r, x in zip(rt_g, yqc)]
    y_c = [x[:, GW:] for x in yqc]
    zero_c = jnp.zeros((C, GW), BF16)
    php = [_dot(cat(cat(wa, wb, 1), cat(zero_c, v, 1), 0), cat(b16(b), b16(k), 0), TN)
           for wa, wb, v, b, k in zip(w_a_b, w_b_b, v_b, bh_g, kh_g)]
    phi = [x[:GW] * state_mask for x in php]
    psi = [x[GW:] * state_mask for x in php]
    for i, (d, b, g) in enumerate(chains):
        s0 = s_ref[d, b, g]
        s0_b = b16(s0)
        dir_refs[d][-1][b, :, g * GW:(g + 1) * GW] = _dot(b16(y_q[i]), s0_b, NT) + y_c[i]
        s_ref[d, b, g] = s0 * pt_g[i] + _dot(s0_b, b16(phi[i])) + psi[i]


def _attn_bias_init(slope_ref, bias_ref):
    rows2 = 2 * BLOCK
    qi = lax.broadcasted_iota(jnp.int32, (rows2, 3 * BLOCK), 0) % BLOCK
    kpos = lax.broadcasted_iota(jnp.int32, (rows2, 3 * BLOCK), 1) - BLOCK
    upper = lax.broadcasted_iota(jnp.int32, (rows2, 3 * BLOCK), 0) >= BLOCK
    dist = jnp.abs(qi - kpos)
    for p in range(ATTN_Q_HEADS // 2):
        slope = jnp.where(upper, slope_ref[2 * p + 1], slope_ref[2 * p])
        bias_ref[p] = jnp.where(dist <= WINDOW, -slope * dist.astype(F32), NEG_INF)


def _attn_step(n, nb, q_ref, kp_ref, kc_ref, kn_ref, qg_ref, kg_ref, slope_ref, sink_ref, o_ref, bias_ref):
    pairs = ATTN_Q_HEADS // 2
    rows2 = 2 * BLOCK
    hq = _head_sum_matrix(ATTN_WIDTH)
    hk = _head_sum_matrix(KV_WIDTH)
    lane = lax.broadcasted_iota(jnp.int32, (BLOCK, LANES), 1)
    low = lane < HEAD
    srow = lax.broadcasted_iota(jnp.int32, (rows2, 1), 0)

    q = q_ref[:, :ATTN_WIDTH]
    q = q * lax.rsqrt(_head_sums(q * q, hq) * (1.0 / HEAD) + RMS_EPS) * (qg_ref[...] * (HEAD ** -0.5))

    def dup(x, g):
        rolled = pltpu.roll(x, HEAD, axis=1)
        return (jnp.where(low, x, rolled) if g == 0 else jnp.where(low, rolled, x)).astype(BF16)

    k_dup, v_dup = [[], []], [[], []]
    for ref in (kp_ref, kc_ref, kn_ref):
        kx = ref[:, :KV_WIDTH]
        kx = kx * lax.rsqrt(_head_sums(kx * kx, hk) * (1.0 / HEAD) + RMS_EPS) * kg_ref[...]
        vx = ref[:, KV_WIDTH:]
        for g in range(ATTN_KV_HEADS):
            k_dup[g].append(dup(kx, g))
            v_dup[g].append(dup(vx, g))
    k_cat = [jnp.concatenate(k_dup[g], axis=0) for g in range(ATTN_KV_HEADS)]
    v_cat = [jnp.concatenate(v_dup[g], axis=0) for g in range(ATTN_KV_HEADS)]
    edge_prev = jnp.where(n > 0, 0.0, NEG_INF)
    edge_next = jnp.where(n < nb - 1, 0.0, NEG_INF)

    for p in range(pairs):
        g = (2 * p) // ATTN_GROUP
        qp = q[:, p * LANES:(p + 1) * LANES]
        qs = jnp.concatenate([jnp.where(low, qp, 0.0), jnp.where(low, 0.0, qp)], axis=0).astype(BF16)
        s = _dot(qs, k_cat[g], NT) + bias_ref[p]
        s0 = s[:, :BLOCK] + edge_prev
        s1 = s[:, BLOCK:2 * BLOCK]
        s2 = s[:, 2 * BLOCK:] + edge_next
        sink = jnp.where(srow >= BLOCK, sink_ref[2 * p + 1], sink_ref[2 * p])
        m = jnp.maximum(jnp.max(jnp.maximum(jnp.maximum(s0, s1), s2), axis=-1, keepdims=True), sink)
        e0, e1, e2 = jnp.exp(s0 - m), jnp.exp(s1 - m), jnp.exp(s2 - m)
        den = jnp.sum(e0 + e1 + e2, axis=-1, keepdims=True) + jnp.exp(sink - m)
        e = jnp.concatenate([e0, e1, e2], axis=1).astype(BF16)
        o2 = _dot(e, v_cat[g]) * (1.0 / den)
        o = jnp.where(low, o2[:BLOCK], o2[BLOCK:])
        z_lo = ATTN_WIDTH + 2 * KV_WIDTH + p * LANES
        zp = q_ref[:, z_lo:z_lo + LANES]
        og = o * (zp * _sigmoid(zp))
        o_ref[:, p * LANES:(p + 1) * LANES] = og.astype(o_ref.dtype)


N_WKV_IN = 12
N_ATTN_IN = 8


def _wkv_attn_body(*refs, nb):
    wkv_in = refs[:N_WKV_IN]
    attn_in = refs[N_WKV_IN:N_WKV_IN + N_ATTN_IN]
    yf_ref, yb_ref, o_ref, s_ref, bias_ref = refs[N_WKV_IN + N_ATTN_IN:]
    c = pl.program_id(0)

    @pl.when(c == 0)
    def _():
        s_ref[...] = jnp.zeros_like(s_ref)
        _attn_bias_init(attn_in[6], bias_ref)

    _wkv_step(*wkv_in, yf_ref, yb_ref, s_ref)
    _attn_step(c % nb, nb, *attn_in, o_ref, bias_ref)


def _wkv_attn(r, v, an, lw, kd, bb, ub, q_g, k_g, slopes, sink):
    bsz, T, W = r.shape
    C = CHUNK
    nc = T // C
    nb = T // BLOCK
    assert nc == bsz * nb
    gw = WKV_GROUP * HEAD
    fwd = pl.BlockSpec((bsz, C, W), lambda c: (0, c, 0))
    bwd = pl.BlockSpec((bsz, C, W), lambda c: (0, nc - 1 - c, 0))
    fwd_d = pl.BlockSpec((None, bsz, C, W), lambda c: (0, 0, c, 0))
    bwd_d = pl.BlockSpec((None, bsz, C, W), lambda c: (1, 0, nc - 1 - c, 0))
    kv_blk = ATTN_WIDTH // (2 * KV_WIDTH)
    kv = lambda f: pl.BlockSpec((None, BLOCK, 2 * KV_WIDTH), f)
    smem = pl.BlockSpec(memory_space=pltpu.SMEM)
    attn_specs = [
        pl.BlockSpec((None, BLOCK, B_COLS), lambda c: (c // nb, c % nb, 0)),
        kv(lambda c: (c // nb, jnp.maximum(c % nb - 1, 0), kv_blk)),
        kv(lambda c: (c // nb, c % nb, kv_blk)),
        kv(lambda c: (c // nb, jnp.minimum(c % nb + 1, nb - 1), kv_blk)),
        pl.BlockSpec((1, ATTN_WIDTH), lambda c: (0, 0)),
        pl.BlockSpec((1, KV_WIDTH), lambda c: (0, 0)),
        smem, smem,
    ]
    out = jax.ShapeDtypeStruct((bsz, T, W), F32)
    return pl.pallas_call(
        functools.partial(_wkv_attn_body, nb=nb),
        grid=(nc,),
        in_specs=[fwd, fwd, fwd, bwd, bwd, bwd, fwd_d, fwd_d, fwd_d, bwd_d, bwd_d, bwd_d] + attn_specs,
        out_specs=[fwd, bwd, pl.BlockSpec((None, BLOCK, ATTN_WIDTH), lambda c: (c // nb, c % nb, 0))],
        out_shape=[out, out, jax.ShapeDtypeStruct((bsz, T, ATTN_WIDTH), BF16)],
        scratch_shapes=[pltpu.VMEM((2, bsz, RWKV_HEADS // WKV_GROUP, gw, gw), F32),
                        pltpu.VMEM((ATTN_Q_HEADS // 2, 2 * BLOCK, 3 * BLOCK), F32)],
        compiler_params=_cparams(("arbitrary",)),
        name="wkv_attn",
    )(r, v, an, r, v, an, lw, kd, bb, lw, kd, bb, ub, ub, ub, ub, q_g, k_g, slopes, sink)


def _merge_out_body(x_ref, yf_ref, yb_ref, bonus_ref, gz_ref, ob_ref, ug_ref, p_ref, lnw_ref, lnb_ref,
                    pa_ref, pb_ref, wo_ref, pg_ref, gw_ref, pp_ref, o_ref):
    hsum = _head_sum_matrix(RWKV_WIDTH)
    y = yf_ref[...] + yb_ref[...]
    mean = _dot_split2_rhs(y, hsum) * (1.0 / HEAD)
    yc = y - mean
    var = _head_sums(yc * yc, hsum) * (1.0 / HEAD)
    yn = yc * lax.rsqrt(var + GN_EPS) * lnw_ref[...] + lnb_ref[...]
    o_a = (yn + bonus_ref[...]) * gz_ref[...]
    y_a = _dot(o_a.astype(BF16), pa_ref[...])
    y_b = _dot(ob_ref[...], pb_ref[...])
    merged = (_sigmoid(ug_ref[:, :D_MODEL].astype(F32)) * y_a
              + _sigmoid(ug_ref[:, D_MODEL:].astype(F32)) * y_b)
    x1 = x_ref[...] + _dot(merged.astype(BF16), wo_ref[...])
    ple = _dot(p_ref[...].astype(BF16), pp_ref[...])
    hn = x1 * lax.rsqrt(jnp.mean(x1 * x1, axis=-1, keepdims=True) + RMS_EPS) * pg_ref[...]
    o_ref[...] = x1 + _sigmoid(_dot(hn.astype(BF16), gw_ref[...])) * ple


def _merge_out(x2, yf, yb, bonus, gz, ob, ug, p3, layer, lnw, lnb, pa, pb, wo, pg, gw, pp, tm=MERGE_TILE):
    rows = x2.shape[0]
    W = RWKV_WIDTH
    tile = lambda n: pl.BlockSpec((tm, n), lambda i: (i, 0))
    full = lambda a: pl.BlockSpec(a.shape, lambda i: (0,) * a.ndim)
    return pl.pallas_call(
        _merge_out_body,
        grid=(rows // tm,),
        in_specs=[tile(D_MODEL), tile(W), tile(W), tile(W), tile(W), tile(ATTN_WIDTH),
                  tile(G_COLS), pl.BlockSpec((None, tm, PLE_DIM), lambda i: (layer, i, 0)), full(lnw), full(lnb), full(pa), full(pb), full(wo), full(pg),
                  full(gw), full(pp)],
        out_specs=tile(D_MODEL),
        out_shape=jax.ShapeDtypeStruct((rows, D_MODEL), F32),
        compiler_params=_cparams(("parallel",)),
        name="merge_out",
    )(x2, yf, yb, bonus, gz, ob, ug, p3, lnw, lnb, pa, pb, wo, pg, gw, pp)


def _block_diag2(m):
    z = jnp.zeros_like(m[0])
    return jnp.concatenate([jnp.concatenate([m[0], z], axis=1), jnp.concatenate([z, m[1]], axis=1)], axis=0)


def kernel(x, p, norm_g, w_in, shift_mu, decay_w0, decay_up, iclr_a0, iclr_up, vres_down, vres_up, vres_v0, k_k, k_a, r_k, ln_x_w, ln_x_b, q_norm_g, k_norm_g, sink, proj_a, proj_b, w_out, ple_norm_g, ple_gate_w, ple_proj):
    bsz, T, _ = x.shape
    depth = w_in.shape[0]
    rows = bsz * T
    W = RWKV_WIDTH
    slopes = jnp.asarray(2.0 ** (-8.0 * jnp.arange(1, ATTN_Q_HEADS + 1, dtype=F32) / ATTN_Q_HEADS), F32)
    v_first = None
    for i in range(depth):
        params = [shift_mu[i].reshape(1, A_COLS), decay_w0[i].reshape(1, 2 * W), _block_diag2(decay_up[i]).astype(BF16),
                  iclr_a0[i].reshape(1, 2 * W), _block_diag2(iclr_up[i]).astype(BF16), k_k[i].reshape(1, W),
                  k_a[i].reshape(1, W), r_k[i].reshape(1, W)]
        vres = None
        if i > 0:
            vu_pad = jnp.concatenate([vres_up[i - 1], jnp.zeros((LANES - VRES_RANK, W), F32)], axis=0)
            vd_pad = jnp.concatenate([vres_down[i - 1], jnp.zeros((D_MODEL, LANES - VRES_RANK), F32)], axis=1)
            vres = (vd_pad.astype(BF16), vu_pad.astype(BF16), vres_v0[i - 1].reshape(1, W), v_first)
        ub, ug, r, v, an, lw, kd, bb, bonus, gz = _proj_prep(x, norm_g[i].reshape(1, D_MODEL), w_in[i].astype(BF16),
                                                             params, vres)
        if i == 0:
            v_first = v
        yf, yb, ob = _wkv_attn(r, v, an, lw, kd, bb, ub, jnp.tile(q_norm_g[i], ATTN_Q_HEADS).reshape(1, ATTN_WIDTH),
                               jnp.tile(k_norm_g[i], ATTN_KV_HEADS).reshape(1, KV_WIDTH), slopes, sink[i])
        x2 = _merge_out(x.reshape(rows, D_MODEL), yf.reshape(rows, W), yb.reshape(rows, W), bonus.reshape(rows, W),
                        gz.reshape(rows, W), ob.reshape(rows, ATTN_WIDTH), ug.reshape(rows, G_COLS),
                        p.reshape(depth, rows, PLE_DIM), i, ln_x_w[i].reshape(1, W), ln_x_b[i].reshape(1, W),
                        proj_a[i].astype(BF16), proj_b[i].astype(BF16), w_out[i].astype(BF16),
                        ple_norm_g[i].reshape(1, D_MODEL), ple_gate_w[i].astype(BF16), ple_proj[i].astype(BF16))
        x = x2.reshape(bsz, T, D_MODEL)
    return x
```
